```python
import jax, jax.numpy as jnp
from jax import lax
import numpy as np

D_MODEL = 1024
BATCH = 2
SEQ = 8192
DEPTH = 1

MEM_LEN = 256

NSA_HEADS = 8
NSA_HEAD_DIM = 64
NSA_KV_GROUPS = 2
NSA_HEADS_PER_GROUP = NSA_HEADS // NSA_KV_GROUPS
NSA_WIDTH = NSA_HEADS * NSA_HEAD_DIM
KV_WIDTH = NSA_KV_GROUPS * NSA_HEAD_DIM
NSA_GATE_WIDTH = NSA_HEADS * 3
CMP_BLOCK = 32
CMP_STRIDE = 16
CMP_HIDDEN = 256
SLC_BLOCK = 64
SLC_TOPN = 16
WINDOW = 512
Q_BLOCK = 128
FORCE_BONUS = 1e4

GMLP_WIDTH = 512
GMLP_GROUPS = 4
GMLP_GROUP_DIM = GMLP_WIDTH // GMLP_GROUPS
GMLP_CHUNK = 128

N_BRANCHES = 2
IN_SPLITS = (NSA_WIDTH, KV_WIDTH, KV_WIDTH, KV_WIDTH, KV_WIDTH, KV_WIDTH, KV_WIDTH,
             NSA_GATE_WIDTH, GMLP_WIDTH, GMLP_WIDTH, N_BRANCHES * D_MODEL)
IN_PROJ_WIDTH = NSA_WIDTH + 6 * KV_WIDTH + NSA_GATE_WIDTH + 2 * GMLP_WIDTH + N_BRANCHES * D_MODEL

XATTN_HEADS = 4
XATTN_HEAD_DIM = D_MODEL // XATTN_HEADS

PEER_HEADS = 8
PEER_KEYS = 128
PEER_EXPERTS = PEER_KEYS * PEER_KEYS
PEER_QUERY_DIM = 256
PEER_HALF = PEER_QUERY_DIM // 2
PEER_TOPK = 16
PEER_TOKEN_CHUNK = 128

NORM_EPS = 1e-6
NEG_INF = -1e30
TINY = 1e-30

kernel_name = "hybrid_nsa_gmlp_peer_block"


def rms_norm(x, g):
    xf = x.astype(jnp.float32)
    y = xf * lax.rsqrt(jnp.mean(xf * xf, axis=-1, keepdims=True) + NORM_EPS)
    return (y * g.astype(jnp.float32)).astype(x.dtype)


def masked_softmax(s, mask):
    s = jnp.where(mask, s.astype(jnp.float32), NEG_INF)
    m = jnp.max(s, axis=-1, keepdims=True)
    p = jnp.where(mask, jnp.exp(s - m), 0.0)
    return p / jnp.maximum(jnp.sum(p, axis=-1, keepdims=True), TINY)


def gelu(x):
    return jax.nn.gelu(x, approximate=False)


def compress_tokens(tok, pe, w1, b1, w2):
    B, S, G, dk = tok.shape
    r = CMP_BLOCK // CMP_STRIDE
    n_chunks = S // CMP_STRIDE
    n_cmp = n_chunks - r + 1
    chunks = tok.reshape(B, n_chunks, CMP_STRIDE, G, dk)
    blocks = jnp.concatenate([chunks[:, j:j + n_cmp] for j in range(r)], axis=2)
    blocks = blocks + pe[None, None, :, None, :]
    flat = blocks.transpose(0, 1, 3, 2, 4).reshape(B, n_cmp, G, CMP_BLOCK * dk)
    return gelu(flat @ w1 + b1) @ w2


def nsa_attention(q, kc, vc, ks, vs, kw, vw, gates,
                  cmp_pe_k, cmp_w1_k, cmp_b1_k, cmp_w2_k,
                  cmp_pe_v, cmp_w1_v, cmp_b1_v, cmp_w2_v):
    B, S, G, R, dk = q.shape
    scale = dk ** -0.5
    k_cmp = compress_tokens(kc, cmp_pe_k, cmp_w1_k, cmp_b1_k, cmp_w2_k)
    v_cmp = compress_tokens(vc, cmp_pe_v, cmp_w1_v, cmp_b1_v, cmp_w2_v)
    n_cmp = k_cmp.shape[1]
    n_slc = S // SLC_BLOCK
    n_sel = min(SLC_TOPN, n_slc)
    n_qblk = S // Q_BLOCK

    k_blocks = ks.reshape(B, n_slc, SLC_BLOCK, G, dk).transpose(0, 3, 1, 2, 4)
    v_blocks = vs.reshape(B, n_slc, SLC_BLOCK, G, dk).transpose(0, 3, 1, 2, 4)
    pad = ((0, 0), (WINDOW, 0), (0, 0), (0, 0))
    kw_pad = jnp.pad(kw, pad)
    vw_pad = jnp.pad(vw, pad)

    c_start = jnp.arange(n_cmp) * CMP_STRIDE
    j_start = jnp.arange(n_slc) * SLC_BLOCK
    ov = jnp.clip(jnp.minimum(c_start[:, None] + CMP_BLOCK, j_start[None, :] + SLC_BLOCK)
                  - jnp.maximum(c_start[:, None], j_start[None, :]), 0, None)
    overlap = ov.astype(jnp.float32) / CMP_BLOCK
    cmp_end = c_start + CMP_BLOCK - 1
    blk = jnp.arange(n_slc)
    b_ix = jnp.arange(B)[:, None, None, None]
    g_ix = jnp.arange(G)[None, :, None, None]

    def one_block(i):
        q0 = i * Q_BLOCK
        qb = lax.dynamic_slice_in_dim(q, q0, Q_BLOCK, axis=1)
        gb = lax.dynamic_slice_in_dim(gates, q0, Q_BLOCK, axis=1)
        t = q0 + jnp.arange(Q_BLOCK)

        s_c = jnp.einsum('bqgrd,bcgd->bgrqc', qb, k_cmp) * scale
        mask_c = cmp_end[None, :] <= t[:, None]
        p_c = masked_softmax(s_c, mask_c)
        o_c = jnp.einsum('bgrqc,bcgd->bqgrd', p_c.astype(v_cmp.dtype), v_cmp)

        imp = jnp.einsum('bgrqc,cj->bgqj', p_c, overlap)
        allowed = blk[None, :] * SLC_BLOCK <= t[:, None]
        cur = (t // SLC_BLOCK)[:, None]
        forced = (blk[None, :] == 0) | (blk[None, :] == cur) | (blk[None, :] == cur - 1)
        score = jnp.where(forced & allowed, FORCE_BONUS, jnp.where(allowed, imp, NEG_INF))
        top_score, idx = lax.top_k(score, n_sel)
        valid_blk = top_score > 0.5 * NEG_INF

        kb = k_blocks[b_ix, g_ix, idx]
        vb = v_blocks[b_ix, g_ix, idx]
        n_tok = n_sel * SLC_BLOCK
        s_s = jnp.einsum('bqgrd,bgqnkd->bgrqnk', qb, kb) * scale
        tok_pos = idx[..., None] * SLC_BLOCK + jnp.arange(SLC_BLOCK)
        mask_s = valid_blk[..., None] & (tok_pos <= t[None, None, :, None, None])
        p_s = masked_softmax(s_s.reshape(B, G, R, Q_BLOCK, n_tok),
                             mask_s.reshape(B, G, 1, Q_BLOCK, n_tok))
        o_s = jnp.einsum('bgrqm,bgqmd->bqgrd', p_s.astype(vb.dtype),
                         vb.reshape(B, G, Q_BLOCK, n_tok, dk))

        kwb = lax.dynamic_slice_in_dim(kw_pad, q0, Q_BLOCK + WINDOW, axis=1)
        vwb = lax.dynamic_slice_in_dim(vw_pad, q0, Q_BLOCK + WINDOW, axis=1)
        s_pos = q0 - WINDOW + jnp.arange(Q_BLOCK + WINDOW)
        diff = t[:, None] - s_pos[None, :]
        mask_w = (s_pos >= 0)[None, :] & (diff >= 0) & (diff < WINDOW)
        s_w = jnp.einsum('bqgrd,bkgd->bgrqk', qb, kwb) * scale
        p_w = masked_softmax(s_w, mask_w)
        o_w = jnp.einsum('bgrqk,bkgd->bqgrd', p_w.astype(vwb.dtype), vwb)

        return gb[..., 0:1] * o_c + gb[..., 1:2] * o_s + gb[..., 2:3] * o_w

    out = lax.map(one_block, jnp.arange(n_qblk))
    return out.transpose(1, 0, 2, 3, 4, 5).reshape(B, S, G * R * dk)


def chunked_gmlp(u_raw, v_raw, ln_g, ln_b, ws, bs):
    B, S, W = u_raw.shape
    u = gelu(u_raw)
    v = gelu(v_raw).astype(jnp.float32)
    mu = jnp.mean(v, axis=-1, keepdims=True)
    var = jnp.mean(jnp.square(v - mu), axis=-1, keepdims=True)
    v = ((v - mu) * lax.rsqrt(var + NORM_EPS) * ln_g + ln_b).astype(u.dtype)
    v = v.reshape(B, S // GMLP_CHUNK, GMLP_CHUNK, GMLP_GROUPS, GMLP_GROUP_DIM)
    causal = jnp.tril(jnp.ones((GMLP_CHUNK, GMLP_CHUNK), dtype=ws.dtype))
    s = jnp.einsum('gts,bcsgd->bctgd', ws * causal, v) + bs.T[None, None, :, :, None]
    return u * s.reshape(B, S, W)


def memory_cross_attention(hn, mn, w_xq, w_xkv, w_xo):
    B, S, D = hn.shape
    M = mn.shape[1]
    q = (hn @ w_xq).reshape(B, S, XATTN_HEADS, XATTN_HEAD_DIM)
    k, v = jnp.split(mn @ w_xkv, 2, axis=-1)
    k = k.reshape(B, M, XATTN_HEADS, XATTN_HEAD_DIM)
    v = v.reshape(B, M, XATTN_HEADS, XATTN_HEAD_DIM)
    s = jnp.einsum('bshd,bmhd->bhsm', q, k) * (XATTN_HEAD_DIM ** -0.5)
    p = jax.nn.softmax(s.astype(jnp.float32), axis=-1).astype(v.dtype)
    o = jnp.einsum('bhsm,bmhd->bshd', p, v).reshape(B, S, D)
    return o @ w_xo


def peer_ffn(hn, w_peer_q, sub_keys, peer_u, peer_v):
    B, S, D = hn.shape
    q = (hn @ w_peer_q).reshape(B, S, PEER_HEADS, 2, PEER_HALF)
    sc = jnp.einsum('bshpd,hpkd->bshpk', q, sub_keys).astype(jnp.float32)
    half_s, half_i = lax.top_k(sc, PEER_TOPK)
    cand_s = half_s[..., 0, :, None] + half_s[..., 1, None, :]
    cand_i = half_i[..., 0, :, None] * PEER_KEYS + half_i[..., 1, None, :]
    cand_s = cand_s.reshape(B, S, PEER_HEADS, PEER_TOPK * PEER_TOPK)
    cand_i = cand_i.reshape(B, S, PEER_HEADS, PEER_TOPK * PEER_TOPK)
    best_s, pos = lax.top_k(cand_s, PEER_TOPK)
    expert = jnp.take_along_axis(cand_i, pos, axis=-1)
    gate = jax.nn.softmax(best_s, axis=-1).astype(hn.dtype)

    n_e = PEER_HEADS * PEER_TOPK
    n_chunk = S // PEER_TOKEN_CHUNK
    def to_chunks(a):
        return jnp.moveaxis(a.reshape(B, n_chunk, PEER_TOKEN_CHUNK, *a.shape[2:]), 1, 0)
    xs = to_chunks(hn)
    es = to_chunks(expert.reshape(B, S, n_e))
    gs = to_chunks(gate.reshape(B, S, n_e))

    def one_chunk(args):
        xc, ec, gc = args
        u = peer_u[ec]
        a = gelu(jnp.einsum('btd,bted->bte', xc, u))
        return jnp.einsum('bte,bted->btd', gc * a, peer_v[ec])

    out = lax.map(one_chunk, (xs, es, gs))
    return jnp.moveaxis(out, 0, 1).reshape(B, S, D)


def setup_inputs(seed: int = 0) -> dict:
    key = jax.random.key(seed)
    ks = jax.random.split(key, 32)
    f32 = jnp.float32
    D = D_MODEL
    dk = NSA_HEAD_DIM

    def nrm(k, shape, scale):
        return jax.random.normal(k, shape, f32) * scale

    def gain(k, n):
        return 1.0 + 0.02 * jax.random.normal(k, (n,), f32)

    return {
        "x": nrm(ks[0], (BATCH, SEQ, D), 1.0),
        "mem": nrm(ks[1], (BATCH, MEM_LEN, D), 1.0),
        "g_mix": gain(ks[2], D),
        "w_in": nrm(ks[3], (D, IN_PROJ_WIDTH), D ** -0.5),
        "cmp_pe_k": nrm(ks[4], (CMP_BLOCK, dk), 0.02),
        "cmp_w1_k": nrm(ks[5], (CMP_BLOCK * dk, CMP_HIDDEN), (CMP_BLOCK * dk) ** -0.5),
        "cmp_b1_k": nrm(ks[6], (CMP_HIDDEN,), 0.01),
        "cmp_w2_k": nrm(ks[7], (CMP_HIDDEN, dk), CMP_HIDDEN ** -0.5),
        "cmp_pe_v": nrm(ks[8], (CMP_BLOCK, dk), 0.02),
        "cmp_w1_v": nrm(ks[9], (CMP_BLOCK * dk, CMP_HIDDEN), (CMP_BLOCK * dk) ** -0.5),
        "cmp_b1_v": nrm(ks[10], (CMP_HIDDEN,), 0.01),
        "cmp_w2_v": nrm(ks[11], (CMP_HIDDEN, dk), CMP_HIDDEN ** -0.5),
        "gmlp_ln_g": gain(ks[12], GMLP_WIDTH),
        "gmlp_ln_b": nrm(ks[13], (GMLP_WIDTH,), 0.01),
        "gmlp_ws": nrm(ks[14], (GMLP_GROUPS, GMLP_CHUNK, GMLP_CHUNK), GMLP_CHUNK ** -0.5),
        "gmlp_bs": 1.0 + 0.1 * jax.random.normal(ks[15], (GMLP_GROUPS, GMLP_CHUNK), f32),
        "w_nsa_out": nrm(ks[16], (NSA_WIDTH, D), NSA_WIDTH ** -0.5),
        "w_gmlp_out": nrm(ks[17], (GMLP_WIDTH, D), GMLP_WIDTH ** -0.5),
        "w_mix_out": nrm(ks[18], (D, D), D ** -0.5),
        "g_xattn": gain(ks[19], D),
        "g_mem": gain(ks[20], D),
        "w_xq": nrm(ks[21], (D, D), D ** -0.5),
        "w_xkv": nrm(ks[22], (D, 2 * D), D ** -0.5),
        "w_xo": nrm(ks[23], (D, D), D ** -0.5),
        "g_peer": gain(ks[24], D),
        "w_peer_q": nrm(ks[25], (D, PEER_HEADS * PEER_QUERY_DIM), D ** -0.5),
        "peer_sub_keys": nrm(ks[26], (PEER_HEADS, 2, PEER_KEYS, PEER_HALF), PEER_HALF ** -0.5),
        "peer_u": nrm(ks[27], (PEER_EXPERTS, D), D ** -0.5),
        "peer_v": nrm(ks[28], (PEER_EXPERTS, D), (PEER_HEADS * PEER_TOPK) ** -0.5),
        "g_final": gain(ks[29], D),
    }


def reference(x, mem, g_mix, w_in,
              cmp_pe_k, cmp_w1_k, cmp_b1_k, cmp_w2_k,
              cmp_pe_v, cmp_w1_v, cmp_b1_v, cmp_w2_v,
              gmlp_ln_g, gmlp_ln_b, gmlp_ws, gmlp_bs,
              w_nsa_out, w_gmlp_out, w_mix_out,
              g_xattn, g_mem, w_xq, w_xkv, w_xo,
              g_peer, w_peer_q, peer_sub_keys, peer_u, peer_v,
              g_final):
    B, S, D = x.shape
    G, R, dk = NSA_KV_GROUPS, NSA_HEADS_PER_GROUP, NSA_HEAD_DIM
    split_points = [int(p) for p in np.cumsum(IN_SPLITS)[:-1]]
    h = x
    for _ in range(DEPTH):
        hn = rms_norm(h, g_mix)
        proj = hn @ w_in
        (q, kc, vc, ks_, vs_, kw, vw, nsa_gate_raw,
         u_raw, v_raw, merge_raw) = jnp.split(proj, split_points, axis=-1)
        q = q.reshape(B, S, G, R, dk)
        kc, vc, ks_, vs_, kw, vw = [a.reshape(B, S, G, dk) for a in (kc, vc, ks_, vs_, kw, vw)]
        nsa_gates = jax.nn.sigmoid(nsa_gate_raw).reshape(B, S, G, R, 3)
        o_nsa = nsa_attention(q, kc, vc, ks_, vs_, kw, vw, nsa_gates,
                              cmp_pe_k, cmp_w1_k, cmp_b1_k, cmp_w2_k,
                              cmp_pe_v, cmp_w1_v, cmp_b1_v, cmp_w2_v)
        o_gmlp = chunked_gmlp(u_raw, v_raw, gmlp_ln_g, gmlp_ln_b, gmlp_ws, gmlp_bs)
        mg = jax.nn.sigmoid(merge_raw).reshape(B, S, N_BRANCHES, D)
        y = mg[:, :, 0] * (o_nsa @ w_nsa_out) + mg[:, :, 1] * (o_gmlp @ w_gmlp_out)
        h = h + y @ w_mix_out
        h = h + memory_cross_attention(rms_norm(h, g_xattn), rms_norm(mem, g_mem),
                                       w_xq, w_xkv, w_xo)
        h = h + peer_ffn(rms_norm(h, g_peer), w_peer_q, peer_sub_keys, peer_u, peer_v)
    return rms_norm(h, g_final)
```

```python
import functools

import jax
import jax.numpy as jnp
from jax import lax
from jax.experimental import pallas as pl
from jax.experimental.pallas import tpu as pltpu

F32 = jnp.float32
BF16 = jnp.bfloat16
I32 = jnp.int32

NORM_EPS = 1e-6
NEG_INF = -1e30
TINY = 1e-30

NSA_HEADS = 8
NSA_HEAD_DIM = 64
NSA_GROUPS = 2
NSA_REP = NSA_HEADS // NSA_GROUPS
CMP_BLOCK = 32
CMP_STRIDE = 16
SLC_BLOCK = 64
SLC_TOPN = 16
WINDOW = 512
Q_BLOCK = 128
FORCE_BONUS = 1e4
GMLP_GROUPS = 4
GMLP_CHUNK = 128
XATTN_HEADS = 4
PEER_HEADS = 8
PEER_KEYS = 128
PEER_TOPK = 16

LANES = 128
SUBLANES = 8
VMEM_LIMIT_BYTES = 56 * 1024 * 1024

PROJ_ROWS = 512
MIX_ROWS = 256
SEL_KEYS = 512
ROUTE_TOKENS = 256
UP_TOKENS = 512
DOWN_TOKENS = 256
EXPERT_ROWS = 8
W_ROW_STRIDE = PEER_KEYS + SUBLANES


def _params(n_axes):
    return pltpu.CompilerParams(
        dimension_semantics=("arbitrary",) * n_axes,
        vmem_limit_bytes=VMEM_LIMIT_BYTES,
    )


def _rms(x, g):
    return x * lax.rsqrt(jnp.mean(x * x, axis=-1, keepdims=True) + NORM_EPS) * g


def _gelu(x):
    return 0.5 * x * (1.0 + lax.erf(x * 0.7071067811865476))


def _sigmoid(x):
    return 1.0 / (1.0 + jnp.exp(-x))


def _dot(a, b):
    return jnp.dot(a, b, preferred_element_type=F32)


def _dot_nt(a, b):
    return lax.dot_general(a, b, (((1,), (1,)), ((), ())), preferred_element_type=F32)


def _masked_softmax(s, mask):
    s = jnp.where(mask, s, NEG_INF)
    m = jnp.max(s, axis=-1, keepdims=True)
    p = jnp.where(mask, jnp.exp(s - m), 0.0)
    return p / jnp.maximum(jnp.sum(p, axis=-1, keepdims=True), TINY)


_C_Q = 0
_C_KV = 512
_C_GATE = 1280
_C_U = 1536
_C_V = 2048
_C_MERGE = 2560
_C_END = 4608


def _in_proj_kernel(x_ref, g_ref, w_ref, q_ref, kvc_ref, ks_ref, vs_ref, kw_ref, vw_ref,
                    gate_ref, u_ref, v_ref, mg_ref):
    hn = _rms(x_ref[...], g_ref[...]).astype(BF16)

    def proj(a, b):
        return _dot(hn, w_ref[:, a:b])

    dk = NSA_HEAD_DIM
    pq = proj(_C_Q, _C_KV) * (dk ** -0.5)
    for g in range(NSA_GROUPS):
        for r in range(NSA_REP):
            c = (g * NSA_REP + r) * dk
            q_ref[0, g, r] = pq[:, c:c + dk].astype(BF16)
    pk = proj(_C_KV, _C_GATE)
    for g in range(NSA_GROUPS):
        kvc_ref[0, 0, g] = pk[:, 0 * 128 + g * dk:0 * 128 + (g + 1) * dk]
        kvc_ref[1, 0, g] = pk[:, 1 * 128 + g * dk:1 * 128 + (g + 1) * dk]
        ks_ref[0, g] = pk[:, 2 * 128 + g * dk:2 * 128 + (g + 1) * dk].astype(BF16)
        vs_ref[0, g] = pk[:, 3 * 128 + g * dk:3 * 128 + (g + 1) * dk].astype(BF16)
        kw_ref[0, g] = pk[:, 4 * 128 + g * dk:4 * 128 + (g + 1) * dk].astype(BF16)
        vw_ref[0, g] = pk[:, 5 * 128 + g * dk:5 * 128 + (g + 1) * dk].astype(BF16)
        gate_ref[0, g] = _sigmoid(proj(_C_GATE + g * 128, _C_GATE + (g + 1) * 128))
    u_ref[...] = proj(_C_U, _C_V)
    v_ref[...] = proj(_C_V, _C_MERGE)
    mg_ref[...] = _sigmoid(proj(_C_MERGE, _C_END))


def _in_proj(x2, g_mix, w_re, B, S):
    T, D = x2.shape
    tm = PROJ_ROWS
    per_b = S // tm
    G, R, dk = NSA_GROUPS, NSA_REP, NSA_HEAD_DIM

    def tok(i):
        return (i, 0)

    def bgs(i):
        return (i // per_b, 0, i % per_b, 0)

    out_shape = (
        jax.ShapeDtypeStruct((B, G, R, S, dk), BF16),
        jax.ShapeDtypeStruct((2, B, G, S, dk), F32),
        jax.ShapeDtypeStruct((B, G, S, dk), BF16),
        jax.ShapeDtypeStruct((B, G, S, dk), BF16),
        jax.ShapeDtypeStruct((B, G, S, dk), BF16),
        jax.ShapeDtypeStruct((B, G, S, dk), BF16),
        jax.ShapeDtypeStruct((B, G, S, LANES), F32),
        jax.ShapeDtypeStruct((T, 512), F32),
        jax.ShapeDtypeStruct((T, 512), F32),
        jax.ShapeDtypeStruct((T, 2 * D), F32),
    )
    kv_spec = pl.BlockSpec((1, G, tm, dk), bgs)
    out_specs = (
        pl.BlockSpec((1, G, R, tm, dk), lambda i: (i // per_b, 0, 0, i % per_b, 0)),
        pl.BlockSpec((2, 1, G, tm, dk), lambda i: (0, i // per_b, 0, i % per_b, 0)),
        kv_spec, kv_spec, kv_spec, kv_spec,
        pl.BlockSpec((1, G, tm, LANES), bgs),
        pl.BlockSpec((tm, 512), tok),
        pl.BlockSpec((tm, 512), tok),
        pl.BlockSpec((tm, 2 * D), tok),
    )
    return pl.pallas_call(
        _in_proj_kernel,
        grid=(T // tm,),
        in_specs=[
            pl.BlockSpec((tm, D), tok),
            pl.BlockSpec((1, D), lambda i: (0, 0)),
            pl.BlockSpec((D, _C_END), lambda i: (0, 0)),
        ],
        out_specs=out_specs,
        out_shape=out_shape,
        compiler_params=_params(1),
        name="in_proj",
    )(x2, g_mix.reshape(1, D), w_re)


def _compress_kernel(x_ref, pe_ref, w1_ref, b1_ref, w2_ref, o_ref):
    x = x_ref[0, 0]
    half = x.shape[1]
    lo = (x + pe_ref[0, 0:1, :]).astype(BF16)
    hi = (x + pe_ref[0, 1:2, :]).astype(BF16)
    p = _dot(lo, w1_ref[0, :half, :])
    q = _dot(hi, w1_ref[0, half:, :])
    n = x.shape[0]
    h = p + pltpu.roll(q, n - 1, 0) + b1_ref[0]
    o_ref[0, 0] = _dot(_gelu(h).astype(BF16), w2_ref[0]).astype(BF16)


def _compress(kvc, pe, w1, b1, w2):
    _, BG, n_chunks, width = kvc.shape
    hidden = w1.shape[-1]
    dk = w2.shape[-1]
    return pl.pallas_call(
        _compress_kernel,
        grid=(2, BG),
        in_specs=[
            pl.BlockSpec((1, 1, n_chunks, width), lambda a, b: (a, b, 0, 0)),
            pl.BlockSpec((1, 2, width), lambda a, b: (a, 0, 0)),
            pl.BlockSpec((1, 2 * width, hidden), lambda a, b: (a, 0, 0)),
            pl.BlockSpec((1, 1, hidden), lambda a, b: (a, 0, 0)),
            pl.BlockSpec((1, hidden, dk), lambda a, b: (a, 0, 0)),
        ],
        out_specs=pl.BlockSpec((1, 1, n_chunks, dk), lambda a, b: (a, b, 0, 0)),
        out_shape=jax.ShapeDtypeStruct((2, BG, n_chunks, dk), BF16),
        compiler_params=_params(2),
        name="compress",
    )(kvc, pe, w1, b1, w2)


def _select_blocks(score):
    n = score.shape[-1]
    col = lax.broadcasted_iota(I32, score.shape, 1).astype(F32)
    sel = jnp.zeros(score.shape, F32)
    s = score
    for _ in range(SLC_TOPN):
        m = jnp.max(s, axis=-1, keepdims=True)
        first = jnp.min(jnp.where(s == m, col, float(n)), axis=-1, keepdims=True)
        pick = col == first
        sel = jnp.where(pick & (m > 0.5 * NEG_INF), 1.0, sel)
        s = jnp.where(pick, -3.0e38, s)
    return sel


def _nsa_kernel(q_ref, ks_ref, vs_ref, kw_ref, vw_ref, kc_ref, vc_ref, gate_ref, ov_ref, o_ref):
    qb = pl.program_id(2)
    q0 = qb * Q_BLOCK
    R, Q, dk = NSA_REP, Q_BLOCK, NSA_HEAD_DIM
    q2 = q_ref[0, 0].reshape(R * Q, dk)

    kc = kc_ref[0, 0]
    n_cmp = kc.shape[0]
    s_c = _dot_nt(q2, kc)
    t_c = q0 + lax.broadcasted_iota(I32, (Q, n_cmp), 0)
    cmp_end = lax.broadcasted_iota(I32, (Q, n_cmp), 1) * CMP_STRIDE + (CMP_BLOCK - 1)
    mask_c = cmp_end <= t_c
    o_c = []
    p_sum = jnp.zeros((Q, n_cmp), F32)
    for r in range(R):
        p = _masked_softmax(s_c[r * Q:(r + 1) * Q], mask_c)
        p_sum = p_sum + p
        o_c.append(_dot(p.astype(BF16), vc_ref[0, 0]))

    n_slc = ov_ref.shape[1]
    imp = jnp.dot(p_sum, ov_ref[...], preferred_element_type=F32, precision=lax.Precision.HIGHEST)
    t_b = q0 + lax.broadcasted_iota(I32, (Q, n_slc), 0)
    blk = lax.broadcasted_iota(I32, (Q, n_slc), 1)
    allowed = blk * SLC_BLOCK <= t_b
    cur = t_b // SLC_BLOCK
    forced = (blk == 0) | (blk == cur) | (blk == cur - 1)
    score = jnp.where(forced & allowed, FORCE_BONUS, jnp.where(allowed, imp, NEG_INF))
    sel = _select_blocks(score).astype(BF16)

    tk = SEL_KEYS
    blocks_per_tile = tk // SLC_BLOCK
    n_tiles = (q0 + Q + tk - 1) // tk
    key_i = lax.broadcasted_iota(I32, (Q, tk), 1)
    t_s = q0 + lax.broadcasted_iota(I32, (Q, tk), 0)
    e_row = lax.broadcasted_iota(I32, (n_slc, tk), 0)
    e_blk = lax.broadcasted_iota(I32, (n_slc, tk), 1) // SLC_BLOCK

    def sel_tile(kt, carry):
        ms, ls, accs = carry
        base = pl.multiple_of(kt * tk, tk)
        k_t = ks_ref[0, 0, pl.ds(base, tk), :]
        v_t = vs_ref[0, 0, pl.ds(base, tk), :]
        s = _dot_nt(q2, k_t)
        expand = (e_row == e_blk + kt * blocks_per_tile).astype(BF16)
        chosen = _dot(sel, expand) > 0.5
        mask = chosen & (key_i + base <= t_s)
        new_m, new_l, new_acc = [], [], []
        for r in range(R):
            s_r = jnp.where(mask, s[r * Q:(r + 1) * Q], NEG_INF)
            m_new = jnp.maximum(ms[r], jnp.max(s_r, axis=-1, keepdims=True))
            alpha = jnp.exp(ms[r] - m_new)
            p = jnp.where(mask, jnp.exp(s_r - m_new), 0.0)
            new_m.append(m_new)
            new_l.append(alpha * ls[r] + jnp.sum(p, axis=-1, keepdims=True))
            new_acc.append(alpha * accs[r] + _dot(p.astype(BF16), v_t))
        return tuple(new_m), tuple(new_l), tuple(new_acc)

    init = (tuple(jnp.full((Q, 1), NEG_INF, F32) for _ in range(R)),
            tuple(jnp.zeros((Q, 1), F32) for _ in range(R)),
            tuple(jnp.zeros((Q, dk), F32) for _ in range(R)))
    _, ls, accs = lax.fori_loop(0, n_tiles, sel_tile, init)
    o_s = [accs[r] / jnp.maximum(ls[r], TINY) for r in range(R)]

    span = WINDOW + Q
    start = pl.multiple_of(jnp.maximum(q0 - WINDOW, 0), Q)
    kw = kw_ref[0, 0, pl.ds(start, span), :]
    vw = vw_ref[0, 0, pl.ds(start, span), :]
    s_w = _dot_nt(q2, kw)
    pos = start + lax.broadcasted_iota(I32, (Q, span), 1)
    diff = q0 + lax.broadcasted_iota(I32, (Q, span), 0) - pos
    mask_w = (diff >= 0) & (diff < WINDOW)
    o_w = []
    for r in range(R):
        p = _masked_softmax(s_w[r * Q:(r + 1) * Q], mask_w)
        o_w.append(_dot(p.astype(BF16), vw))

    gate = gate_ref[0, 0]
    for r in range(R):
        out = (gate[:, 3 * r:3 * r + 1] * o_c[r] + gate[:, 3 * r + 1:3 * r + 2] * o_s[r]
               + gate[:, 3 * r + 2:3 * r + 3] * o_w[r])
        o_ref[0, :, r * dk:(r + 1) * dk] = out.astype(BF16)


def _nsa(q, ks, vs, kw, vw, kcmp, vcmp, gates, overlap):
    B, G, R, S, dk = q.shape
    n_cmp = kcmp.shape[2]
    n_slc = overlap.shape[1]
    kv_spec = pl.BlockSpec((1, 1, S, dk), lambda b, g, i: (b, g, 0, 0))
    cmp_spec = pl.BlockSpec((1, 1, n_cmp, dk), lambda b, g, i: (b, g, 0, 0))
    return pl.pallas_call(
        _nsa_kernel,
        grid=(B, G, S // Q_BLOCK),
        in_specs=[
            pl.BlockSpec((1, 1, R, Q_BLOCK, dk), lambda b, g, i: (b, g, 0, i, 0)),
            kv_spec, kv_spec, kv_spec, kv_spec,
            cmp_spec, cmp_spec,
            pl.BlockSpec((1, 1, Q_BLOCK, LANES), lambda b, g, i: (b, g, i, 0)),
            pl.BlockSpec((n_cmp, n_slc), lambda b, g, i: (0, 0)),
        ],
        out_specs=pl.BlockSpec((1, Q_BLOCK, R * dk), lambda b, g, i: (b, i, g)),
        out_shape=jax.ShapeDtypeStruct((B, S, G * R * dk), BF16),
        compiler_params=_params(3),
        name="nsa",
    )(q, ks, vs, kw, vw, kcmp, vcmp, gates, overlap)


def _mix_kernel(x_ref, onsa_ref, u_ref, v_ref, mg_ref, lng_ref, lnb_ref, ws_ref, bs_ref,
                wn_ref, wg_ref, wm_ref, h_ref):
    D = x_ref.shape[1]
    u = _gelu(u_ref[...])
    v = _gelu(v_ref[...])
    mu = jnp.mean(v, axis=-1, keepdims=True)
    var = jnp.mean(jnp.square(v - mu), axis=-1, keepdims=True)
    vn = ((v - mu) * lax.rsqrt(var + NORM_EPS) * lng_ref[...] + lnb_ref[...]).astype(BF16)
    C = GMLP_CHUNK
    gd = vn.shape[1] // GMLP_GROUPS
    rows = []
    for c in range(vn.shape[0] // C):
        cols = []
        for g in range(GMLP_GROUPS):
            cols.append(_dot(ws_ref[g], vn[c * C:(c + 1) * C, g * gd:(g + 1) * gd]) + bs_ref[g])
        rows.append(jnp.concatenate(cols, axis=1))
    o_gmlp = u * jnp.concatenate(rows, axis=0)
    mg = mg_ref[...]
    y = (mg[:, :D] * _dot(onsa_ref[...], wn_ref[...])
         + mg[:, D:] * _dot(o_gmlp.astype(BF16), wg_ref[...]))
    h_ref[...] = x_ref[...] + _dot(y.astype(BF16), wm_ref[...])


def _mix(x2, o_nsa, u_raw, v_raw, mg, ln_g, ln_b, ws_causal, bs_b, w_nsa_out, w_gmlp_out, w_mix_out):
    T, D = x2.shape
    tm = MIX_ROWS
    W = u_raw.shape[1]

    def tok(i):
        return (i, 0)

    def const2(i):
        return (0, 0)

    def const3(i):
        return (0, 0, 0)

    return pl.pallas_call(
        _mix_kernel,
        grid=(T // tm,),
        in_specs=[
            pl.BlockSpec((tm, D), tok),
            pl.BlockSpec((tm, o_nsa.shape[1]), tok),
            pl.BlockSpec((tm, W), tok),
            pl.BlockSpec((tm, W), tok),
            pl.BlockSpec((tm, 2 * D), tok),
            pl.BlockSpec((1, W), const2),
            pl.BlockSpec((1, W), const2),
            pl.BlockSpec(ws_causal.shape, const3),
            pl.BlockSpec(bs_b.shape, const3),
            pl.BlockSpec(w_nsa_out.shape, const2),
            pl.BlockSpec(w_gmlp_out.shape, const2),
            pl.BlockSpec(w_mix_out.shape, const2),
        ],
        out_specs=pl.BlockSpec((tm, D), tok),
        out_shape=jax.ShapeDtypeStruct((T, D), F32),
        compiler_params=_params(1),
        name="mix",
    )(x2, o_nsa, u_raw, v_raw, mg, ln_g.reshape(1, W), ln_b.reshape(1, W), ws_causal, bs_b,
      w_nsa_out, w_gmlp_out, w_mix_out)


def _mem_kv_kernel(mem_ref, g_ref, w_ref, k_ref, v_ref):
    D = mem_ref.shape[2]
    mn = _rms(mem_ref[0], g_ref[...]).astype(BF16)
    kv = _dot(mn, w_ref[...])
    k_ref[0] = kv[:, :D].astype(BF16)
    v_ref[0] = kv[:, D:].astype(BF16)


def _mem_kv(mem, g_mem, w_xkv):
    B, M, D = mem.shape
    spec = pl.BlockSpec((1, M, D), lambda b: (b, 0, 0))
    return pl.pallas_call(
        _mem_kv_kernel,
        grid=(B,),
        in_specs=[spec, pl.BlockSpec((1, D), lambda b: (0, 0)),
                  pl.BlockSpec((D, 2 * D), lambda b: (0, 0))],
        out_specs=(spec, spec),
        out_shape=(jax.ShapeDtypeStruct((B, M, D), BF16), jax.ShapeDtypeStruct((B, M, D), BF16)),
        compiler_params=_params(1),
        name="mem_kv",
    )(mem, g_mem.reshape(1, D), w_xkv)


def _xattn_kernel(h_ref, g_ref, wq_ref, k_ref, v_ref, wo_ref, o_ref):
    h = h_ref[...]
    D = h.shape[1]
    hd = D // XATTN_HEADS
    hn = _rms(h, g_ref[...]).astype(BF16)
    q = _dot(hn, wq_ref[...]) * (hd ** -0.5)
    outs = []
    for a in range(XATTN_HEADS):
        s = _dot_nt(q[:, a * hd:(a + 1) * hd].astype(BF16), k_ref[0, :, a * hd:(a + 1) * hd])
        m = jnp.max(s, axis=-1, keepdims=True)
        e = jnp.exp(s - m)
        p = e / jnp.sum(e, axis=-1, keepdims=True)
        outs.append(_dot(p.astype(BF16), v_ref[0, :, a * hd:(a + 1) * hd]))
    o = jnp.concatenate(outs, axis=1).astype(BF16)
    o_ref[...] = h + _dot(o, wo_ref[...])


def _xattn(h, g_xattn, w_xq, k_mem, v_mem, w_xo, S):
    T, D = h.shape
    tm = MIX_ROWS
    per_b = S // tm
    M = k_mem.shape[1]
    mem_spec = pl.BlockSpec((1, M, D), lambda i: (i // per_b, 0, 0))
    return pl.pallas_call(
        _xattn_kernel,
        grid=(T // tm,),
        in_specs=[
            pl.BlockSpec((tm, D), lambda i: (i, 0)),
            pl.BlockSpec((1, D), lambda i: (0, 0)),
            pl.BlockSpec((D, D), lambda i: (0, 0)),
            mem_spec, mem_spec,
            pl.BlockSpec((D, D), lambda i: (0, 0)),
        ],
        out_specs=pl.BlockSpec((tm, D), lambda i: (i, 0)),
        out_shape=jax.ShapeDtypeStruct((T, D), F32),
        compiler_params=_params(1),
        name="xattn",
    )(h, g_xattn.reshape(1, D), w_xq, k_mem, v_mem, w_xo)


def _top_rows(vals, k):
    n = vals.shape[0]
    row = lax.broadcasted_iota(I32, vals.shape, 0).astype(F32)
    top_v, top_i = [], []
    for _ in range(k):
        m = jnp.max(vals, axis=0, keepdims=True)
        first = jnp.min(jnp.where(vals == m, row, float(n)), axis=0, keepdims=True)
        top_v.append(m)
        top_i.append(first)
        vals = jnp.where(row == first, -3.0e38, vals)
    return top_v, top_i


def _route_kernel(h_ref, g_ref, wq_ref, sk_ref, hn_ref, i_ref, j_ref, gate_ref,
                  qt_scr, i_scr, j_scr, gate_scr):
    K = PEER_TOPK
    hn = _rms(h_ref[...], g_ref[...]).astype(BF16)
    hn_ref[...] = hn
    qt_scr[...] = _dot_nt(wq_ref[...], hn)
    half_dim = sk_ref.shape[2]
    n_tok = hn.shape[0]
    cand_a = lax.broadcasted_iota(I32, (K, n_tok), 0).astype(F32)

    def head(hd, _):
        tops = []
        for p in range(2):
            off = pl.multiple_of((hd * 2 + p) * half_dim, half_dim)
            qs = qt_scr[pl.ds(off, half_dim), :].astype(BF16)
            sc = _dot(sk_ref[hd * 2 + p], qs)
            tops.append(_top_rows(sc, K))
        (s0, i0), (s1, i1) = tops
        s1_all = jnp.concatenate(s1, axis=0)
        i0_all = jnp.concatenate(i0, axis=0)
        i1_all = jnp.concatenate(i1, axis=0)
        cand = jnp.concatenate([s0[a] + s1_all for a in range(K)], axis=0)
        best_s, pos = _top_rows(cand, K)
        best = jnp.concatenate(best_s, axis=0)
        m = jnp.max(best, axis=0, keepdims=True)
        e = jnp.exp(best - m)
        gate = e / jnp.sum(e, axis=0, keepdims=True)
        ei, ej = [], []
        for kk in range(K):
            a = jnp.floor(pos[kk] * (1.0 / K))
            b = pos[kk] - a * K
            ei.append(jnp.sum(jnp.where(cand_a == a, i0_all, 0.0), axis=0, keepdims=True))
            ej.append(jnp.sum(jnp.where(cand_a == b, i1_all, 0.0), axis=0, keepdims=True))
        row0 = pl.multiple_of(hd * K, K)
        i_scr[pl.ds(row0, K), :] = jnp.concatenate(ei, axis=0)
        j_scr[pl.ds(row0, K), :] = jnp.concatenate(ej, axis=0)
        gate_scr[pl.ds(row0, K), :] = gate
        return 0

    lax.fori_loop(0, PEER_HEADS, head, 0)
    i_ref[...] = i_scr[...].T.astype(I32)
    j_ref[...] = j_scr[...].T.astype(I32)
    gate_ref[...] = gate_scr[...].T


def _route(h, g_peer, wq_t, sub_keys):
    T, D = h.shape
    tt = ROUTE_TOKENS
    n_sel = PEER_HEADS * PEER_TOPK
    half_dim = sub_keys.shape[2]
    tok = lambda i: (i, 0)
    return pl.pallas_call(
        _route_kernel,
        grid=(T // tt,),
        in_specs=[
            pl.BlockSpec((tt, D), tok),
            pl.BlockSpec((1, D), lambda i: (0, 0)),
            pl.BlockSpec(wq_t.shape, lambda i: (0, 0)),
            pl.BlockSpec(sub_keys.shape, lambda i: (0, 0, 0)),
        ],
        out_specs=(
            pl.BlockSpec((tt, D), tok),
            pl.BlockSpec((tt, n_sel), tok),
            pl.BlockSpec((tt, n_sel), tok),
            pl.BlockSpec((tt, n_sel), tok),
        ),
        out_shape=(
            jax.ShapeDtypeStruct((T, D), BF16),
            jax.ShapeDtypeStruct((T, n_sel), I32),
            jax.ShapeDtypeStruct((T, n_sel), I32),
            jax.ShapeDtypeStruct((T, n_sel), F32),
        ),
        scratch_shapes=[
            pltpu.VMEM((wq_t.shape[0], tt), F32),
            pltpu.VMEM((n_sel, tt), F32),
            pltpu.VMEM((n_sel, tt), F32),
            pltpu.VMEM((n_sel, tt), F32),
        ],
        compiler_params=_params(1),
        name="route",
    )(h, g_peer.reshape(1, D), wq_t, sub_keys)


def _peer_up_kernel(hn_ref, u_ref, i_ref, j_ref, a_ref):
    e = pl.program_id(1)

    @pl.when(e == 0)
    def _():
        a_ref[...] = jnp.zeros(a_ref.shape, F32)

    z = _dot_nt(hn_ref[...], u_ref[...])
    ii = i_ref[...]
    jj = j_ref[...]
    acc = a_ref[...]
    for r in range(EXPERT_ROWS):
        picked = jnp.take_along_axis(z[:, r * PEER_KEYS:(r + 1) * PEER_KEYS], jj, axis=1)
        acc = jnp.where(ii == e * EXPERT_ROWS + r, picked, acc)
    a_ref[...] = acc


def _peer_up(hn, u_bf, i_idx, j_idx):
    T, D = hn.shape
    tt = UP_TOKENS
    n_sel = i_idx.shape[1]
    rows = EXPERT_ROWS * PEER_KEYS
    tok = lambda t, e: (t, 0)
    return pl.pallas_call(
        _peer_up_kernel,
        grid=(T // tt, u_bf.shape[0] // rows),
        in_specs=[
            pl.BlockSpec((tt, D), tok),
            pl.BlockSpec((rows, D), lambda t, e: (e, 0)),
            pl.BlockSpec((tt, n_sel), tok),
            pl.BlockSpec((tt, n_sel), tok),
        ],
        out_specs=pl.BlockSpec((tt, n_sel), tok),
        out_shape=jax.ShapeDtypeStruct((T, n_sel), F32),
        compiler_params=_params(2),
        name="peer_up",
    )(hn, u_bf, i_idx, j_idx)


def _peer_down_kernel(a_ref, gate_ref, i_ref, j_ref, v_ref, h_ref, g_ref, o_ref,
                      w_scr, coef_scr, acc_scr):
    e = pl.program_id(1)
    n_tok = a_ref.shape[0]
    nk = PEER_KEYS

    @pl.when(e == 0)
    def _():
        coef_scr[...] = gate_ref[...] * _gelu(a_ref[...])
        acc_scr[...] = jnp.zeros(acc_scr.shape, F32)
        sub = lax.broadcasted_iota(I32, (nk, i_ref.shape[1]), 0)

        def expand(t, _):
            ii = i_ref[pl.ds(t, 1), :]
            jj = j_ref[pl.ds(t, 1), :]
            cc = coef_scr[pl.ds(t, 1), :]
            a = jnp.where(sub == ii, cc, 0.0).astype(BF16)
            bt = jnp.where(sub == jj, 1.0, 0.0).astype(BF16)
            base = pl.multiple_of(t * W_ROW_STRIDE, SUBLANES)
            w_scr[pl.ds(base, nk), :] = _dot_nt(a, bt)
            return 0

        lax.fori_loop(0, n_tok, expand, 0)

    parts = []
    for r in range(EXPERT_ROWS):
        row = e * EXPERT_ROWS + r
        parts.append(w_scr[pl.ds(row, n_tok, stride=W_ROW_STRIDE), :].astype(BF16))
    lhs = jnp.concatenate(parts, axis=1)
    acc_scr[...] += _dot(lhs, v_ref[...])

    @pl.when(e == pl.num_programs(1) - 1)
    def _():
        o_ref[...] = _rms(h_ref[...] + acc_scr[...], g_ref[...])


def _peer_down(a_pre, gate, i_idx, j_idx, v_bf, h, g_final):
    T, D = h.shape
    tt = DOWN_TOKENS
    n_sel = i_idx.shape[1]
    rows = EXPERT_ROWS * PEER_KEYS
    tok = lambda t, e: (t, 0)
    sel_spec = pl.BlockSpec((tt, n_sel), tok)
    return pl.pallas_call(
        _peer_down_kernel,
        grid=(T // tt, v_bf.shape[0] // rows),
        in_specs=[
            sel_spec, sel_spec, sel_spec, sel_spec,
            pl.BlockSpec((rows, D), lambda t, e: (e, 0)),
            pl.BlockSpec((tt, D), tok),
            pl.BlockSpec((1, D), lambda t, e: (0, 0)),
        ],
        out_specs=pl.BlockSpec((tt, D), tok),
        out_shape=jax.ShapeDtypeStruct((T, D), F32),
        scratch_shapes=[
            pltpu.VMEM((tt * W_ROW_STRIDE, PEER_KEYS), F32),
            pltpu.VMEM((tt, n_sel), F32),
            pltpu.VMEM((tt, D), F32),
        ],
        compiler_params=_params(2),
        name="peer_down",
    )(a_pre, gate, i_idx, j_idx, v_bf, h, g_final.reshape(1, D))


def _overlap_table(n_cmp, n_slc):
    c0 = jnp.arange(n_cmp) * CMP_STRIDE
    j0 = jnp.arange(n_slc) * SLC_BLOCK
    ov = jnp.clip(jnp.minimum(c0[:, None] + CMP_BLOCK, j0[None, :] + SLC_BLOCK)
                  - jnp.maximum(c0[:, None], j0[None, :]), 0, None)
    return ov.astype(F32) / CMP_BLOCK


def kernel(x, mem, g_mix, w_in, cmp_pe_k, cmp_w1_k, cmp_b1_k, cmp_w2_k, cmp_pe_v, cmp_w1_v, cmp_b1_v, cmp_w2_v, gmlp_ln_g, gmlp_ln_b, gmlp_ws, gmlp_bs, w_nsa_out, w_gmlp_out, w_mix_out, g_xattn, g_mem, w_xq, w_xkv, w_xo, g_peer, w_peer_q, peer_sub_keys, peer_u, peer_v, g_final):
    B, S, D = x.shape
    T = B * S
    G, R, dk = NSA_GROUPS, NSA_REP, NSA_HEAD_DIM
    x2 = x.reshape(T, D)

    n_gate = NSA_HEADS * 3
    gate_cols = w_in[:, 1280:1280 + n_gate]
    per_g = n_gate // G
    gate_blocks = [jnp.pad(gate_cols[:, g * per_g:(g + 1) * per_g], ((0, 0), (0, LANES - per_g)))
                   for g in range(G)]
    w_re = jnp.concatenate([w_in[:, :1280]] + gate_blocks + [w_in[:, 1280 + n_gate:]],
                           axis=1).astype(BF16)
    half = CMP_STRIDE * dk
    pe = jnp.stack([cmp_pe_k.reshape(2, half), cmp_pe_v.reshape(2, half)])
    w1 = jnp.stack([cmp_w1_k, cmp_w1_v]).astype(BF16)
    b1 = jnp.stack([cmp_b1_k, cmp_b1_v])[:, None, :]
    w2 = jnp.stack([cmp_w2_k, cmp_w2_v]).astype(BF16)
    C = GMLP_CHUNK
    ws_causal = (gmlp_ws * jnp.tril(jnp.ones((C, C), F32))).astype(BF16)
    group_dim = gmlp_ln_g.shape[0] // GMLP_GROUPS
    bs_b = jnp.broadcast_to(gmlp_bs[:, :, None], (GMLP_GROUPS, C, group_dim))
    n_chunks = S // CMP_STRIDE
    overlap = _overlap_table(n_chunks, S // SLC_BLOCK)

    q, kvc, ks, vs, kw, vw, gates, u_raw, v_raw, mg = _in_proj(x2, g_mix, w_re, B, S)
    cmp_kv = _compress(kvc.reshape(2, B * G, n_chunks, half), pe, w1, b1, w2)
    cmp_kv = cmp_kv.reshape(2, B, G, n_chunks, dk)
    o_nsa = _nsa(q, ks, vs, kw, vw, cmp_kv[0], cmp_kv[1], gates, overlap)
    h = _mix(x2, o_nsa.reshape(T, G * R * dk), u_raw, v_raw, mg, gmlp_ln_g, gmlp_ln_b,
             ws_causal, bs_b, w_nsa_out.astype(BF16), w_gmlp_out.astype(BF16),
             w_mix_out.astype(BF16))

    k_mem, v_mem = _mem_kv(mem, g_mem, w_xkv.astype(BF16))
    h = _xattn(h, g_xattn, w_xq.astype(BF16), k_mem, v_mem, w_xo.astype(BF16), S)

    half_dim = peer_sub_keys.shape[3]
    sk = peer_sub_keys.reshape(PEER_HEADS * 2, PEER_KEYS, half_dim).astype(BF16)
    hn, i_idx, j_idx, gate = _route(h, g_peer, w_peer_q.T.astype(BF16), sk)
    a_pre = _peer_up(hn, peer_u.astype(BF16), i_idx, j_idx)
    out = _peer_down(a_pre, gate, i_idx, j_idx, peer_v.astype(BF16), h, g_final)
    return out.reshape(B, S, D)
```

```python
import functools

import jax
import jax.numpy as jnp
from jax import lax
from jax.experimental import pallas as pl
from jax.experimental.pallas import tpu as pltpu

F32 = jnp.float32
BF16 = jnp.bfloat16
I32 = jnp.int32

NORM_EPS = 1e-6
NEG_INF = -1e30
TINY = 1e-30

NSA_HEADS = 8
NSA_HEAD_DIM = 64
NSA_GROUPS = 2
NSA_REP = NSA_HEADS // NSA_GROUPS
CMP_BLOCK = 32
CMP_STRIDE = 16
SLC_BLOCK = 64
SLC_TOPN = 16
WINDOW = 512
Q_BLOCK = 128
FORCE_BONUS = 1e4
GMLP_GROUPS = 4
GMLP_CHUNK = 128
XATTN_HEADS = 4
PEER_HEADS = 8
PEER_KEYS = 128
PEER_TOPK = 16

LANES = 128
SUBLANES = 8
VMEM_LIMIT_BYTES = 56 * 1024 * 1024

PROJ_ROWS = 512
MIX_ROWS = 256
SEL_KEYS = 1024
ROUTE_TOKENS = 256
UP_TOKENS = 512
DOWN_TOKENS = 512
EXPERT_ROWS = 8
DOWN_EXPERT_ROWS = 8
W_ROW_STRIDE = PEER_KEYS + SUBLANES
EXPAND_UNROLL = 16


def _params(n_axes):
    return pltpu.CompilerParams(
        dimension_semantics=("arbitrary",) * n_axes,
        vmem_limit_bytes=VMEM_LIMIT_BYTES,
    )


def _rms(x, g):
    return x * lax.rsqrt(jnp.mean(x * x, axis=-1, keepdims=True) + NORM_EPS) * g


def _gelu(x):
    return 0.5 * x * (1.0 + lax.erf(x * 0.7071067811865476))


def _sigmoid(x):
    return 1.0 / (1.0 + jnp.exp(-x))


def _dot(a, b):
    return jnp.dot(a, b, preferred_element_type=F32)


def _dot_nt(a, b):
    return lax.dot_general(a, b, (((1,), (1,)), ((), ())), preferred_element_type=F32)


def _masked_softmax(s, mask):
    s = jnp.where(mask, s, NEG_INF)
    m = jnp.max(s, axis=-1, keepdims=True)
    p = jnp.where(mask, jnp.exp(s - m), 0.0)
    return p / jnp.maximum(jnp.sum(p, axis=-1, keepdims=True), TINY)


_C_Q = 0
_C_KV = 512
_C_GATE = 1280
_C_U = 1536
_C_V = 2048
_C_MERGE = 2560
_C_END = 4608


def _in_proj_kernel(x_ref, g_ref, w_ref, q_ref, kvc_ref, ks_ref, vs_ref, kw_ref, vw_ref,
                    gate_ref, u_ref, v_ref, mg_ref):
    hn = _rms(x_ref[...], g_ref[...]).astype(BF16)

    def proj(a, b):
        return _dot(hn, w_ref[:, a:b])

    dk = NSA_HEAD_DIM
    pq = proj(_C_Q, _C_KV) * (dk ** -0.5)
    for g in range(NSA_GROUPS):
        for r in range(NSA_REP):
            c = (g * NSA_REP + r) * dk
            q_ref[0, g, r] = pq[:, c:c + dk].astype(BF16)
    pk = proj(_C_KV, _C_GATE)
    for g in range(NSA_GROUPS):
        kvc_ref[0, 0, g] = pk[:, 0 * 128 + g * dk:0 * 128 + (g + 1) * dk]
        kvc_ref[1, 0, g] = pk[:, 1 * 128 + g * dk:1 * 128 + (g + 1) * dk]
        ks_ref[0, g] = pk[:, 2 * 128 + g * dk:2 * 128 + (g + 1) * dk].astype(BF16)
        vs_ref[0, g] = pk[:, 3 * 128 + g * dk:3 * 128 + (g + 1) * dk].astype(BF16)
        kw_ref[0, g] = pk[:, 4 * 128 + g * dk:4 * 128 + (g + 1) * dk].astype(BF16)
        vw_ref[0, g] = pk[:, 5 * 128 + g * dk:5 * 128 + (g + 1) * dk].astype(BF16)
        gate_ref[0, g] = _sigmoid(proj(_C_GATE + g * 128, _C_GATE + (g + 1) * 128))
    u_ref[...] = proj(_C_U, _C_V)
    v_ref[...] = proj(_C_V, _C_MERGE)
    mg_ref[...] = _sigmoid(proj(_C_MERGE, _C_END))


def _in_proj(x2, g_mix, w_re, B, S):
    T, D = x2.shape
    tm = PROJ_ROWS
    per_b = S // tm
    G, R, dk = NSA_GROUPS, NSA_REP, NSA_HEAD_DIM

    def tok(i):
        return (i, 0)

    def bgs(i):
        return (i // per_b, 0, i % per_b, 0)

    out_shape = (
        jax.ShapeDtypeStruct((B, G, R, S, dk), BF16),
        jax.ShapeDtypeStruct((2, B, G, S, dk), F32),
        jax.ShapeDtypeStruct((B, G, S, dk), BF16),
        jax.ShapeDtypeStruct((B, G, S, dk), BF16),
        jax.ShapeDtypeStruct((B, G, S, dk), BF16),
        jax.ShapeDtypeStruct((B, G, S, dk), BF16),
        jax.ShapeDtypeStruct((B, G, S, LANES), F32),
        jax.ShapeDtypeStruct((T, 512), F32),
        jax.ShapeDtypeStruct((T, 512), F32),
        jax.ShapeDtypeStruct((T, 2 * D), F32),
    )
    kv_spec = pl.BlockSpec((1, G, tm, dk), bgs)
    out_specs = (
        pl.BlockSpec((1, G, R, tm, dk), lambda i: (i // per_b, 0, 0, i % per_b, 0)),
        pl.BlockSpec((2, 1, G, tm, dk), lambda i: (0, i // per_b, 0, i % per_b, 0)),
        kv_spec, kv_spec, kv_spec, kv_spec,
        pl.BlockSpec((1, G, tm, LANES), bgs),
        pl.BlockSpec((tm, 512), tok),
        pl.BlockSpec((tm, 512), tok),
        pl.BlockSpec((tm, 2 * D), tok),
    )
    return pl.pallas_call(
        _in_proj_kernel,
        grid=(T // tm,),
        in_specs=[
            pl.BlockSpec((tm, D), tok),
            pl.BlockSpec((1, D), lambda i: (0, 0)),
            pl.BlockSpec((D, _C_END), lambda i: (0, 0)),
        ],
        out_specs=out_specs,
        out_shape=out_shape,
        compiler_params=_params(1),
        name="in_proj",
    )(x2, g_mix.reshape(1, D), w_re)


def _compress_kernel(x_ref, pe_ref, w1_ref, b1_ref, w2_ref, o_ref):
    x = x_ref[0, 0]
    half = x.shape[1]
    lo = (x + pe_ref[0, 0:1, :]).astype(BF16)
    hi = (x + pe_ref[0, 1:2, :]).astype(BF16)
    p = _dot(lo, w1_ref[0, :half, :])
    q = _dot(hi, w1_ref[0, half:, :])
    n = x.shape[0]
    h = p + pltpu.roll(q, n - 1, 0) + b1_ref[0]
    o_ref[0, 0] = _dot(_gelu(h).astype(BF16), w2_ref[0]).astype(BF16)


def _compress(kvc, pe, w1, b1, w2):
    _, BG, n_chunks, width = kvc.shape
    hidden = w1.shape[-1]
    dk = w2.shape[-1]
    return pl.pallas_call(
        _compress_kernel,
        grid=(2, BG),
        in_specs=[
            pl.BlockSpec((1, 1, n_chunks, width), lambda a, b: (a, b, 0, 0)),
            pl.BlockSpec((1, 2, width), lambda a, b: (a, 0, 0)),
            pl.BlockSpec((1, 2 * width, hidden), lambda a, b: (a, 0, 0)),
            pl.BlockSpec((1, 1, hidden), lambda a, b: (a, 0, 0)),
            pl.BlockSpec((1, hidden, dk), lambda a, b: (a, 0, 0)),
        ],
        out_specs=pl.BlockSpec((1, 1, n_chunks, dk), lambda a, b: (a, b, 0, 0)),
        out_shape=jax.ShapeDtypeStruct((2, BG, n_chunks, dk), BF16),
        compiler_params=_params(2),
        name="compress",
    )(kvc, pe, w1, b1, w2)


def _select_blocks(score_t):
    n = score_t.shape[0]
    row = lax.broadcasted_iota(I32, score_t.shape, 0).astype(F32)
    sel = jnp.zeros(score_t.shape, F32)
    s = score_t
    for _ in range(SLC_TOPN):
        m = jnp.max(s, axis=0, keepdims=True)
        first = jnp.min(jnp.where(s == m, row, float(n)), axis=0, keepdims=True)
        pick = row == first
        sel = jnp.where(pick & (m > 0.5 * NEG_INF), 1.0, sel)
        s = jnp.where(pick, -3.0e38, s)
    return sel


def _nsa_kernel(q_ref, ks_ref, vs_ref, kw_ref, vw_ref, kc_ref, vc_ref, gate_ref, ovt_ref, o_ref):
    qb = pl.program_id(2)
    q0 = qb * Q_BLOCK
    R, Q, dk = NSA_REP, Q_BLOCK, NSA_HEAD_DIM
    q2 = q_ref[0, 0].reshape(R * Q, dk)

    kc = kc_ref[0, 0]
    n_cmp = kc.shape[0]
    s_c = _dot_nt(q2, kc)
    t_c = q0 + lax.broadcasted_iota(I32, (Q, n_cmp), 0)
    cmp_end = lax.broadcasted_iota(I32, (Q, n_cmp), 1) * CMP_STRIDE + (CMP_BLOCK - 1)
    mask_c = cmp_end <= t_c
    o_c = []
    p_sum = jnp.zeros((Q, n_cmp), F32)
    for r in range(R):
        p = _masked_softmax(s_c[r * Q:(r + 1) * Q], mask_c)
        p_sum = p_sum + p
        o_c.append(_dot(p.astype(BF16), vc_ref[0, 0]))

    span = WINDOW + Q
    start = pl.multiple_of(jnp.maximum(q0 - WINDOW, 0), Q)
    kw = kw_ref[0, 0, pl.ds(start, span), :]
    vw = vw_ref[0, 0, pl.ds(start, span), :]
    s_w = _dot_nt(q2, kw)
    pos = start + lax.broadcasted_iota(I32, (Q, span), 1)
    diff = q0 + lax.broadcasted_iota(I32, (Q, span), 0) - pos
    mask_w = (diff >= 0) & (diff < WINDOW)
    o_w = []
    for r in range(R):
        s_r = jnp.where(mask_w, s_w[r * Q:(r + 1) * Q], NEG_INF)
        e = jnp.exp(s_r - jnp.max(s_r, axis=-1, keepdims=True))
        p = e / jnp.maximum(jnp.sum(e, axis=-1, keepdims=True), TINY)
        o_w.append(_dot(p.astype(BF16), vw))

    n_slc = ovt_ref.shape[0]
    imp_t = lax.dot_general(ovt_ref[...], p_sum, (((1,), (1,)), ((), ())),
                            preferred_element_type=F32, precision=lax.Precision.HIGHEST)
    t_b = q0 + lax.broadcasted_iota(I32, (n_slc, Q), 1)
    blk = lax.broadcasted_iota(I32, (n_slc, Q), 0)
    allowed = blk * SLC_BLOCK <= t_b
    cur = t_b // SLC_BLOCK
    forced = (blk == 0) | (blk == cur) | (blk == cur - 1)
    score_t = jnp.where(forced & allowed, FORCE_BONUS, jnp.where(allowed, imp_t, NEG_INF))
    sel = _select_blocks(score_t).T.astype(BF16)

    tk = SEL_KEYS
    blocks_per_tile = tk // SLC_BLOCK
    n_tiles = (q0 + Q + tk - 1) // tk
    key_i = lax.broadcasted_iota(I32, (Q, tk), 1)
    t_s = q0 + lax.broadcasted_iota(I32, (Q, tk), 0)
    e_row = lax.broadcasted_iota(I32, (n_slc, tk), 0)
    e_blk = lax.broadcasted_iota(I32, (n_slc, tk), 1) // SLC_BLOCK

    def sel_tile(kt, carry):
        ms, ls, accs = carry
        base = pl.multiple_of(kt * tk, tk)
        k_t = ks_ref[0, 0, pl.ds(base, tk), :]
        v_t = vs_ref[0, 0, pl.ds(base, tk), :]
        s = _dot_nt(q2, k_t)
        expand = (e_row == e_blk + kt * blocks_per_tile).astype(BF16)
        chosen = _dot(sel, expand) > 0.5
        mask = chosen & (key_i + base <= t_s)
        new_m, new_l, new_acc = [], [], []
        for r in range(R):
            s_r = jnp.where(mask, s[r * Q:(r + 1) * Q], NEG_INF)
            m_new = jnp.maximum(ms[r], jnp.max(s_r, axis=-1, keepdims=True))
            alpha = jnp.exp(ms[r] - m_new)
            p = jnp.exp(s_r - m_new)
            new_m.append(m_new)
            new_l.append(alpha * ls[r] + jnp.sum(p, axis=-1, keepdims=True))
            new_acc.append(alpha * accs[r] + _dot(p.astype(BF16), v_t))
        return tuple(new_m), tuple(new_l), tuple(new_acc)

    init = (tuple(jnp.full((Q, 1), NEG_INF, F32) for _ in range(R)),
            tuple(jnp.zeros((Q, 1), F32) for _ in range(R)),
            tuple(jnp.zeros((Q, dk), F32) for _ in range(R)))
    _, ls, accs = lax.fori_loop(0, n_tiles, sel_tile, init)
    o_s = [accs[r] / jnp.maximum(ls[r], TINY) for r in range(R)]

    gate = gate_ref[0, 0]
    for r in range(R):
        out = (gate[:, 3 * r:3 * r + 1] * o_c[r] + gate[:, 3 * r + 1:3 * r + 2] * o_s[r]
               + gate[:, 3 * r + 2:3 * r + 3] * o_w[r])
        o_ref[0, :, r * dk:(r + 1) * dk] = out.astype(BF16)


def _nsa(q, ks, vs, kw, vw, kcmp, vcmp, gates, overlap_t):
    B, G, R, S, dk = q.shape
    n_cmp = kcmp.shape[2]
    n_slc = overlap_t.shape[0]
    kv_spec = pl.BlockSpec((1, 1, S, dk), lambda b, g, i: (b, g, 0, 0))
    cmp_spec = pl.BlockSpec((1, 1, n_cmp, dk), lambda b, g, i: (b, g, 0, 0))
    return pl.pallas_call(
        _nsa_kernel,
        grid=(B, G, S // Q_BLOCK),
        in_specs=[
            pl.BlockSpec((1, 1, R, Q_BLOCK, dk), lambda b, g, i: (b, g, 0, i, 0)),
            kv_spec, kv_spec, kv_spec, kv_spec,
            cmp_spec, cmp_spec,
            pl.BlockSpec((1, 1, Q_BLOCK, LANES), lambda b, g, i: (b, g, i, 0)),
            pl.BlockSpec((n_slc, n_cmp), lambda b, g, i: (0, 0)),
        ],
        out_specs=pl.BlockSpec((1, Q_BLOCK, R * dk), lambda b, g, i: (b, i, g)),
        out_shape=jax.ShapeDtypeStruct((B, S, G * R * dk), BF16),
        compiler_params=_params(3),
        name="nsa",
    )(q, ks, vs, kw, vw, kcmp, vcmp, gates, overlap_t)


def _mix_kernel(x_ref, onsa_ref, u_ref, v_ref, mg_ref, lng_ref, lnb_ref, ws_ref, bs_ref,
                wn_ref, wg_ref, wm_ref, h_ref):
    D = x_ref.shape[1]
    u = _gelu(u_ref[...])
    v = _gelu(v_ref[...])
    mu = jnp.mean(v, axis=-1, keepdims=True)
    var = jnp.mean(jnp.square(v - mu), axis=-1, keepdims=True)
    vn = ((v - mu) * lax.rsqrt(var + NORM_EPS) * lng_ref[...] + lnb_ref[...]).astype(BF16)
    C = GMLP_CHUNK
    gd = vn.shape[1] // GMLP_GROUPS
    rows = []
    for c in range(vn.shape[0] // C):
        cols = []
        for g in range(GMLP_GROUPS):
            cols.append(_dot(ws_ref[g], vn[c * C:(c + 1) * C, g * gd:(g + 1) * gd]) + bs_ref[g])
        rows.append(jnp.concatenate(cols, axis=1))
    o_gmlp = u * jnp.concatenate(rows, axis=0)
    mg = mg_ref[...]
    y = (mg[:, :D] * _dot(onsa_ref[...], wn_ref[...])
         + mg[:, D:] * _dot(o_gmlp.astype(BF16), wg_ref[...]))
    h_ref[...] = x_ref[...] + _dot(y.astype(BF16), wm_ref[...])


def _mix(x2, o_nsa, u_raw, v_raw, mg, ln_g, ln_b, ws_causal, bs_b, w_nsa_out, w_gmlp_out, w_mix_out):
    T, D = x2.shape
    tm = MIX_ROWS
    W = u_raw.shape[1]

    def tok(i):
        return (i, 0)

    def const2(i):
        return (0, 0)

    def const3(i):
        return (0, 0, 0)

    return pl.pallas_call(
        _mix_kernel,
        grid=(T // tm,),
        in_specs=[
            pl.BlockSpec((tm, D), tok),
            pl.BlockSpec((tm, o_nsa.shape[1]), tok),
            pl.BlockSpec((tm, W), tok),
            pl.BlockSpec((tm, W), tok),
            pl.BlockSpec((tm, 2 * D), tok),
            pl.BlockSpec((1, W), const2),
            pl.BlockSpec((1, W), const2),
            pl.BlockSpec(ws_causal.shape, const3),
            pl.BlockSpec(bs_b.shape, const3),
            pl.BlockSpec(w_nsa_out.shape, const2),
            pl.BlockSpec(w_gmlp_out.shape, const2),
            pl.BlockSpec(w_mix_out.shape, const2),
        ],
        out_specs=pl.BlockSpec((tm, D), tok),
        out_shape=jax.ShapeDtypeStruct((T, D), F32),
        compiler_params=_params(1),
        name="mix",
    )(x2, o_nsa, u_raw, v_raw, mg, ln_g.reshape(1, W), ln_b.reshape(1, W), ws_causal, bs_b,
      w_nsa_out, w_gmlp_out, w_mix_out)


def _mem_kv_kernel(mem_ref, g_ref, w_ref, k_ref, v_ref):
    D = mem_ref.shape[2]
    mn = _rms(mem_ref[0], g_ref[...]).astype(BF16)
    kv = _dot(mn, w_ref[...])
    k_ref[0] = kv[:, :D].astype(BF16)
    v_ref[0] = kv[:, D:].astype(BF16)


def _mem_kv(mem, g_mem, w_xkv):
    B, M, D = mem.shape
    spec = pl.BlockSpec((1, M, D), lambda b: (b, 0, 0))
    return pl.pallas_call(
        _mem_kv_kernel,
        grid=(B,),
        in_specs=[spec, pl.BlockSpec((1, D), lambda b: (0, 0)),
                  pl.BlockSpec((D, 2 * D), lambda b: (0, 0))],
        out_specs=(spec, spec),
        out_shape=(jax.ShapeDtypeStruct((B, M, D), BF16), jax.ShapeDtypeStruct((B, M, D), BF16)),
        compiler_params=_params(1),
        name="mem_kv",
    )(mem, g_mem.reshape(1, D), w_xkv)


def _xattn_kernel(h_ref, g_ref, wq_ref, k_ref, v_ref, wo_ref, o_ref):
    h = h_ref[...]
    D = h.shape[1]
    hd = D // XATTN_HEADS
    hn = _rms(h, g_ref[...]).astype(BF16)
    q = _dot(hn, wq_ref[...]) * (hd ** -0.5)
    outs = []
    for a in range(XATTN_HEADS):
        s = _dot_nt(q[:, a * hd:(a + 1) * hd].astype(BF16), k_ref[0, :, a * hd:(a + 1) * hd])
        m = jnp.max(s, axis=-1, keepdims=True)
        e = jnp.exp(s - m)
        p = e / jnp.sum(e, axis=-1, keepdims=True)
        outs.append(_dot(p.astype(BF16), v_ref[0, :, a * hd:(a + 1) * hd]))
    o = jnp.concatenate(outs, axis=1).astype(BF16)
    o_ref[...] = h + _dot(o, wo_ref[...])


def _xattn(h, g_xattn, w_xq, k_mem, v_mem, w_xo, S):
    T, D = h.shape
    tm = MIX_ROWS
    per_b = S // tm
    M = k_mem.shape[1]
    mem_spec = pl.BlockSpec((1, M, D), lambda i: (i // per_b, 0, 0))
    return pl.pallas_call(
        _xattn_kernel,
        grid=(T // tm,),
        in_specs=[
            pl.BlockSpec((tm, D), lambda i: (i, 0)),
            pl.BlockSpec((1, D), lambda i: (0, 0)),
            pl.BlockSpec((D, D), lambda i: (0, 0)),
            mem_spec, mem_spec,
            pl.BlockSpec((D, D), lambda i: (0, 0)),
        ],
        out_specs=pl.BlockSpec((tm, D), lambda i: (i, 0)),
        out_shape=jax.ShapeDtypeStruct((T, D), F32),
        compiler_params=_params(1),
        name="xattn",
    )(h, g_xattn.reshape(1, D), w_xq, k_mem, v_mem, w_xo)


def _top_rows(vals, k, row=None):
    if row is None:
        row = lax.broadcasted_iota(I32, vals.shape, 0).astype(F32)
    top_v, top_i = [], []
    for _ in range(k):
        m = jnp.max(vals, axis=0, keepdims=True)
        first = jnp.min(jnp.where(vals == m, row, 1.0e9), axis=0, keepdims=True)
        top_v.append(m)
        top_i.append(first)
        vals = jnp.where(row == first, -3.0e38, vals)
    return top_v, top_i


def _pair_candidates(s0_all, s1_all):
    K, n_tok = s0_all.shape
    sub = lax.broadcasted_iota(I32, (SUBLANES, n_tok), 0).astype(F32)
    vals, ids = [], []
    a = 0
    while K // (a + 1) > 1:
        nb = K // (a + 1)
        for b0 in range(0, nb, SUBLANES):
            v = s0_all[a:a + 1] + s1_all[b0:b0 + SUBLANES]
            vals.append(v if b0 + SUBLANES <= nb else jnp.where(sub < float(nb - b0), v, -3.0e38))
            ids.append(sub + float(a * K + b0))
        a += 1
    while a < K:
        vals.append(s0_all[a:a + SUBLANES] + s1_all[0:1])
        ids.append((sub + float(a)) * float(K))
        a += SUBLANES
    return jnp.concatenate(vals, axis=0), jnp.concatenate(ids, axis=0)


def _route_kernel(h_ref, g_ref, wq_ref, sk_ref, hn_ref, i_ref, j_ref, gate_ref,
                  qt_scr, i_scr, j_scr, gate_scr):
    K = PEER_TOPK
    hn = _rms(h_ref[...], g_ref[...]).astype(BF16)
    hn_ref[...] = hn
    qt_scr[...] = _dot_nt(wq_ref[...], hn)
    half_dim = sk_ref.shape[2]
    n_tok = hn.shape[0]
    cand_a = lax.broadcasted_iota(I32, (K, n_tok), 0).astype(F32)

    def head(hd, _):
        tops = []
        for p in range(2):
            off = pl.multiple_of((hd * 2 + p) * half_dim, half_dim)
            qs = qt_scr[pl.ds(off, half_dim), :].astype(BF16)
            sc = _dot(sk_ref[hd * 2 + p], qs)
            tops.append(_top_rows(sc, K))
        (s0, i0), (s1, i1) = tops
        s0_all = jnp.concatenate(s0, axis=0)
        s1_all = jnp.concatenate(s1, axis=0)
        i0_all = jnp.concatenate(i0, axis=0)
        i1_all = jnp.concatenate(i1, axis=0)
        cand, cand_id = _pair_candidates(s0_all, s1_all)
        best_s, pos = _top_rows(cand, K, cand_id)
        best = jnp.concatenate(best_s, axis=0)
        m = jnp.max(best, axis=0, keepdims=True)
        e = jnp.exp(best - m)
        gate = e / jnp.sum(e, axis=0, keepdims=True)
        ei, ej = [], []
        for kk in range(K):
            a = jnp.floor(pos[kk] * (1.0 / K))
            b = pos[kk] - a * K
            ei.append(jnp.sum(jnp.where(cand_a == a, i0_all, 0.0), axis=0, keepdims=True))
            ej.append(jnp.sum(jnp.where(cand_a == b, i1_all, 0.0), axis=0, keepdims=True))
        row0 = pl.multiple_of(hd * K, K)
        i_scr[pl.ds(row0, K), :] = jnp.concatenate(ei, axis=0)
        j_scr[pl.ds(row0, K), :] = jnp.concatenate(ej, axis=0)
        gate_scr[pl.ds(row0, K), :] = gate
        return 0

    lax.fori_loop(0, PEER_HEADS, head, 0)
    i_ref[...] = i_scr[...].T.astype(I32)
    j_ref[...] = j_scr[...].T.astype(I32)
    gate_ref[...] = gate_scr[...].T


def _route(h, g_peer, wq_t, sub_keys):
    T, D = h.shape
    tt = ROUTE_TOKENS
    n_sel = PEER_HEADS * PEER_TOPK
    half_dim = sub_keys.shape[2]
    tok = lambda i: (i, 0)
    return pl.pallas_call(
        _route_kernel,
        grid=(T // tt,),
        in_specs=[
            pl.BlockSpec((tt, D), tok),
            pl.BlockSpec((1, D), lambda i: (0, 0)),
            pl.BlockSpec(wq_t.shape, lambda i: (0, 0)),
            pl.BlockSpec(sub_keys.shape, lambda i: (0, 0, 0)),
        ],
        out_specs=(
            pl.BlockSpec((tt, D), tok),
            pl.BlockSpec((tt, n_sel), tok),
            pl.BlockSpec((tt, n_sel), tok),
            pl.BlockSpec((tt, n_sel), tok),
        ),
        out_shape=(
            jax.ShapeDtypeStruct((T, D), BF16),
            jax.ShapeDtypeStruct((T, n_sel), I32),
            jax.ShapeDtypeStruct((T, n_sel), I32),
            jax.ShapeDtypeStruct((T, n_sel), F32),
        ),
        scratch_shapes=[
            pltpu.VMEM((wq_t.shape[0], tt), F32),
            pltpu.VMEM((n_sel, tt), F32),
            pltpu.VMEM((n_sel, tt), F32),
            pltpu.VMEM((n_sel, tt), F32),
        ],
        compiler_params=_params(1),
        name="route",
    )(h, g_peer.reshape(1, D), wq_t, sub_keys)


def _peer_up_kernel(hn_ref, u_ref, i_ref, j_ref, a_ref):
    e = pl.program_id(1)

    @pl.when(e == 0)
    def _():
        a_ref[...] = jnp.zeros(a_ref.shape, F32)

    z = _dot_nt(hn_ref[...], u_ref[...])
    ii = i_ref[...]
    jj = j_ref[...]
    acc = a_ref[...]
    for r in range(EXPERT_ROWS):
        picked = jnp.take_along_axis(z[:, r * PEER_KEYS:(r + 1) * PEER_KEYS], jj, axis=1)
        acc = jnp.where(ii == e * EXPERT_ROWS + r, picked, acc)
    a_ref[...] = acc


def _peer_up(hn, u_bf, i_idx, j_idx):
    T, D = hn.shape
    tt = UP_TOKENS
    n_sel = i_idx.shape[1]
    rows = EXPERT_ROWS * PEER_KEYS
    tok = lambda t, e: (t, 0)
    return pl.pallas_call(
        _peer_up_kernel,
        grid=(T // tt, u_bf.shape[0] // rows),
        in_specs=[
            pl.BlockSpec((tt, D), tok),
            pl.BlockSpec((rows, D), lambda t, e: (e, 0)),
            pl.BlockSpec((tt, n_sel), tok),
            pl.BlockSpec((tt, n_sel), tok),
        ],
        out_specs=pl.BlockSpec((tt, n_sel), tok),
        out_shape=jax.ShapeDtypeStruct((T, n_sel), F32),
        compiler_params=_params(2),
        name="peer_up",
    )(hn, u_bf, i_idx, j_idx)


def _peer_down_kernel(a_ref, gate_ref, i_ref, j_ref, v_ref, h_ref, g_ref, o_ref,
                      w_scr, coef_scr, acc_scr):
    e = pl.program_id(1)
    n_tok = a_ref.shape[0]
    nk = PEER_KEYS

    @pl.when(e == 0)
    def _():
        coef_scr[...] = gate_ref[...] * _gelu(a_ref[...])
        acc_scr[...] = jnp.zeros(acc_scr.shape, F32)
        sub = lax.broadcasted_iota(I32, (nk, i_ref.shape[1]), 0)

        def expand(t, _):
            ii = i_ref[pl.ds(t, 1), :]
            jj = j_ref[pl.ds(t, 1), :]
            cc = coef_scr[pl.ds(t, 1), :]
            a = jnp.where(sub == ii, cc, 0.0).astype(BF16)
            bt = jnp.where(sub == jj, 1.0, 0.0).astype(BF16)
            base = pl.multiple_of(t * W_ROW_STRIDE, SUBLANES)
            w_scr[pl.ds(base, nk), :] = _dot_nt(a, bt)
            return 0

        lax.fori_loop(0, n_tok, expand, 0, unroll=EXPAND_UNROLL)

    parts = []
    for r in range(DOWN_EXPERT_ROWS):
        row = e * DOWN_EXPERT_ROWS + r
        parts.append(w_scr[pl.ds(row, n_tok, stride=W_ROW_STRIDE), :].astype(BF16))
    lhs = jnp.concatenate(parts, axis=1)
    acc_scr[...] += _dot(lhs, v_ref[...])

    @pl.when(e == pl.num_programs(1) - 1)
    def _():
        o_ref[...] = _rms(h_ref[...] + acc_scr[...], g_ref[...])


def _peer_down(a_pre, gate, i_idx, j_idx, v_bf, h, g_final):
    T, D = h.shape
    tt = DOWN_TOKENS
    n_sel = i_idx.shape[1]
    rows = DOWN_EXPERT_ROWS * PEER_KEYS
    tok = lambda t, e: (t, 0)
    sel_spec = pl.BlockSpec((tt, n_sel), tok)
    return pl.pallas_call(
        _peer_down_kernel,
        grid=(T // tt, v_bf.shape[0] // rows),
        in_specs=[
            sel_spec, sel_spec, sel_spec, sel_spec,
            pl.BlockSpec((rows, D), lambda t, e: (e, 0)),
            pl.BlockSpec((tt, D), tok),
            pl.BlockSpec((1, D), lambda t, e: (0, 0)),
        ],
        out_specs=pl.BlockSpec((tt, D), tok),
        out_shape=jax.ShapeDtypeStruct((T, D), F32),
        scratch_shapes=[
            pltpu.VMEM((tt * W_ROW_STRIDE, PEER_KEYS), F32),
            pltpu.VMEM((tt, n_sel), F32),
            pltpu.VMEM((tt, D), F32),
        ],
        compiler_params=_params(2),
        name="peer_down",
    )(a_pre, gate, i_idx, j_idx, v_bf, h, g_final.reshape(1, D))


def _overlap_table(n_cmp, n_slc):
    c0 = jnp.arange(n_cmp) * CMP_STRIDE
    j0 = jnp.arange(n_slc) * SLC_BLOCK
    ov = jnp.clip(jnp.minimum(c0[:, None] + CMP_BLOCK, j0[None, :] + SLC_BLOCK)
                  - jnp.maximum(c0[:, None], j0[None, :]), 0, None)
    return ov.astype(F32) / CMP_BLOCK


def kernel(x, mem, g_mix, w_in, cmp_pe_k, cmp_w1_k, cmp_b1_k, cmp_w2_k, cmp_pe_v, cmp_w1_v, cmp_b1_v, cmp_w2_v, gmlp_ln_g, gmlp_ln_b, gmlp_ws, gmlp_bs, w_nsa_out, w_gmlp_out, w_mix_out, g_xattn, g_mem, w_xq, w_xkv, w_xo, g_peer, w_peer_q, peer_sub_keys, peer_u, peer_v, g_final):
    B, S, D = x.shape
    T = B * S
    G, R, dk = NSA_GROUPS, NSA_REP, NSA_HEAD_DIM
    x2 = x.reshape(T, D)

    n_gate = NSA_HEADS * 3
    gate_cols = w_in[:, 1280:1280 + n_gate]
    per_g = n_gate // G
    gate_blocks = [jnp.pad(gate_cols[:, g * per_g:(g + 1) * per_g], ((0, 0), (0, LANES - per_g)))
                   for g in range(G)]
    w_re = jnp.concatenate([w_in[:, :1280]] + gate_blocks + [w_in[:, 1280 + n_gate:]],
                           axis=1).astype(BF16)
    half = CMP_STRIDE * dk
    pe = jnp.stack([cmp_pe_k.reshape(2, half), cmp_pe_v.reshape(2, half)])
    w1 = jnp.stack([cmp_w1_k, cmp_w1_v]).astype(BF16)
    b1 = jnp.stack([cmp_b1_k, cmp_b1_v])[:, None, :]
    w2 = jnp.stack([cmp_w2_k, cmp_w2_v]).astype(BF16)
    C = GMLP_CHUNK
    ws_causal = (gmlp_ws * jnp.tril(jnp.ones((C, C), F32))).astype(BF16)
    group_dim = gmlp_ln_g.shape[0] // GMLP_GROUPS
    bs_b = jnp.broadcast_to(gmlp_bs[:, :, None], (GMLP_GROUPS, C, group_dim))
    n_chunks = S // CMP_STRIDE
    overlap = _overlap_table(n_chunks, S // SLC_BLOCK)

    q, kvc, ks, vs, kw, vw, gates, u_raw, v_raw, mg = _in_proj(x2, g_mix, w_re, B, S)
    cmp_kv = _compress(kvc.reshape(2, B * G, n_chunks, half), pe, w1, b1, w2)
    cmp_kv = cmp_kv.reshape(2, B, G, n_chunks, dk)
    o_nsa = _nsa(q, ks, vs, kw, vw, cmp_kv[0], cmp_kv[1], gates, overlap.T)
    h = _mix(x2, o_nsa.reshape(T, G * R * dk), u_raw, v_raw, mg, gmlp_ln_g, gmlp_ln_b,
             ws_causal, bs_b, w_nsa_out.astype(BF16), w_gmlp_out.astype(BF16),
             w_mix_out.astype(BF16))

    k_mem, v_mem = _mem_kv(mem, g_mem, w_xkv.astype(BF16))
    h = _xattn(h, g_xattn, w_xq.astype(BF16), k_mem, v_mem, w_xo.astype(BF16), S)

    half_dim = peer_sub_keys.shape[3]
    sk = peer_sub_keys.reshape(PEER_HEADS * 2, PEER_KEYS, half_dim).astype(BF16)
    hn, i_idx, j_idx, gate = _route(h, g_peer, w_peer_q.T.astype(BF16), sk)
    a_pre = _peer_up(hn, peer_u.astype(BF16), i_idx, j_idx)
    out = _peer_down(a_pre, gate, i_idx, j_idx, peer_v.astype(BF16), h, g_final)
    return out.reshape(B, S, D)
```

```python
import functools

import jax
import jax.numpy as jnp
from jax import lax
from jax.experimental import pallas as pl
from jax.experimental.pallas import tpu as pltpu

F32 = jnp.float32
BF16 = jnp.bfloat16
I32 = jnp.int32

NORM_EPS = 1e-6
NEG_INF = -1e30
TINY = 1e-30
LOG2_E = 1.4426950408889634

NSA_HEADS = 8
NSA_HEAD_DIM = 64
NSA_GROUPS = 2
NSA_REP = NSA_HEADS // NSA_GROUPS
CMP_BLOCK = 32
CMP_STRIDE = 16
SLC_BLOCK = 64
SLC_TOPN = 16
WINDOW = 512
FORCE_BONUS = 1e4
GMLP_GROUPS = 4
GMLP_CHUNK = 128
XATTN_HEADS = 4
PEER_HEADS = 8
PEER_KEYS = 128
PEER_TOPK = 16

LANES = 128
SUBLANES = 8
VMEM_LIMIT_BYTES = 56 * 1024 * 1024

PROJ_ROWS = 512
MIX_ROWS = 256
NSA_QUERIES = 256
SEL_KEYS = 1024
ROUTE_TOKENS = 512
UP_TOKENS = 512
DOWN_TOKENS = 512
EXPERT_ROWS = 16
DOWN_EXPERT_ROWS = 8
W_ROW_STRIDE = PEER_KEYS + SUBLANES
EXPAND_UNROLL = 32


def _params(n_axes):
    return pltpu.CompilerParams(
        dimension_semantics=("arbitrary",) * n_axes,
        vmem_limit_bytes=VMEM_LIMIT_BYTES,
    )


def _rms(x, g):
    return x * lax.rsqrt(jnp.mean(x * x, axis=-1, keepdims=True) + NORM_EPS) * g


def _gelu(x):
    return 0.5 * x * (1.0 + lax.erf(x * 0.7071067811865476))


def _sigmoid(x):
    return 1.0 / (1.0 + jnp.exp(-x))


def _dot(a, b):
    return jnp.dot(a, b, preferred_element_type=F32)


def _dot_nt(a, b):
    return lax.dot_general(a, b, (((1,), (1,)), ((), ())), preferred_element_type=F32)


def _masked_softmax(s, mask):
    s = jnp.where(mask, s, NEG_INF)
    m = jnp.max(s, axis=-1, keepdims=True)
    p = jnp.where(mask, jnp.exp(s - m), 0.0)
    return p / jnp.maximum(jnp.sum(p, axis=-1, keepdims=True), TINY)


_C_Q = 0
_C_KV = 512
_C_GATE = 1280
_C_U = 1536
_C_V = 2048
_C_MERGE = 2560
_C_END = 4608


def _in_proj_kernel(x_ref, g_ref, w_ref, q_ref, kvc_ref, ks_ref, vs_ref, kw_ref, vw_ref,
                    gate_ref, u_ref, v_ref, mg_ref):
    hn = _rms(x_ref[...], g_ref[...]).astype(BF16)

    def proj(a, b):
        return _dot(hn, w_ref[:, a:b])

    dk = NSA_HEAD_DIM
    pq = proj(_C_Q, _C_KV) * (dk ** -0.5 * LOG2_E)
    for g in range(NSA_GROUPS):
        for r in range(NSA_REP):
            c = (g * NSA_REP + r) * dk
            q_ref[0, g, r] = pq[:, c:c + dk].astype(BF16)
    pk = proj(_C_KV, _C_GATE)
    for g in range(NSA_GROUPS):
        kvc_ref[0, 0, g] = pk[:, 0 * 128 + g * dk:0 * 128 + (g + 1) * dk]
        kvc_ref[1, 0, g] = pk[:, 1 * 128 + g * dk:1 * 128 + (g + 1) * dk]
        ks_ref[0, g] = pk[:, 2 * 128 + g * dk:2 * 128 + (g + 1) * dk].astype(BF16)
        vs_ref[0, g] = pk[:, 3 * 128 + g * dk:3 * 128 + (g + 1) * dk].astype(BF16)
        kw_ref[0, g] = pk[:, 4 * 128 + g * dk:4 * 128 + (g + 1) * dk].astype(BF16)
        vw_ref[0, g] = pk[:, 5 * 128 + g * dk:5 * 128 + (g + 1) * dk].astype(BF16)
        gate_ref[0, g] = _sigmoid(proj(_C_GATE + g * 128, _C_GATE + (g + 1) * 128))
    u_ref[...] = proj(_C_U, _C_V)
    v_ref[...] = proj(_C_V, _C_MERGE)
    mg_ref[...] = _sigmoid(proj(_C_MERGE, _C_END))


def _in_proj(x2, g_mix, w_re, B, S):
    T, D = x2.shape
    tm = PROJ_ROWS
    per_b = S // tm
    G, R, dk = NSA_GROUPS, NSA_REP, NSA_HEAD_DIM

    def tok(i):
        return (i, 0)

    def bgs(i):
        return (i // per_b, 0, i % per_b, 0)

    out_shape = (
        jax.ShapeDtypeStruct((B, G, R, S, dk), BF16),
        jax.ShapeDtypeStruct((2, B, G, S, dk), F32),
        jax.ShapeDtypeStruct((B, G, S, dk), BF16),
        jax.ShapeDtypeStruct((B, G, S, dk), BF16),
        jax.ShapeDtypeStruct((B, G, S, dk), BF16),
        jax.ShapeDtypeStruct((B, G, S, dk), BF16),
        jax.ShapeDtypeStruct((B, G, S, LANES), F32),
        jax.ShapeDtypeStruct((T, 512), F32),
        jax.ShapeDtypeStruct((T, 512), F32),
        jax.ShapeDtypeStruct((T, 2 * D), F32),
    )
    kv_spec = pl.BlockSpec((1, G, tm, dk), bgs)
    out_specs = (
        pl.BlockSpec((1, G, R, tm, dk), lambda i: (i // per_b, 0, 0, i % per_b, 0)),
        pl.BlockSpec((2, 1, G, tm, dk), lambda i: (0, i // per_b, 0, i % per_b, 0)),
        kv_spec, kv_spec, kv_spec, kv_spec,
        pl.BlockSpec((1, G, tm, LANES), bgs),
        pl.BlockSpec((tm, 512), tok),
        pl.BlockSpec((tm, 512), tok),
        pl.BlockSpec((tm, 2 * D), tok),
    )
    return pl.pallas_call(
        _in_proj_kernel,
        grid=(T // tm,),
        in_specs=[
            pl.BlockSpec((tm, D), tok),
            pl.BlockSpec((1, D), lambda i: (0, 0)),
            pl.BlockSpec((D, _C_END), lambda i: (0, 0)),
        ],
        out_specs=out_specs,
        out_shape=out_shape,
        compiler_params=_params(1),
        name="in_proj",
    )(x2, g_mix.reshape(1, D), w_re)


def _compress_kernel(x_ref, pe_ref, w1_ref, b1_ref, w2_ref, o_ref):
    x = x_ref[0, 0]
    half = x.shape[1]
    lo = (x + pe_ref[0, 0:1, :]).astype(BF16)
    hi = (x + pe_ref[0, 1:2, :]).astype(BF16)
    p = _dot(lo, w1_ref[0, :half, :])
    q = _dot(hi, w1_ref[0, half:, :])
    n = x.shape[0]
    h = p + pltpu.roll(q, n - 1, 0) + b1_ref[0]
    o_ref[0, 0] = _dot(_gelu(h).astype(BF16), w2_ref[0]).astype(BF16)


def _compress(kvc, pe, w1, b1, w2):
    _, BG, n_chunks, width = kvc.shape
    hidden = w1.shape[-1]
    dk = w2.shape[-1]
    return pl.pallas_call(
        _compress_kernel,
        grid=(2, BG),
        in_specs=[
            pl.BlockSpec((1, 1, n_chunks, width), lambda a, b: (a, b, 0, 0)),
            pl.BlockSpec((1, 2, width), lambda a, b: (a, 0, 0)),
            pl.BlockSpec((1, 2 * width, hidden), lambda a, b: (a, 0, 0)),
            pl.BlockSpec((1, 1, hidden), lambda a, b: (a, 0, 0)),
            pl.BlockSpec((1, hidden, dk), lambda a, b: (a, 0, 0)),
        ],
        out_specs=pl.BlockSpec((1, 1, n_chunks, dk), lambda a, b: (a, b, 0, 0)),
        out_shape=jax.ShapeDtypeStruct((2, BG, n_chunks, dk), BF16),
        compiler_params=_params(2),
        name="compress",
    )(kvc, pe, w1, b1, w2)


def _select_blocks(score_t):
    n = score_t.shape[0]
    row = lax.broadcasted_iota(I32, score_t.shape, 0).astype(F32)
    sel = jnp.zeros(score_t.shape, F32)
    s = score_t
    for _ in range(SLC_TOPN):
        m = jnp.max(s, axis=0, keepdims=True)
        first = jnp.min(jnp.where(s == m, row, float(n)), axis=0, keepdims=True)
        pick = row == first
        sel = jnp.where(pick & (m > 0.5 * NEG_INF), 1.0, sel)
        s = jnp.where(pick, -3.0e38, s)
    return sel


def _nsa_kernel(q_ref, ks_ref, vs_ref, kw_ref, vw_ref, kc_ref, vc_ref, gate_ref, ovt_ref, o_ref):
    qb = pl.program_id(2)
    q0 = qb * NSA_QUERIES
    R, Q, dk = NSA_REP, NSA_QUERIES, NSA_HEAD_DIM
    q2 = q_ref[0, 0].reshape(R * Q, dk)

    kc = kc_ref[0, 0]
    n_cmp = kc.shape[0]
    s_c = _dot_nt(q2, kc)
    t_c = q0 + lax.broadcasted_iota(I32, (Q, n_cmp), 0)
    cmp_end = lax.broadcasted_iota(I32, (Q, n_cmp), 1) * CMP_STRIDE + (CMP_BLOCK - 1)
    mask_c = cmp_end <= t_c
    o_c = []
    p_sum = jnp.zeros((Q, n_cmp), F32)
    for r in range(R):
        s_r = jnp.where(mask_c, s_c[r * Q:(r + 1) * Q], NEG_INF)
        e = jnp.where(mask_c, jnp.exp2(s_r - jnp.max(s_r, axis=-1, keepdims=True)), 0.0)
        p = e / jnp.maximum(jnp.sum(e, axis=-1, keepdims=True), TINY)
        p_sum = p_sum + p
        o_c.append(_dot(p.astype(BF16), vc_ref[0, 0]))

    span = WINDOW + Q
    start = pl.multiple_of(jnp.maximum(q0 - WINDOW, 0), Q)
    kw = kw_ref[0, 0, pl.ds(start, span), :]
    vw = vw_ref[0, 0, pl.ds(start, span), :]
    s_w = _dot_nt(q2, kw)
    pos = start + lax.broadcasted_iota(I32, (Q, span), 1)
    diff = q0 + lax.broadcasted_iota(I32, (Q, span), 0) - pos
    mask_w = (diff >= 0) & (diff < WINDOW)
    o_w = []
    for r in range(R):
        s_r = jnp.where(mask_w, s_w[r * Q:(r + 1) * Q], NEG_INF)
        e = jnp.exp2(s_r - jnp.max(s_r, axis=-1, keepdims=True))
        p = e / jnp.maximum(jnp.sum(e, axis=-1, keepdims=True), TINY)
        o_w.append(_dot(p.astype(BF16), vw))

    n_slc = ovt_ref.shape[0]
    imp_t = lax.dot_general(ovt_ref[...], p_sum, (((1,), (1,)), ((), ())),
                            preferred_element_type=F32, precision=lax.Precision.HIGHEST)
    t_b = q0 + lax.broadcasted_iota(I32, (n_slc, Q), 1)
    blk = lax.broadcasted_iota(I32, (n_slc, Q), 0)
    allowed = blk * SLC_BLOCK <= t_b
    cur = t_b // SLC_BLOCK
    forced = (blk == 0) | (blk == cur) | (blk == cur - 1)
    score_t = jnp.where(forced & allowed, FORCE_BONUS, jnp.where(allowed, imp_t, NEG_INF))
    sel = _select_blocks(score_t).T.astype(BF16)

    tk = SEL_KEYS
    blocks_per_tile = tk // SLC_BLOCK
    n_tiles = (q0 + Q + tk - 1) // tk
    key_i = lax.broadcasted_iota(I32, (Q, tk), 1)
    t_s = q0 + lax.broadcasted_iota(I32, (Q, tk), 0)
    e_row = lax.broadcasted_iota(I32, (n_slc, tk), 0)
    e_blk = lax.broadcasted_iota(I32, (n_slc, tk), 1) // SLC_BLOCK

    def sel_tile(kt, carry):
        ms, ls, accs = carry
        base = pl.multiple_of(kt * tk, tk)
        k_t = ks_ref[0, 0, pl.ds(base, tk), :]
        v_t = vs_ref[0, 0, pl.ds(base, tk), :]
        s = _dot_nt(q2, k_t)
        expand = (e_row == e_blk + kt * blocks_per_tile).astype(BF16)
        chosen = _dot(sel, expand) > 0.5
        mask = chosen & (key_i + base <= t_s)
        new_m, new_l, new_acc = [], [], []
        for r in range(R):
            s_r = jnp.where(mask, s[r * Q:(r + 1) * Q], NEG_INF)
            m_new = jnp.maximum(ms[r], jnp.max(s_r, axis=-1, keepdims=True))
            alpha = jnp.exp2(ms[r] - m_new)
            p = jnp.exp2(s_r - m_new)
            new_m.append(m_new)
            new_l.append(alpha * ls[r] + jnp.sum(p, axis=-1, keepdims=True))
            new_acc.append(alpha * accs[r] + _dot(p.astype(BF16), v_t))
        return tuple(new_m), tuple(new_l), tuple(new_acc)

    init = (tuple(jnp.full((Q, 1), NEG_INF, F32) for _ in range(R)),
            tuple(jnp.zeros((Q, 1), F32) for _ in range(R)),
            tuple(jnp.zeros((Q, dk), F32) for _ in range(R)))
    _, ls, accs = lax.fori_loop(0, n_tiles, sel_tile, init)
    o_s = [accs[r] / jnp.maximum(ls[r], TINY) for r in range(R)]

    gate = gate_ref[0, 0]
    for r in range(R):
        out = (gate[:, 3 * r:3 * r + 1] * o_c[r] + gate[:, 3 * r + 1:3 * r + 2] * o_s[r]
               + gate[:, 3 * r + 2:3 * r + 3] * o_w[r])
        o_ref[0, :, r * dk:(r + 1) * dk] = out.astype(BF16)


def _nsa(q, ks, vs, kw, vw, kcmp, vcmp, gates, overlap_t):
    B, G, R, S, dk = q.shape
    n_cmp = kcmp.shape[2]
    n_slc = overlap_t.shape[0]
    kv_spec = pl.BlockSpec((1, 1, S, dk), lambda b, g, i: (b, g, 0, 0))
    cmp_spec = pl.BlockSpec((1, 1, n_cmp, dk), lambda b, g, i: (b, g, 0, 0))
    return pl.pallas_call(
        _nsa_kernel,
        grid=(B, G, S // NSA_QUERIES),
        in_specs=[
            pl.BlockSpec((1, 1, R, NSA_QUERIES, dk), lambda b, g, i: (b, g, 0, i, 0)),
            kv_spec, kv_spec, kv_spec, kv_spec,
            cmp_spec, cmp_spec,
            pl.BlockSpec((1, 1, NSA_QUERIES, LANES), lambda b, g, i: (b, g, i, 0)),
            pl.BlockSpec((n_slc, n_cmp), lambda b, g, i: (0, 0)),
        ],
        out_specs=pl.BlockSpec((1, NSA_QUERIES, R * dk), lambda b, g, i: (b, i, g)),
        out_shape=jax.ShapeDtypeStruct((B, S, G * R * dk), BF16),
        compiler_params=_params(3),
        name="nsa",
    )(q, ks, vs, kw, vw, kcmp, vcmp, gates, overlap_t)


def _mix_kernel(x_ref, onsa_ref, u_ref, v_ref, mg_ref, lng_ref, lnb_ref, ws_ref, bs_ref,
                wn_ref, wg_ref, wm_ref, h_ref):
    D = x_ref.shape[1]
    u = _gelu(u_ref[...])
    v = _gelu(v_ref[...])
    mu = jnp.mean(v, axis=-1, keepdims=True)
    var = jnp.mean(jnp.square(v - mu), axis=-1, keepdims=True)
    vn = ((v - mu) * lax.rsqrt(var + NORM_EPS) * lng_ref[...] + lnb_ref[...]).astype(BF16)
    C = GMLP_CHUNK
    gd = vn.shape[1] // GMLP_GROUPS
    rows = []
    for c in range(vn.shape[0] // C):
        cols = []
        for g in range(GMLP_GROUPS):
            cols.append(_dot(ws_ref[g], vn[c * C:(c + 1) * C, g * gd:(g + 1) * gd]) + bs_ref[g])
        rows.append(jnp.concatenate(cols, axis=1))
    o_gmlp = u * jnp.concatenate(rows, axis=0)
    mg = mg_ref[...]
    y = (mg[:, :D] * _dot(onsa_ref[...], wn_ref[...])
         + mg[:, D:] * _dot(o_gmlp.astype(BF16), wg_ref[...]))
    h_ref[...] = x_ref[...] + _dot(y.astype(BF16), wm_ref[...])


def _mix(x2, o_nsa, u_raw, v_raw, mg, ln_g, ln_b, ws_causal, bs_b, w_nsa_out, w_gmlp_out, w_mix_out):
    T, D = x2.shape
    tm = MIX_ROWS
    W = u_raw.shape[1]

    def tok(i):
        return (i, 0)

    def const2(i):
        return (0, 0)

    def const3(i):
        return (0, 0, 0)

    return pl.pallas_call(
        _mix_kernel,
        grid=(T // tm,),
        in_specs=[
            pl.BlockSpec((tm, D), tok),
            pl.BlockSpec((tm, o_nsa.shape[1]), tok),
            pl.BlockSpec((tm, W), tok),
            pl.BlockSpec((tm, W), tok),
            pl.BlockSpec((tm, 2 * D), tok),
            pl.BlockSpec((1, W), const2),
            pl.BlockSpec((1, W), const2),
            pl.BlockSpec(ws_causal.shape, const3),
            pl.BlockSpec(bs_b.shape, const3),
            pl.BlockSpec(w_nsa_out.shape, const2),
            pl.BlockSpec(w_gmlp_out.shape, const2),
            pl.BlockSpec(w_mix_out.shape, const2),
        ],
        out_specs=pl.BlockSpec((tm, D), tok),
        out_shape=jax.ShapeDtypeStruct((T, D), F32),
        compiler_params=_params(1),
        name="mix",
    )(x2, o_nsa, u_raw, v_raw, mg, ln_g.reshape(1, W), ln_b.reshape(1, W), ws_causal, bs_b,
      w_nsa_out, w_gmlp_out, w_mix_out)


def _mem_kv_kernel(mem_ref, g_ref, w_ref, k_ref, v_ref):
    D = mem_ref.shape[2]
    mn = _rms(mem_ref[0], g_ref[...]).astype(BF16)
    kv = _dot(mn, w_ref[...])
    k_ref[0] = kv[:, :D].astype(BF16)
    v_ref[0] = kv[:, D:].astype(BF16)


def _mem_kv(mem, g_mem, w_xkv):
    B, M, D = mem.shape
    spec = pl.BlockSpec((1, M, D), lambda b: (b, 0, 0))
    return pl.pallas_call(
        _mem_kv_kernel,
        grid=(B,),
        in_specs=[spec, pl.BlockSpec((1, D), lambda b: (0, 0)),
                  pl.BlockSpec((D, 2 * D), lambda b: (0, 0))],
        out_specs=(spec, spec),
        out_shape=(jax.ShapeDtypeStruct((B, M, D), BF16), jax.ShapeDtypeStruct((B, M, D), BF16)),
        compiler_params=_params(1),
        name="mem_kv",
    )(mem, g_mem.reshape(1, D), w_xkv)


def _xattn_kernel(h_ref, g_ref, wq_ref, k_ref, v_ref, wo_ref, o_ref):
    h = h_ref[...]
    D = h.shape[1]
    hd = D // XATTN_HEADS
    hn = _rms(h, g_ref[...]).astype(BF16)
    q = _dot(hn, wq_ref[...]) * (hd ** -0.5)
    outs = []
    for a in range(XATTN_HEADS):
        s = _dot_nt(q[:, a * hd:(a + 1) * hd].astype(BF16), k_ref[0, :, a * hd:(a + 1) * hd])
        m = jnp.max(s, axis=-1, keepdims=True)
        e = jnp.exp(s - m)
        p = e / jnp.sum(e, axis=-1, keepdims=True)
        outs.append(_dot(p.astype(BF16), v_ref[0, :, a * hd:(a + 1) * hd]))
    o = jnp.concatenate(outs, axis=1).astype(BF16)
    o_ref[...] = h + _dot(o, wo_ref[...])


def _xattn(h, g_xattn, w_xq, k_mem, v_mem, w_xo, S):
    T, D = h.shape
    tm = MIX_ROWS
    per_b = S // tm
    M = k_mem.shape[1]
    mem_spec = pl.BlockSpec((1, M, D), lambda i: (i // per_b, 0, 0))
    return pl.pallas_call(
        _xattn_kernel,
        grid=(T // tm,),
        in_specs=[
            pl.BlockSpec((tm, D), lambda i: (i, 0)),
            pl.BlockSpec((1, D), lambda i: (0, 0)),
            pl.BlockSpec((D, D), lambda i: (0, 0)),
            mem_spec, mem_spec,
            pl.BlockSpec((D, D), lambda i: (0, 0)),
        ],
        out_specs=pl.BlockSpec((tm, D), lambda i: (i, 0)),
        out_shape=jax.ShapeDtypeStruct((T, D), F32),
        compiler_params=_params(1),
        name="xattn",
    )(h, g_xattn.reshape(1, D), w_xq, k_mem, v_mem, w_xo)


def _top_rows(vals, k, row=None):
    if row is None:
        row = lax.broadcasted_iota(I32, vals.shape, 0).astype(F32)
    top_v, top_i = [], []
    for _ in range(k):
        m = jnp.max(vals, axis=0, keepdims=True)
        first = jnp.min(jnp.where(vals == m, row, 1.0e9), axis=0, keepdims=True)
        top_v.append(m)
        top_i.append(first)
        vals = jnp.where(row == first, -3.0e38, vals)
    return top_v, top_i


def _pair_candidates(s0_all, s1_all):
    K, n_tok = s0_all.shape
    sub = lax.broadcasted_iota(I32, (SUBLANES, n_tok), 0).astype(F32)
    vals, ids = [], []
    a = 0
    while K // (a + 1) > 1:
        nb = K // (a + 1)
        for b0 in range(0, nb, SUBLANES):
            v = s0_all[a:a + 1] + s1_all[b0:b0 + SUBLANES]
            vals.append(v if b0 + SUBLANES <= nb else jnp.where(sub < float(nb - b0), v, -3.0e38))
            ids.append(sub + float(a * K + b0))
        a += 1
    while a < K:
        vals.append(s0_all[a:a + SUBLANES] + s1_all[0:1])
        ids.append((sub + float(a)) * float(K))
        a += SUBLANES
    return jnp.concatenate(vals, axis=0), jnp.concatenate(ids, axis=0)


def _route_head(qt_scr, sk_ref, hd):
    K = PEER_TOPK
    half_dim = sk_ref.shape[2]
    n_tok = qt_scr.shape[1]
    cand_a = lax.broadcasted_iota(I32, (K, n_tok), 0).astype(F32)
    tops = []
    for p in range(2):
        off = pl.multiple_of((hd * 2 + p) * half_dim, half_dim)
        qs = qt_scr[pl.ds(off, half_dim), :].astype(BF16)
        sc = _dot(sk_ref[hd * 2 + p], qs)
        tops.append(_top_rows(sc, K))
    (s0, i0), (s1, i1) = tops
    s0_all = jnp.concatenate(s0, axis=0)
    s1_all = jnp.concatenate(s1, axis=0)
    i0_all = jnp.concatenate(i0, axis=0)
    i1_all = jnp.concatenate(i1, axis=0)
    cand, cand_id = _pair_candidates(s0_all, s1_all)
    best_s, pos = _top_rows(cand, K, cand_id)
    best = jnp.concatenate(best_s, axis=0)
    m = jnp.max(best, axis=0, keepdims=True)
    e = jnp.exp(best - m)
    gate = e / jnp.sum(e, axis=0, keepdims=True)
    ei, ej = [], []
    for kk in range(K):
        a = jnp.floor(pos[kk] * (1.0 / K))
        b = pos[kk] - a * K
        ei.append(jnp.sum(jnp.where(cand_a == a, i0_all, 0.0), axis=0, keepdims=True))
        ej.append(jnp.sum(jnp.where(cand_a == b, i1_all, 0.0), axis=0, keepdims=True))
    return jnp.concatenate(ei, axis=0), jnp.concatenate(ej, axis=0), gate


def _route_kernel(h_ref, g_ref, wq_ref, sk_ref, hn_ref, i_ref, j_ref, gate_ref,
                  qt_scr, i_scr, j_scr, gate_scr):
    K = PEER_TOPK
    hn = _rms(h_ref[...], g_ref[...]).astype(BF16)
    hn_ref[...] = hn
    qt_scr[...] = _dot_nt(wq_ref[...], hn)

    def head(hd, _):
        ei, ej, gate = _route_head(qt_scr, sk_ref, hd)
        row0 = pl.multiple_of(hd * K, K)
        i_scr[pl.ds(row0, K), :] = ei
        j_scr[pl.ds(row0, K), :] = ej
        gate_scr[pl.ds(row0, K), :] = gate
        return 0

    lax.fori_loop(0, PEER_HEADS, head, 0)
    i_ref[...] = i_scr[...].T.astype(I32)
    j_ref[...] = j_scr[...].T.astype(I32)
    gate_ref[...] = gate_scr[...].T


def _route(h, g_peer, wq_t, sub_keys):
    T, D = h.shape
    tt = ROUTE_TOKENS
    n_sel = PEER_HEADS * PEER_TOPK
    tok = lambda i: (i, 0)
    return pl.pallas_call(
        _route_kernel,
        grid=(T // tt,),
        in_specs=[
            pl.BlockSpec((tt, D), tok),
            pl.BlockSpec((1, D), lambda i: (0, 0)),
            pl.BlockSpec(wq_t.shape, lambda i: (0, 0)),
            pl.BlockSpec(sub_keys.shape, lambda i: (0, 0, 0)),
        ],
        out_specs=(
            pl.BlockSpec((tt, D), tok),
            pl.BlockSpec((tt, n_sel), tok),
            pl.BlockSpec((tt, n_sel), tok),
            pl.BlockSpec((tt, n_sel), tok),
        ),
        out_shape=(
            jax.ShapeDtypeStruct((T, D), BF16),
            jax.ShapeDtypeStruct((T, n_sel), I32),
            jax.ShapeDtypeStruct((T, n_sel), I32),
            jax.ShapeDtypeStruct((T, n_sel), F32),
        ),
        scratch_shapes=[
            pltpu.VMEM((wq_t.shape[0], tt), F32),
            pltpu.VMEM((n_sel, tt), F32),
            pltpu.VMEM((n_sel, tt), F32),
            pltpu.VMEM((n_sel, tt), F32),
        ],
        compiler_params=_params(1),
        name="route",
    )(h, g_peer.reshape(1, D), wq_t, sub_keys)


def _peer_up_kernel(hn_ref, u_ref, i_ref, j_ref, a_ref):
    e = pl.program_id(1)

    @pl.when(e == 0)
    def _():
        a_ref[...] = jnp.zeros(a_ref.shape, F32)

    hn = hn_ref[...]
    ii = i_ref[...]
    jj = j_ref[...]
    acc = a_ref[...]
    for r0 in range(0, EXPERT_ROWS, 2):
        z = _dot_nt(hn, u_ref[r0 * PEER_KEYS:(r0 + 2) * PEER_KEYS, :])
        for r in range(r0, r0 + 2):
            picked = jnp.take_along_axis(
                z[:, (r - r0) * PEER_KEYS:(r - r0 + 1) * PEER_KEYS], jj, axis=1)
            acc = jnp.where(ii == e * EXPERT_ROWS + r, picked, acc)
    a_ref[...] = acc


def _peer_up(hn, u_bf, i_idx, j_idx):
    T, D = hn.shape
    tt = UP_TOKENS
    n_sel = i_idx.shape[1]
    rows = EXPERT_ROWS * PEER_KEYS
    tok = lambda t, e: (t, 0)
    return pl.pallas_call(
        _peer_up_kernel,
        grid=(T // tt, u_bf.shape[0] // rows),
        in_specs=[
            pl.BlockSpec((tt, D), tok),
            pl.BlockSpec((rows, D), lambda t, e: (e, 0)),
            pl.BlockSpec((tt, n_sel), tok),
            pl.BlockSpec((tt, n_sel), tok),
        ],
        out_specs=pl.BlockSpec((tt, n_sel), tok),
        out_shape=jax.ShapeDtypeStruct((T, n_sel), F32),
        compiler_params=_params(2),
        name="peer_up",
    )(hn, u_bf, i_idx, j_idx)


def _peer_down_kernel(a_ref, gate_ref, i_ref, j_ref, v_ref, h_ref, g_ref, o_ref,
                      w_scr, coef_scr, acc_scr):
    e = pl.program_id(1)
    n_tok = a_ref.shape[0]
    nk = PEER_KEYS

    @pl.when(e == 0)
    def _():
        coef_scr[...] = gate_ref[...] * _gelu(a_ref[...])
        acc_scr[...] = jnp.zeros(acc_scr.shape, F32)
        sub = lax.broadcasted_iota(I32, (nk, i_ref.shape[1]), 0)

        def expand(t, _):
            ii = i_ref[pl.ds(t, 1), :]
            jj = j_ref[pl.ds(t, 1), :]
            cc = coef_scr[pl.ds(t, 1), :]
            a = jnp.where(sub == ii, cc, 0.0).astype(BF16)
            bt = jnp.where(sub == jj, 1.0, 0.0).astype(BF16)
            base = pl.multiple_of(t * W_ROW_STRIDE, SUBLANES)
            w_scr[pl.ds(base, nk), :] = _dot_nt(a, bt)
            return 0

        lax.fori_loop(0, n_tok, expand, 0, unroll=EXPAND_UNROLL)

    parts = []
    for r in range(DOWN_EXPERT_ROWS):
        row = e * DOWN_EXPERT_ROWS + r
        parts.append(w_scr[pl.ds(row, n_tok, stride=W_ROW_STRIDE), :].astype(BF16))
    lhs = jnp.concatenate(parts, axis=1)
    acc_scr[...] += _dot(lhs, v_ref[...])

    @pl.when(e == pl.num_programs(1) - 1)
    def _():
        o_ref[...] = _rms(h_ref[...] + acc_scr[...], g_ref[...])


def _peer_down(a_pre, gate, i_idx, j_idx, v_bf, h, g_final):
    T, D = h.shape
    tt = DOWN_TOKENS
    n_sel = i_idx.shape[1]
    rows = DOWN_EXPERT_ROWS * PEER_KEYS
    tok = lambda t, e: (t, 0)
    sel_spec = pl.BlockSpec((tt, n_sel), tok)
    return pl.pallas_call(
        _peer_down_kernel,
        grid=(T // tt, v_bf.shape[0] // rows),
        in_specs=[
            sel_spec, sel_spec, sel_spec, sel_spec,
            pl.BlockSpec((rows, D), lambda t, e: (e, 0)),
            pl.BlockSpec((tt, D), tok),
            pl.BlockSpec((1, D), lambda t, e: (0, 0)),
        ],
        out_specs=pl.BlockSpec((tt, D), tok),
        out_shape=jax.ShapeDtypeStruct((T, D), F32),
        scratch_shapes=[
            pltpu.VMEM((tt * W_ROW_STRIDE, PEER_KEYS), F32),
            pltpu.VMEM((tt, n_sel), F32),
            pltpu.VMEM((tt, D), F32),
        ],
        compiler_params=_params(2),
        name="peer_down",
    )(a_pre, gate, i_idx, j_idx, v_bf, h, g_final.reshape(1, D))


def _overlap_table(n_cmp, n_slc):
    c0 = jnp.arange(n_cmp) * CMP_STRIDE
    j0 = jnp.arange(n_slc) * SLC_BLOCK
    ov = jnp.clip(jnp.minimum(c0[:, None] + CMP_BLOCK, j0[None, :] + SLC_BLOCK)
                  - jnp.maximum(c0[:, None], j0[None, :]), 0, None)
    return ov.astype(F32) / CMP_BLOCK


def kernel(x, mem, g_mix, w_in, cmp_pe_k, cmp_w1_k, cmp_b1_k, cmp_w2_k, cmp_pe_v, cmp_w1_v, cmp_b1_v, cmp_w2_v, gmlp_ln_g, gmlp_ln_b, gmlp_ws, gmlp_bs, w_nsa_out, w_gmlp_out, w_mix_out, g_xattn, g_mem, w_xq, w_xkv, w_xo, g_peer, w_peer_q, peer_sub_keys, peer_u, peer_v, g_final):
    B, S, D = x.shape
    T = B * S
    G, R, dk = NSA_GROUPS, NSA_REP, NSA_HEAD_DIM
    x2 = x.reshape(T, D)

    n_gate = NSA_HEADS * 3
    gate_cols = w_in[:, 1280:1280 + n_gate]
    per_g = n_gate // G
    gate_blocks = [jnp.pad(gate_cols[:, g * per_g:(g + 1) * per_g], ((0, 0), (0, LANES - per_g)))
                   for g in range(G)]
    w_re = jnp.concatenate([w_in[:, :1280]] + gate_blocks + [w_in[:, 1280 + n_gate:]],
                           axis=1).astype(BF16)
    half = CMP_STRIDE * dk
    pe = jnp.stack([cmp_pe_k.reshape(2, half), cmp_pe_v.reshape(2, half)])
    w1 = jnp.stack([cmp_w1_k, cmp_w1_v]).astype(BF16)
    b1 = jnp.stack([cmp_b1_k, cmp_b1_v])[:, None, :]
    w2 = jnp.stack([cmp_w2_k, cmp_w2_v]).astype(BF16)
    C = GMLP_CHUNK
    ws_causal = (gmlp_ws * jnp.tril(jnp.ones((C, C), F32))).astype(BF16)
    group_dim = gmlp_ln_g.shape[0] // GMLP_GROUPS
    bs_b = jnp.broadcast_to(gmlp_bs[:, :, None], (GMLP_GROUPS, C, group_dim))
    n_chunks = S // CMP_STRIDE
    overlap = _overlap_table(n_chunks, S // SLC_BLOCK)

    q, kvc, ks, vs, kw, vw, gates, u_raw, v_raw, mg = _in_proj(x2, g_mix, w_re, B, S)
    cmp_kv = _compress(kvc.reshape(2, B * G, n_chunks, half), pe, w1, b1, w2)
    cmp_kv = cmp_kv.reshape(2, B, G, n_chunks, dk)
    o_nsa = _nsa(q, ks, vs, kw, vw, cmp_kv[0], cmp_kv[1], gates, overlap.T)
    h = _mix(x2, o_nsa.reshape(T, G * R * dk), u_raw, v_raw, mg, gmlp_ln_g, gmlp_ln_b,
             ws_causal, bs_b, w_nsa_out.astype(BF16), w_gmlp_out.astype(BF16),
             w_mix_out.astype(BF16))

    k_mem, v_mem = _mem_kv(mem, g_mem, w_xkv.astype(BF16))
    h = _xattn(h, g_xattn, w_xq.astype(BF16), k_mem, v_mem, w_xo.astype(BF16), S)

    half_dim = peer_sub_keys.shape[3]
    sk = peer_sub_keys.reshape(PEER_HEADS * 2, PEER_KEYS, half_dim).astype(BF16)
    hn, i_idx, j_idx, gate = _route(h, g_peer, w_peer_q.T.astype(BF16), sk)
    a_pre = _peer_up(hn, peer_u.astype(BF16), i_idx, j_idx)
    out = _peer_down(a_pre, gate, i_idx, j_idx, peer_v.astype(BF16), h, g_final)
    return out.reshape(B, S, D)
```

```python
import functools

import jax
import jax.numpy as jnp
from jax import lax
from jax.experimental import pallas as pl
from jax.experimental.pallas import tpu as pltpu

F32 = jnp.float32
BF16 = jnp.bfloat16
I32 = jnp.int32

NORM_EPS = 1e-6
NEG_INF = -1e30
TINY = 1e-30
LOG2_E = 1.4426950408889634
MASK_MARGIN = 300.0

NSA_HEADS = 8
NSA_HEAD_DIM = 64
NSA_GROUPS = 2
NSA_REP = NSA_HEADS // NSA_GROUPS
CMP_BLOCK = 32
CMP_STRIDE = 16
SLC_BLOCK = 64
SLC_TOPN = 16
WINDOW = 512
FORCE_BONUS = 1e4
GMLP_GROUPS = 4
GMLP_CHUNK = 128
XATTN_HEADS = 4
PEER_HEADS = 8
PEER_KEYS = 128
PEER_TOPK = 16

LANES = 128
SUBLANES = 8
VMEM_LIMIT_BYTES = 56 * 1024 * 1024

PROJ_ROWS = 512
MIX_ROWS = 256
NSA_QUERIES = 256
SEL_KEYS = 1024
ROUTE_TOKENS = 512
UP_TOKENS = 512
DOWN_TOKENS = 512
EXPERT_ROWS = 16
DOWN_EXPERT_ROWS = 8
W_ROW_STRIDE = PEER_KEYS + SUBLANES
EXPAND_UNROLL = 32


def _params(n_axes):
    return pltpu.CompilerParams(
        dimension_semantics=("arbitrary",) * n_axes,
        vmem_limit_bytes=VMEM_LIMIT_BYTES,
    )


def _rms(x, g):
    return x * lax.rsqrt(jnp.mean(x * x, axis=-1, keepdims=True) + NORM_EPS) * g


def _gelu(x):
    return 0.5 * x * (1.0 + lax.erf(x * 0.7071067811865476))


def _sigmoid(x):
    return 1.0 / (1.0 + jnp.exp(-x))


def _dot(a, b):
    return jnp.dot(a, b, preferred_element_type=F32)


def _dot_nt(a, b):
    return lax.dot_general(a, b, (((1,), (1,)), ((), ())), preferred_element_type=F32)


def _masked_softmax(s, mask):
    s = jnp.where(mask, s, NEG_INF)
    m = jnp.max(s, axis=-1, keepdims=True)
    p = jnp.where(mask, jnp.exp(s - m), 0.0)
    return p / jnp.maximum(jnp.sum(p, axis=-1, keepdims=True), TINY)


_C_Q = 0
_C_KV = 512
_C_GATE = 1280
_C_U = 1536
_C_V = 2048
_C_MERGE = 2560
_C_END = 4608


def _in_proj_kernel(x_ref, g_ref, w_ref, q_ref, kvc_ref, ks_ref, vs_ref, kw_ref, vw_ref,
                    gate_ref, u_ref, v_ref, mg_ref):
    hn = _rms(x_ref[...], g_ref[...]).astype(BF16)

    def proj(a, b):
        return _dot(hn, w_ref[:, a:b])

    dk = NSA_HEAD_DIM
    pq = proj(_C_Q, _C_KV) * (dk ** -0.5 * LOG2_E)
    for g in range(NSA_GROUPS):
        for r in range(NSA_REP):
            c = (g * NSA_REP + r) * dk
            q_ref[0, g, r] = pq[:, c:c + dk].astype(BF16)
    pk = proj(_C_KV, _C_GATE)
    for g in range(NSA_GROUPS):
        kvc_ref[0, 0, g] = pk[:, 0 * 128 + g * dk:0 * 128 + (g + 1) * dk]
        kvc_ref[1, 0, g] = pk[:, 1 * 128 + g * dk:1 * 128 + (g + 1) * dk]
        ks_ref[0, g] = pk[:, 2 * 128 + g * dk:2 * 128 + (g + 1) * dk].astype(BF16)
        vs_ref[0, g] = pk[:, 3 * 128 + g * dk:3 * 128 + (g + 1) * dk].astype(BF16)
        kw_ref[0, g] = pk[:, 4 * 128 + g * dk:4 * 128 + (g + 1) * dk].astype(BF16)
        vw_ref[0, g] = pk[:, 5 * 128 + g * dk:5 * 128 + (g + 1) * dk].astype(BF16)
        gate_ref[0, g] = _sigmoid(proj(_C_GATE + g * 128, _C_GATE + (g + 1) * 128))
    u_ref[...] = proj(_C_U, _C_V)
    v_ref[...] = proj(_C_V, _C_MERGE)
    mg_ref[...] = _sigmoid(proj(_C_MERGE, _C_END))


def _in_proj(x2, g_mix, w_re, B, S):
    T, D = x2.shape
    tm = PROJ_ROWS
    per_b = S // tm
    G, R, dk = NSA_GROUPS, NSA_REP, NSA_HEAD_DIM

    def tok(i):
        return (i, 0)

    def bgs(i):
        return (i // per_b, 0, i % per_b, 0)

    out_shape = (
        jax.ShapeDtypeStruct((B, G, R, S, dk), BF16),
        jax.ShapeDtypeStruct((2, B, G, S, dk), F32),
        jax.ShapeDtypeStruct((B, G, S, dk), BF16),
        jax.ShapeDtypeStruct((B, G, S, dk), BF16),
        jax.ShapeDtypeStruct((B, G, S, dk), BF16),
        jax.ShapeDtypeStruct((B, G, S, dk), BF16),
        jax.ShapeDtypeStruct((B, G, S, LANES), F32),
        jax.ShapeDtypeStruct((T, 512), F32),
        jax.ShapeDtypeStruct((T, 512), F32),
        jax.ShapeDtypeStruct((T, 2 * D), F32),
    )
    kv_spec = pl.BlockSpec((1, G, tm, dk), bgs)
    out_specs = (
        pl.BlockSpec((1, G, R, tm, dk), lambda i: (i // per_b, 0, 0, i % per_b, 0)),
        pl.BlockSpec((2, 1, G, tm, dk), lambda i: (0, i // per_b, 0, i % per_b, 0)),
        kv_spec, kv_spec, kv_spec, kv_spec,
        pl.BlockSpec((1, G, tm, LANES), bgs),
        pl.BlockSpec((tm, 512), tok),
        pl.BlockSpec((tm, 512), tok),
        pl.BlockSpec((tm, 2 * D), tok),
    )
    return pl.pallas_call(
        _in_proj_kernel,
        grid=(T // tm,),
        in_specs=[
            pl.BlockSpec((tm, D), tok),
            pl.BlockSpec((1, D), lambda i: (0, 0)),
            pl.BlockSpec((D, _C_END), lambda i: (0, 0)),
        ],
        out_specs=out_specs,
        out_shape=out_shape,
        compiler_params=_params(1),
        name="in_proj",
    )(x2, g_mix.reshape(1, D), w_re)


def _compress_kernel(x_ref, pe_ref, w1_ref, b1_ref, w2_ref, o_ref):
    x = x_ref[0, 0]
    half = x.shape[1]
    lo = (x + pe_ref[0, 0:1, :]).astype(BF16)
    hi = (x + pe_ref[0, 1:2, :]).astype(BF16)
    p = _dot(lo, w1_ref[0, :half, :])
    q = _dot(hi, w1_ref[0, half:, :])
    n = x.shape[0]
    h = p + pltpu.roll(q, n - 1, 0) + b1_ref[0]
    o_ref[0, 0] = _dot(_gelu(h).astype(BF16), w2_ref[0]).astype(BF16)


def _compress(kvc, pe, w1, b1, w2):
    _, BG, n_chunks, width = kvc.shape
    hidden = w1.shape[-1]
    dk = w2.shape[-1]
    return pl.pallas_call(
        _compress_kernel,
        grid=(2, BG),
        in_specs=[
            pl.BlockSpec((1, 1, n_chunks, width), lambda a, b: (a, b, 0, 0)),
            pl.BlockSpec((1, 2, width), lambda a, b: (a, 0, 0)),
            pl.BlockSpec((1, 2 * width, hidden), lambda a, b: (a, 0, 0)),
            pl.BlockSpec((1, 1, hidden), lambda a, b: (a, 0, 0)),
            pl.BlockSpec((1, hidden, dk), lambda a, b: (a, 0, 0)),
        ],
        out_specs=pl.BlockSpec((1, 1, n_chunks, dk), lambda a, b: (a, b, 0, 0)),
        out_shape=jax.ShapeDtypeStruct((2, BG, n_chunks, dk), BF16),
        compiler_params=_params(2),
        name="compress",
    )(kvc, pe, w1, b1, w2)


def _select_blocks(score_t):
    n = score_t.shape[0]
    row = lax.broadcasted_iota(I32, score_t.shape, 0).astype(F32)
    s = score_t
    for _ in range(SLC_TOPN):
        m = jnp.max(s, axis=0, keepdims=True)
        first = jnp.min(jnp.where(s == m, row, float(n)), axis=0, keepdims=True)
        s = jnp.where(row == first, -3.0e38, s)
    return jnp.where((s < -2.0e38) & (score_t > 0.5 * NEG_INF), 1.0, 0.0)


def _nsa_kernel(q_ref, ksa_ref, vsa_ref, kw_ref, vw_ref, kc_ref, vc_ref, gate_ref, ovt_ref, o_ref,
                kmax_scr):
    qb = pl.program_id(2)
    q0 = qb * NSA_QUERIES
    R, Q, dk = NSA_REP, NSA_QUERIES, NSA_HEAD_DIM
    q2 = q_ref[0, 0].reshape(R * Q, dk)

    kc = kc_ref[0, 0]
    n_cmp = kc.shape[0]
    s_c = _dot_nt(q2, kc)
    t_c = q0 + lax.broadcasted_iota(I32, (Q, n_cmp), 0)
    cmp_end = lax.broadcasted_iota(I32, (Q, n_cmp), 1) * CMP_STRIDE + (CMP_BLOCK - 1)
    mask_c = cmp_end <= t_c
    o_c = []
    p_sum = jnp.zeros((Q, n_cmp), F32)
    for r in range(R):
        s_r = jnp.where(mask_c, s_c[r * Q:(r + 1) * Q], NEG_INF)
        e = jnp.where(mask_c, jnp.exp2(s_r - jnp.max(s_r, axis=-1, keepdims=True)), 0.0)
        inv = 1.0 / jnp.maximum(jnp.sum(e, axis=-1, keepdims=True), TINY)
        p_sum = p_sum + e * inv
        o_c.append(_dot(e.astype(BF16), vc_ref[0, 0]) * inv)

    span = WINDOW + Q
    start = pl.multiple_of(jnp.maximum(q0 - WINDOW, 0), Q)
    kw = kw_ref[0, 0, pl.ds(start, span), :]
    vw = vw_ref[0, 0, pl.ds(start, span), :]
    s_w = _dot_nt(q2, kw)
    pos = start + lax.broadcasted_iota(I32, (Q, span), 1)
    diff = q0 + lax.broadcasted_iota(I32, (Q, span), 0) - pos
    mask_w = (diff >= 0) & (diff < WINDOW)
    o_w = []
    for r in range(R):
        s_r = jnp.where(mask_w, s_w[r * Q:(r + 1) * Q], NEG_INF)
        e = jnp.exp2(s_r - jnp.max(s_r, axis=-1, keepdims=True))
        inv = 1.0 / jnp.maximum(jnp.sum(e, axis=-1, keepdims=True), TINY)
        o_w.append(_dot(e.astype(BF16), vw) * inv)

    n_slc = ovt_ref.shape[0]
    ovt = ovt_ref[...]
    p_hi = p_sum.astype(BF16)
    rest = p_sum - p_hi.astype(F32)
    p_mid = rest.astype(BF16)
    p_lo = (rest - p_mid.astype(F32)).astype(BF16)
    imp_t = _dot_nt(ovt, p_hi) + _dot_nt(ovt, p_mid) + _dot_nt(ovt, p_lo)
    t_b = q0 + lax.broadcasted_iota(I32, (n_slc, Q), 1)
    blk = lax.broadcasted_iota(I32, (n_slc, Q), 0)
    allowed = blk * SLC_BLOCK <= t_b
    cur = t_b // SLC_BLOCK
    forced = (blk == 0) | (blk == cur) | (blk == cur - 1)
    score_t = jnp.where(forced & allowed, FORCE_BONUS, jnp.where(allowed, imp_t, NEG_INF))
    not_sel = 1.0 - _select_blocks(score_t).T

    tk = SEL_KEYS
    blk_w = ksa_ref.shape[3] - 2 * dk

    @pl.when(qb == 0)
    def _():
        k_all = ksa_ref[0, 0, :, blk_w:blk_w + dk].astype(F32)
        k_sq = jnp.max(jnp.sum(k_all * k_all, axis=-1, keepdims=True), axis=0, keepdims=True)
        kmax_scr[...] = jnp.broadcast_to(jnp.sqrt(k_sq), kmax_scr.shape)

    q_f = q2.astype(F32)
    q_sq = jnp.max(jnp.sum(q_f * q_f, axis=-1, keepdims=True), axis=0, keepdims=True)
    drop = (2.0 * jnp.sqrt(q_sq) * kmax_scr[0:1, 0:1] + MASK_MARGIN) * 1.02
    pieces = [not_sel] if blk_w == n_slc else [not_sel, jnp.zeros((Q, blk_w - n_slc), F32)]
    mask_cols = (jnp.concatenate(pieces, axis=1) * drop).astype(BF16)
    q_aug = jnp.concatenate(
        [jnp.concatenate([mask_cols] * R, axis=0), q2, jnp.zeros((R * Q, dk), BF16)], axis=1)
    last = q0 // tk
    va_w = vsa_ref.shape[3]

    def sel_tile(kt, carry, causal):
        ms, accs = carry
        base = pl.multiple_of(kt * tk, tk)
        s = _dot_nt(q_aug, ksa_ref[0, 0, pl.ds(base, tk), :])
        v_t = vsa_ref[0, 0, pl.ds(base, tk), :]
        if causal:
            ahead = (base + lax.broadcasted_iota(I32, (Q, tk), 1)
                     > q0 + lax.broadcasted_iota(I32, (Q, tk), 0))
        new_m, new_acc = [], []
        for r in range(R):
            s_r = s[r * Q:(r + 1) * Q]
            if causal:
                s_r = jnp.where(ahead, NEG_INF, s_r)
            m_new = jnp.maximum(ms[r], jnp.max(s_r, axis=-1, keepdims=True))
            alpha = jnp.exp2(ms[r] - m_new)
            p = jnp.exp2(s_r - m_new)
            new_m.append(m_new)
            new_acc.append(alpha * accs[r] + _dot(p.astype(BF16), v_t))
        return tuple(new_m), tuple(new_acc)

    init = (tuple(jnp.full((Q, 1), NEG_INF, F32) for _ in range(R)),
            tuple(jnp.zeros((Q, va_w), F32) for _ in range(R)))
    carry = lax.fori_loop(0, last, functools.partial(sel_tile, causal=False), init)
    _, accs = sel_tile(last, carry, causal=True)
    o_s = [accs[r][:, :dk] / jnp.maximum(accs[r][:, dk:dk + 1], TINY) for r in range(R)]

    gate = gate_ref[0, 0]
    for r in range(R):
        out = (gate[:, 3 * r:3 * r + 1] * o_c[r] + gate[:, 3 * r + 1:3 * r + 2] * o_s[r]
               + gate[:, 3 * r + 2:3 * r + 3] * o_w[r])
        o_ref[0, :, r * dk:(r + 1) * dk] = out.astype(BF16)


def _augment_selected_kv(ks, vs):
    B, G, S, dk = ks.shape
    n_slc = S // SLC_BLOCK
    blk_w = -(-n_slc // LANES) * LANES
    own = (jnp.arange(S)[:, None] // SLC_BLOCK) == jnp.arange(blk_w)[None, :]
    tab = jnp.where(own, -1.0, 0.0).astype(BF16)
    ksa = jnp.concatenate([jnp.broadcast_to(tab, (B, G, S, blk_w)), ks,
                           jnp.zeros((B, G, S, dk), BF16)], axis=-1)
    vsa = jnp.concatenate([vs, jnp.ones((B, G, S, 1), BF16),
                           jnp.zeros((B, G, S, LANES - dk - 1), BF16)], axis=-1)
    return ksa, vsa


def _nsa(q, ksa, vsa, kw, vw, kcmp, vcmp, gates, overlap_t):
    B, G, R, S, dk = q.shape
    n_cmp = kcmp.shape[2]
    n_slc = overlap_t.shape[0]
    kv_spec = pl.BlockSpec((1, 1, S, dk), lambda b, g, i: (b, g, 0, 0))
    cmp_spec = pl.BlockSpec((1, 1, n_cmp, dk), lambda b, g, i: (b, g, 0, 0))
    return pl.pallas_call(
        _nsa_kernel,
        grid=(B, G, S // NSA_QUERIES),
        in_specs=[
            pl.BlockSpec((1, 1, R, NSA_QUERIES, dk), lambda b, g, i: (b, g, 0, i, 0)),
            pl.BlockSpec((1, 1, S, ksa.shape[3]), lambda b, g, i: (b, g, 0, 0)),
            pl.BlockSpec((1, 1, S, vsa.shape[3]), lambda b, g, i: (b, g, 0, 0)),
            kv_spec, kv_spec,
            cmp_spec, cmp_spec,
            pl.BlockSpec((1, 1, NSA_QUERIES, LANES), lambda b, g, i: (b, g, i, 0)),
            pl.BlockSpec((n_slc, n_cmp), lambda b, g, i: (0, 0)),
        ],
        out_specs=pl.BlockSpec((1, NSA_QUERIES, R * dk), lambda b, g, i: (b, i, g)),
        out_shape=jax.ShapeDtypeStruct((B, S, G * R * dk), BF16),
        scratch_shapes=[pltpu.VMEM((SUBLANES, LANES), F32)],
        compiler_params=_params(3),
        name="nsa",
    )(q, ksa, vsa, kw, vw, kcmp, vcmp, gates, overlap_t)


def _mix_kernel(x_ref, onsa_ref, u_ref, v_ref, mg_ref, lng_ref, lnb_ref, ws_ref, bs_ref,
                wn_ref, wg_ref, wm_ref, h_ref):
    D = x_ref.shape[1]
    u = _gelu(u_ref[...])
    v = _gelu(v_ref[...])
    mu = jnp.mean(v, axis=-1, keepdims=True)
    var = jnp.mean(jnp.square(v - mu), axis=-1, keepdims=True)
    vn = ((v - mu) * lax.rsqrt(var + NORM_EPS) * lng_ref[...] + lnb_ref[...]).astype(BF16)
    C = GMLP_CHUNK
    gd = vn.shape[1] // GMLP_GROUPS
    rows = []
    for c in range(vn.shape[0] // C):
        cols = []
        for g in range(GMLP_GROUPS):
            cols.append(_dot(ws_ref[g], vn[c * C:(c + 1) * C, g * gd:(g + 1) * gd]) + bs_ref[g])
        rows.append(jnp.concatenate(cols, axis=1))
    o_gmlp = u * jnp.concatenate(rows, axis=0)
    mg = mg_ref[...]
    y = (mg[:, :D] * _dot(onsa_ref[...], wn_ref[...])
         + mg[:, D:] * _dot(o_gmlp.astype(BF16), wg_ref[...]))
    h_ref[...] = x_ref[...] + _dot(y.astype(BF16), wm_ref[...])


def _mix(x2, o_nsa, u_raw, v_raw, mg, ln_g, ln_b, ws_causal, bs_b, w_nsa_out, w_gmlp_out, w_mix_out):
    T, D = x2.shape
    tm = MIX_ROWS
    W = u_raw.shape[1]

    def tok(i):
        return (i, 0)

    def const2(i):
        return (0, 0)

    def const3(i):
        return (0, 0, 0)

    return pl.pallas_call(
        _mix_kernel,
        grid=(T // tm,),
        in_specs=[
            pl.BlockSpec((tm, D), tok),
            pl.BlockSpec((tm, o_nsa.shape[1]), tok),
            pl.BlockSpec((tm, W), tok),
            pl.BlockSpec((tm, W), tok),
            pl.BlockSpec((tm, 2 * D), tok),
            pl.BlockSpec((1, W), const2),
            pl.BlockSpec((1, W), const2),
            pl.BlockSpec(ws_causal.shape, const3),
            pl.BlockSpec(bs_b.shape, const3),
            pl.BlockSpec(w_nsa_out.shape, const2),
            pl.BlockSpec(w_gmlp_out.shape, const2),
            pl.BlockSpec(w_mix_out.shape, const2),
        ],
        out_specs=pl.BlockSpec((tm, D), tok),
        out_shape=jax.ShapeDtypeStruct((T, D), F32),
        compiler_params=_params(1),
        name="mix",
    )(x2, o_nsa, u_raw, v_raw, mg, ln_g.reshape(1, W), ln_b.reshape(1, W), ws_causal, bs_b,
      w_nsa_out, w_gmlp_out, w_mix_out)


def _mem_kv_kernel(mem_ref, g_ref, w_ref, k_ref, v_ref):
    D = mem_ref.shape[2]
    mn = _rms(mem_ref[0], g_ref[...]).astype(BF16)
    kv = _dot(mn, w_ref[...])
    k_ref[0] = kv[:, :D].astype(BF16)
    v_ref[0] = kv[:, D:].astype(BF16)


def _mem_kv(mem, g_mem, w_xkv):
    B, M, D = mem.shape
    spec = pl.BlockSpec((1, M, D), lambda b: (b, 0, 0))
    return pl.pallas_call(
        _mem_kv_kernel,
        grid=(B,),
        in_specs=[spec, pl.BlockSpec((1, D), lambda b: (0, 0)),
                  pl.BlockSpec((D, 2 * D), lambda b: (0, 0))],
        out_specs=(spec, spec),
        out_shape=(jax.ShapeDtypeStruct((B, M, D), BF16), jax.ShapeDtypeStruct((B, M, D), BF16)),
        compiler_params=_params(1),
        name="mem_kv",
    )(mem, g_mem.reshape(1, D), w_xkv)


def _xattn_kernel(h_ref, g_ref, wq_ref, k_ref, v_ref, wo_ref, o_ref):
    h = h_ref[...]
    D = h.shape[1]
    hd = D // XATTN_HEADS
    hn = _rms(h, g_ref[...]).astype(BF16)
    q = _dot(hn, wq_ref[...]) * (hd ** -0.5)
    outs = []
    for a in range(XATTN_HEADS):
        s = _dot_nt(q[:, a * hd:(a + 1) * hd].astype(BF16), k_ref[0, :, a * hd:(a + 1) * hd])
        m = jnp.max(s, axis=-1, keepdims=True)
        e = jnp.exp(s - m)
        p = e / jnp.sum(e, axis=-1, keepdims=True)
        outs.append(_dot(p.astype(BF16), v_ref[0, :, a * hd:(a + 1) * hd]))
    o = jnp.concatenate(outs, axis=1).astype(BF16)
    o_ref[...] = h + _dot(o, wo_ref[...])


def _xattn(h, g_xattn, w_xq, k_mem, v_mem, w_xo, S):
    T, D = h.shape
    tm = MIX_ROWS
    per_b = S // tm
    M = k_mem.shape[1]
    mem_spec = pl.BlockSpec((1, M, D), lambda i: (i // per_b, 0, 0))
    return pl.pallas_call(
        _xattn_kernel,
        grid=(T // tm,),
        in_specs=[
            pl.BlockSpec((tm, D), lambda i: (i, 0)),
            pl.BlockSpec((1, D), lambda i: (0, 0)),
            pl.BlockSpec((D, D), lambda i: (0, 0)),
            mem_spec, mem_spec,
            pl.BlockSpec((D, D), lambda i: (0, 0)),
        ],
        out_specs=pl.BlockSpec((tm, D), lambda i: (i, 0)),
        out_shape=jax.ShapeDtypeStruct((T, D), F32),
        compiler_params=_params(1),
        name="xattn",
    )(h, g_xattn.reshape(1, D), w_xq, k_mem, v_mem, w_xo)


def _top_rows(vals, k, row=None):
    if row is None:
        row = lax.broadcasted_iota(I32, vals.shape, 0).astype(F32)
    top_v, top_i = [], []
    for _ in range(k):
        m = jnp.max(vals, axis=0, keepdims=True)
        first = jnp.min(jnp.where(vals == m, row, 1.0e9), axis=0, keepdims=True)
        top_v.append(m)
        top_i.append(first)
        vals = jnp.where(row == first, -3.0e38, vals)
    return top_v, top_i


def _pair_candidates(s0_all, s1_all):
    K, n_tok = s0_all.shape
    sub = lax.broadcasted_iota(I32, (SUBLANES, n_tok), 0).astype(F32)
    vals, ids = [], []
    a = 0
    while K // (a + 1) > 1:
        nb = K // (a + 1)
        for b0 in range(0, nb, SUBLANES):
            v = s0_all[a:a + 1] + s1_all[b0:b0 + SUBLANES]
            vals.append(v if b0 + SUBLANES <= nb else jnp.where(sub < float(nb - b0), v, -3.0e38))
            ids.append(sub + float(a * K + b0))
        a += 1
    while a < K:
        vals.append(s0_all[a:a + SUBLANES] + s1_all[0:1])
        ids.append((sub + float(a)) * float(K))
        a += SUBLANES
    return jnp.concatenate(vals, axis=0), jnp.concatenate(ids, axis=0)


def _route_head(qt_scr, sk_ref, hd):
    K = PEER_TOPK
    half_dim = sk_ref.shape[2]
    n_tok = qt_scr.shape[1]
    cand_a = lax.broadcasted_iota(I32, (K, n_tok), 0).astype(F32)
    tops = []
    for p in range(2):
        off = pl.multiple_of((hd * 2 + p) * half_dim, half_dim)
        qs = qt_scr[pl.ds(off, half_dim), :].astype(BF16)
        sc = _dot(sk_ref[hd * 2 + p], qs)
        tops.append(_top_rows(sc, K))
    (s0, i0), (s1, i1) = tops
    s0_all = jnp.concatenate(s0, axis=0)
    s1_all = jnp.concatenate(s1, axis=0)
    i0_all = jnp.concatenate(i0, axis=0)
    i1_all = jnp.concatenate(i1, axis=0)
    cand, cand_id = _pair_candidates(s0_all, s1_all)
    best_s, pos = _top_rows(cand, K, cand_id)
    best = jnp.concatenate(best_s, axis=0)
    m = jnp.max(best, axis=0, keepdims=True)
    e = jnp.exp(best - m)
    gate = e / jnp.sum(e, axis=0, keepdims=True)
    ei, ej = [], []
    for kk in range(K):
        a = jnp.floor(pos[kk] * (1.0 / K))
        b = pos[kk] - a * K
        ei.append(jnp.sum(jnp.where(cand_a == a, i0_all, 0.0), axis=0, keepdims=True))
        ej.append(jnp.sum(jnp.where(cand_a == b, i1_all, 0.0), axis=0, keepdims=True))
    return jnp.concatenate(ei, axis=0), jnp.concatenate(ej, axis=0), gate


def _route_kernel(h_ref, g_ref, wq_ref, sk_ref, hn_ref, i_ref, j_ref, gate_ref,
                  qt_scr, i_scr, j_scr, gate_scr):
    K = PEER_TOPK
    hn = _rms(h_ref[...], g_ref[...]).astype(BF16)
    hn_ref[...] = hn
    qt_scr[...] = _dot_nt(wq_ref[...], hn)

    def head(hd, _):
        ei, ej, gate = _route_head(qt_scr, sk_ref, hd)
        row0 = pl.multiple_of(hd * K, K)
        i_scr[pl.ds(row0, K), :] = ei
        j_scr[pl.ds(row0, K), :] = ej
        gate_scr[pl.ds(row0, K), :] = gate
        return 0

    lax.fori_loop(0, PEER_HEADS, head, 0)
    i_ref[...] = i_scr[...].T.astype(I32)
    j_ref[...] = j_scr[...].T.astype(I32)
    gate_ref[...] = gate_scr[...].T


def _route(h, g_peer, wq_t, sub_keys):
    T, D = h.shape
    tt = ROUTE_TOKENS
    n_sel = PEER_HEADS * PEER_TOPK
    tok = lambda i: (i, 0)
    return pl.pallas_call(
        _route_kernel,
        grid=(T // tt,),
        in_specs=[
            pl.BlockSpec((tt, D), tok),
            pl.BlockSpec((1, D), lambda i: (0, 0)),
            pl.BlockSpec(wq_t.shape, lambda i: (0, 0)),
            pl.BlockSpec(sub_keys.shape, lambda i: (0, 0, 0)),
        ],
        out_specs=(
            pl.BlockSpec((tt, D), tok),
            pl.BlockSpec((tt, n_sel), tok),
            pl.BlockSpec((tt, n_sel), tok),
            pl.BlockSpec((tt, n_sel), tok),
        ),
        out_shape=(
            jax.ShapeDtypeStruct((T, D), BF16),
            jax.ShapeDtypeStruct((T, n_sel), I32),
            jax.ShapeDtypeStruct((T, n_sel), I32),
            jax.ShapeDtypeStruct((T, n_sel), F32),
        ),
        scratch_shapes=[
            pltpu.VMEM((wq_t.shape[0], tt), F32),
            pltpu.VMEM((n_sel, tt), F32),
            pltpu.VMEM((n_sel, tt), F32),
            pltpu.VMEM((n_sel, tt), F32),
        ],
        compiler_params=_params(1),
        name="route",
    )(h, g_peer.reshape(1, D), wq_t, sub_keys)


def _peer_up_kernel(hn_ref, u_ref, i_ref, j_ref, a_ref):
    e = pl.program_id(1)

    @pl.when(e == 0)
    def _():
        a_ref[...] = jnp.zeros(a_ref.shape, F32)

    hn = hn_ref[...]
    ii = i_ref[...]
    jj = j_ref[...]
    acc = a_ref[...]
    for r0 in range(0, EXPERT_ROWS, 2):
        z = _dot_nt(hn, u_ref[r0 * PEER_KEYS:(r0 + 2) * PEER_KEYS, :])
        for r in range(r0, r0 + 2):
            picked = jnp.take_along_axis(
                z[:, (r - r0) * PEER_KEYS:(r - r0 + 1) * PEER_KEYS], jj, axis=1)
            acc = jnp.where(ii == e * EXPERT_ROWS + r, picked, acc)
    a_ref[...] = acc


def _peer_up(hn, u_bf, i_idx, j_idx):
    T, D = hn.shape
    tt = UP_TOKENS
    n_sel = i_idx.shape[1]
    rows = EXPERT_ROWS * PEER_KEYS
    tok = lambda t, e: (t, 0)
    return pl.pallas_call(
        _peer_up_kernel,
        grid=(T // tt, u_bf.shape[0] // rows),
        in_specs=[
            pl.BlockSpec((tt, D), tok),
            pl.BlockSpec((rows, D), lambda t, e: (e, 0)),
            pl.BlockSpec((tt, n_sel), tok),
            pl.BlockSpec((tt, n_sel), tok),
        ],
        out_specs=pl.BlockSpec((tt, n_sel), tok),
        out_shape=jax.ShapeDtypeStruct((T, n_sel), F32),
        compiler_params=_params(2),
        name="peer_up",
    )(hn, u_bf, i_idx, j_idx)


def _peer_down_kernel(a_ref, gate_ref, i_ref, j_ref, v_ref, h_ref, g_ref, o_ref,
                      w_scr, coef_scr, acc_scr):
    e = pl.program_id(1)
    n_tok = a_ref.shape[0]
    nk = PEER_KEYS

    @pl.when(e == 0)
    def _():
        coef_scr[...] = gate_ref[...] * _gelu(a_ref[...])
        acc_scr[...] = jnp.zeros(acc_scr.shape, F32)
        sub = lax.broadcasted_iota(I32, (nk, i_ref.shape[1]), 0)

        def expand(t, _):
            ii = i_ref[pl.ds(t, 1), :]
            jj = j_ref[pl.ds(t, 1), :]
            cc = coef_scr[pl.ds(t, 1), :]
            a = jnp.where(sub == ii, cc, 0.0).astype(BF16)
            bt = jnp.where(sub == jj, 1.0, 0.0).astype(BF16)
            base = pl.multiple_of(t * W_ROW_STRIDE, SUBLANES)
            w_scr[pl.ds(base, nk), :] = _dot_nt(a, bt)
            return 0

        lax.fori_loop(0, n_tok, expand, 0, unroll=EXPAND_UNROLL)

    parts = []
    for r in range(DOWN_EXPERT_ROWS):
        row = e * DOWN_EXPERT_ROWS + r
        parts.append(w_scr[pl.ds(row, n_tok, stride=W_ROW_STRIDE), :].astype(BF16))
    lhs = jnp.concatenate(parts, axis=1)
    acc_scr[...] += _dot(lhs, v_ref[...])

    @pl.when(e == pl.num_programs(1) - 1)
    def _():
        o_ref[...] = _rms(h_ref[...] + acc_scr[...], g_ref[...])


def _peer_down(a_pre, gate, i_idx, j_idx, v_bf, h, g_final):
    T, D = h.shape
    tt = DOWN_TOKENS
    n_sel = i_idx.shape[1]
    rows = DOWN_EXPERT_ROWS * PEER_KEYS
    tok = lambda t, e: (t, 0)
    sel_spec = pl.BlockSpec((tt, n_sel), tok)
    return pl.pallas_call(
        _peer_down_kernel,
        grid=(T // tt, v_bf.shape[0] // rows),
        in_specs=[
            sel_spec, sel_spec, sel_spec, sel_spec,
            pl.BlockSpec((rows, D), lambda t, e: (e, 0)),
            pl.BlockSpec((tt, D), tok),
            pl.BlockSpec((1, D), lambda t, e: (0, 0)),
        ],
        out_specs=pl.BlockSpec((tt, D), tok),
        out_shape=jax.ShapeDtypeStruct((T, D), F32),
        scratch_shapes=[
            pltpu.VMEM((tt * W_ROW_STRIDE, PEER_KEYS), F32),
            pltpu.VMEM((tt, n_sel), F32),
            pltpu.VMEM((tt, D), F32),
        ],
        compiler_params=_params(2),
        name="peer_down",
    )(a_pre, gate, i_idx, j_idx, v_bf, h, g_final.reshape(1, D))


def _overlap_table(n_cmp, n_slc):
    c0 = jnp.arange(n_cmp) * CMP_STRIDE
    j0 = jnp.arange(n_slc) * SLC_BLOCK
    ov = jnp.clip(jnp.minimum(c0[:, None] + CMP_BLOCK, j0[None, :] + SLC_BLOCK)
                  - jnp.maximum(c0[:, None], j0[None, :]), 0, None)
    return ov.astype(F32) / CMP_BLOCK


def kernel(x, mem, g_mix, w_in, cmp_pe_k, cmp_w1_k, cmp_b1_k, cmp_w2_k, cmp_pe_v, cmp_w1_v, cmp_b1_v, cmp_w2_v, gmlp_ln_g, gmlp_ln_b, gmlp_ws, gmlp_bs, w_nsa_out, w_gmlp_out, w_mix_out, g_xattn, g_mem, w_xq, w_xkv, w_xo, g_peer, w_peer_q, peer_sub_keys, peer_u, peer_v, g_final):
    B, S, D = x.shape
    T = B * S
    G, R, dk = NSA_GROUPS, NSA_REP, NSA_HEAD_DIM
    x2 = x.reshape(T, D)

    n_gate = NSA_HEADS * 3
    gate_cols = w_in[:, 1280:1280 + n_gate]
    per_g = n_gate // G
    gate_blocks = [jnp.pad(gate_cols[:, g * per_g:(g + 1) * per_g], ((0, 0), (0, LANES - per_g)))
                   for g in range(G)]
    w_re = jnp.concatenate([w_in[:, :1280]] + gate_blocks + [w_in[:, 1280 + n_gate:]],
                           axis=1).astype(BF16)
    half = CMP_STRIDE * dk
    pe = jnp.stack([cmp_pe_k.reshape(2, half), cmp_pe_v.reshape(2, half)])
    w1 = jnp.stack([cmp_w1_k, cmp_w1_v]).astype(BF16)
    b1 = jnp.stack([cmp_b1_k, cmp_b1_v])[:, None, :]
    w2 = jnp.stack([cmp_w2_k, cmp_w2_v]).astype(BF16)
    C = GMLP_CHUNK
    ws_causal = (gmlp_ws * jnp.tril(jnp.ones((C, C), F32))).astype(BF16)
    group_dim = gmlp_ln_g.shape[0] // GMLP_GROUPS
    bs_b = jnp.broadcast_to(gmlp_bs[:, :, None], (GMLP_GROUPS, C, group_dim))
    n_chunks = S // CMP_STRIDE
    overlap = _overlap_table(n_chunks, S // SLC_BLOCK)

    q, kvc, ks, vs, kw, vw, gates, u_raw, v_raw, mg = _in_proj(x2, g_mix, w_re, B, S)
    cmp_kv = _compress(kvc.reshape(2, B * G, n_chunks, half), pe, w1, b1, w2)
    cmp_kv = cmp_kv.reshape(2, B, G, n_chunks, dk)
    ksa, vsa = _augment_selected_kv(ks, vs)
    o_nsa = _nsa(q, ksa, vsa, kw, vw, cmp_kv[0], cmp_kv[1], gates, overlap.T.astype(BF16))
    h = _mix(x2, o_nsa.reshape(T, G * R * dk), u_raw, v_raw, mg, gmlp_ln_g, gmlp_ln_b,
             ws_causal, bs_b, w_nsa_out.astype(BF16), w_gmlp_out.astype(BF16),
             w_mix_out.astype(BF16))

    k_mem, v_mem = _mem_kv(mem, g_mem, w_xkv.astype(BF16))
    h = _xattn(h, g_xattn, w_xq.astype(BF16), k_mem, v_mem, w_xo.astype(BF16), S)

    half_dim = peer_sub_keys.shape[3]
    sk = peer_sub_keys.reshape(PEER_HEADS * 2, PEER_KEYS, half_dim).astype(BF16)
    hn, i_idx, j_idx, gate = _route(h, g_peer, w_peer_q.T.astype(BF16), sk)
    a_pre = _peer_up(hn, peer_u.astype(BF16), i_idx, j_idx)
    out = _peer_down(a_pre, gate, i_idx, j_idx, peer_v.astype(BF16), h, g_final)
    return out.reshape(B, S, D)
```

```python
import functools

import jax
import jax.numpy as jnp
from jax import lax
from jax.experimental import pallas as pl
from jax.experimental.pallas import tpu as pltpu

F32 = jnp.float32
BF16 = jnp.bfloat16
I32 = jnp.int32

NORM_EPS = 1e-6
NEG_INF = -1e30
TINY = 1e-30
LOG2_E = 1.4426950408889634
MASK_MARGIN = 300.0

NSA_HEADS = 8
NSA_HEAD_DIM = 64
NSA_GROUPS = 2
NSA_REP = NSA_HEADS // NSA_GROUPS
CMP_BLOCK = 32
CMP_STRIDE = 16
SLC_BLOCK = 64
SLC_TOPN = 16
WINDOW = 512
FORCE_BONUS = 1e4
GMLP_GROUPS = 4
GMLP_CHUNK = 128
XATTN_HEADS = 4
PEER_HEADS = 8
PEER_KEYS = 128
PEER_TOPK = 16

LANES = 128
SUBLANES = 8
VMEM_LIMIT_BYTES = 56 * 1024 * 1024

PROJ_ROWS = 512
MIX_ROWS = 256
NSA_QUERIES = 256
SEL_KEYS = 1024
ROUTE_TOKENS = 512
UP_TOKENS = 512
DOWN_TOKENS = 512
EXPERT_ROWS = 32
DOWN_EXPERT_ROWS = 8
W_ROW_STRIDE = PEER_KEYS + SUBLANES
EXPAND_UNROLL = 64


def _params(n_axes):
    return pltpu.CompilerParams(
        dimension_semantics=("arbitrary",) * n_axes,
        vmem_limit_bytes=VMEM_LIMIT_BYTES,
    )


def _rms(x, g):
    return x * lax.rsqrt(jnp.mean(x * x, axis=-1, keepdims=True) + NORM_EPS) * g


def _gelu(x):
    return 0.5 * x * (1.0 + lax.erf(x * 0.7071067811865476))


def _sigmoid(x):
    return 1.0 / (1.0 + jnp.exp(-x))


def _dot(a, b):
    return jnp.dot(a, b, preferred_element_type=F32)


def _dot_nt(a, b):
    return lax.dot_general(a, b, (((1,), (1,)), ((), ())), preferred_element_type=F32)


def _masked_softmax(s, mask):
    s = jnp.where(mask, s, NEG_INF)
    m = jnp.max(s, axis=-1, keepdims=True)
    p = jnp.where(mask, jnp.exp(s - m), 0.0)
    return p / jnp.maximum(jnp.sum(p, axis=-1, keepdims=True), TINY)


_C_Q = 0
_C_KV = 512
_C_GATE = 1280
_C_U = 1536
_C_V = 2048
_C_MERGE = 2560
_C_END = 4608


def _in_proj_kernel(x_ref, g_ref, w_ref, q_ref, kvc_ref, ks_ref, vs_ref, kw_ref, vw_ref,
                    gate_ref, u_ref, v_ref, mg_ref):
    hn = _rms(x_ref[...], g_ref[...]).astype(BF16)

    def proj(a, b):
        return _dot(hn, w_ref[:, a:b])

    dk = NSA_HEAD_DIM
    pq = proj(_C_Q, _C_KV) * (dk ** -0.5 * LOG2_E)
    for g in range(NSA_GROUPS):
        for r in range(NSA_REP):
            c = (g * NSA_REP + r) * dk
            q_ref[0, g, r] = pq[:, c:c + dk].astype(BF16)
    pk = proj(_C_KV, _C_GATE)
    for g in range(NSA_GROUPS):
        kvc_ref[0, 0, g] = pk[:, 0 * 128 + g * dk:0 * 128 + (g + 1) * dk]
        kvc_ref[1, 0, g] = pk[:, 1 * 128 + g * dk:1 * 128 + (g + 1) * dk]
        ks_ref[0, g] = pk[:, 2 * 128 + g * dk:2 * 128 + (g + 1) * dk].astype(BF16)
        vs_ref[0, g] = pk[:, 3 * 128 + g * dk:3 * 128 + (g + 1) * dk].astype(BF16)
        kw_ref[0, g] = pk[:, 4 * 128 + g * dk:4 * 128 + (g + 1) * dk].astype(BF16)
        vw_ref[0, g] = pk[:, 5 * 128 + g * dk:5 * 128 + (g + 1) * dk].astype(BF16)
        gate_ref[0, g] = _sigmoid(proj(_C_GATE + g * 128, _C_GATE + (g + 1) * 128))
    u_ref[...] = proj(_C_U, _C_V)
    v_ref[...] = proj(_C_V, _C_MERGE)
    mg_ref[...] = _sigmoid(proj(_C_MERGE, _C_END))


def _in_proj(x2, g_mix, w_re, B, S):
    T, D = x2.shape
    tm = PROJ_ROWS
    per_b = S // tm
    G, R, dk = NSA_GROUPS, NSA_REP, NSA_HEAD_DIM

    def tok(i):
        return (i, 0)

    def bgs(i):
        return (i // per_b, 0, i % per_b, 0)

    out_shape = (
        jax.ShapeDtypeStruct((B, G, R, S, dk), BF16),
        jax.ShapeDtypeStruct((2, B, G, S, dk), F32),
        jax.ShapeDtypeStruct((B, G, S, dk), BF16),
        jax.ShapeDtypeStruct((B, G, S, dk), BF16),
        jax.ShapeDtypeStruct((B, G, S, dk), BF16),
        jax.ShapeDtypeStruct((B, G, S, dk), BF16),
        jax.ShapeDtypeStruct((B, G, S, LANES), F32),
        jax.ShapeDtypeStruct((T, 512), F32),
        jax.ShapeDtypeStruct((T, 512), F32),
        jax.ShapeDtypeStruct((T, 2 * D), F32),
    )
    kv_spec = pl.BlockSpec((1, G, tm, dk), bgs)
    out_specs = (
        pl.BlockSpec((1, G, R, tm, dk), lambda i: (i // per_b, 0, 0, i % per_b, 0)),
        pl.BlockSpec((2, 1, G, tm, dk), lambda i: (0, i // per_b, 0, i % per_b, 0)),
        kv_spec, kv_spec, kv_spec, kv_spec,
        pl.BlockSpec((1, G, tm, LANES), bgs),
        pl.BlockSpec((tm, 512), tok),
        pl.BlockSpec((tm, 512), tok),
        pl.BlockSpec((tm, 2 * D), tok),
    )
    return pl.pallas_call(
        _in_proj_kernel,
        grid=(T // tm,),
        in_specs=[
            pl.BlockSpec((tm, D), tok),
            pl.BlockSpec((1, D), lambda i: (0, 0)),
            pl.BlockSpec((D, _C_END), lambda i: (0, 0)),
        ],
        out_specs=out_specs,
        out_shape=out_shape,
        compiler_params=_params(1),
        name="in_proj",
    )(x2, g_mix.reshape(1, D), w_re)


def _compress_kernel(x_ref, pe_ref, w1_ref, b1_ref, w2_ref, o_ref):
    x = x_ref[0, 0]
    half = x.shape[1]
    lo = (x + pe_ref[0, 0:1, :]).astype(BF16)
    hi = (x + pe_ref[0, 1:2, :]).astype(BF16)
    p = _dot(lo, w1_ref[0, :half, :])
    q = _dot(hi, w1_ref[0, half:, :])
    n = x.shape[0]
    h = p + pltpu.roll(q, n - 1, 0) + b1_ref[0]
    o_ref[0, 0] = _dot(_gelu(h).astype(BF16), w2_ref[0]).astype(BF16)


def _compress(kvc, pe, w1, b1, w2):
    _, BG, n_chunks, width = kvc.shape
    hidden = w1.shape[-1]
    dk = w2.shape[-1]
    return pl.pallas_call(
        _compress_kernel,
        grid=(2, BG),
        in_specs=[
            pl.BlockSpec((1, 1, n_chunks, width), lambda a, b: (a, b, 0, 0)),
            pl.BlockSpec((1, 2, width), lambda a, b: (a, 0, 0)),
            pl.BlockSpec((1, 2 * width, hidden), lambda a, b: (a, 0, 0)),
            pl.BlockSpec((1, 1, hidden), lambda a, b: (a, 0, 0)),
            pl.BlockSpec((1, hidden, dk), lambda a, b: (a, 0, 0)),
        ],
        out_specs=pl.BlockSpec((1, 1, n_chunks, dk), lambda a, b: (a, b, 0, 0)),
        out_shape=jax.ShapeDtypeStruct((2, BG, n_chunks, dk), BF16),
        compiler_params=_params(2),
        name="compress",
    )(kvc, pe, w1, b1, w2)


def _select_blocks(score_t):
    n = score_t.shape[0]
    row = lax.broadcasted_iota(I32, score_t.shape, 0).astype(F32)
    s = score_t
    for _ in range(SLC_TOPN):
        m = jnp.max(s, axis=0, keepdims=True)
        first = jnp.min(jnp.where(s == m, row, float(n)), axis=0, keepdims=True)
        s = jnp.where(row == first, -3.0e38, s)
    return jnp.where((s < -2.0e38) & (score_t > 0.5 * NEG_INF), 1.0, 0.0)


def _nsa_kernel(q_ref, ksa_ref, vsa_ref, kw_ref, vw_ref, kc_ref, vc_ref, gate_ref, ovt_ref, o_ref,
                kmax_scr):
    qb = pl.program_id(2)
    q0 = qb * NSA_QUERIES
    R, Q, dk = NSA_REP, NSA_QUERIES, NSA_HEAD_DIM
    q2 = q_ref[0, 0].reshape(R * Q, dk)

    kc = kc_ref[0, 0]
    n_cmp = kc.shape[0]
    s_c = _dot_nt(q2, kc)
    t_c = q0 + lax.broadcasted_iota(I32, (Q, n_cmp), 0)
    cmp_end = lax.broadcasted_iota(I32, (Q, n_cmp), 1) * CMP_STRIDE + (CMP_BLOCK - 1)
    mask_c = cmp_end <= t_c
    o_c = []
    p_sum = jnp.zeros((Q, n_cmp), F32)
    for r in range(R):
        s_r = jnp.where(mask_c, s_c[r * Q:(r + 1) * Q], NEG_INF)
        e = jnp.where(mask_c, jnp.exp2(s_r - jnp.max(s_r, axis=-1, keepdims=True)), 0.0)
        inv = 1.0 / jnp.maximum(jnp.sum(e, axis=-1, keepdims=True), TINY)
        p_sum = p_sum + e * inv
        o_c.append(_dot(e.astype(BF16), vc_ref[0, 0]) * inv)

    span = WINDOW + Q
    start = pl.multiple_of(jnp.maximum(q0 - WINDOW, 0), Q)
    kw = kw_ref[0, 0, pl.ds(start, span), :]
    vw = vw_ref[0, 0, pl.ds(start, span), :]
    s_w = _dot_nt(q2, kw)
    pos = start + lax.broadcasted_iota(I32, (Q, span), 1)
    diff = q0 + lax.broadcasted_iota(I32, (Q, span), 0) - pos
    mask_w = (diff >= 0) & (diff < WINDOW)
    o_w = []
    for r in range(R):
        s_r = jnp.where(mask_w, s_w[r * Q:(r + 1) * Q], NEG_INF)
        e = jnp.exp2(s_r - jnp.max(s_r, axis=-1, keepdims=True))
        inv = 1.0 / jnp.maximum(jnp.sum(e, axis=-1, keepdims=True), TINY)
        o_w.append(_dot(e.astype(BF16), vw) * inv)

    n_slc = ovt_ref.shape[0]
    ovt = ovt_ref[...]
    p_hi = p_sum.astype(BF16)
    rest = p_sum - p_hi.astype(F32)
    p_mid = rest.astype(BF16)
    p_lo = (rest - p_mid.astype(F32)).astype(BF16)
    imp_t = _dot_nt(ovt, p_hi) + _dot_nt(ovt, p_mid) + _dot_nt(ovt, p_lo)
    t_b = q0 + lax.broadcasted_iota(I32, (n_slc, Q), 1)
    blk = lax.broadcasted_iota(I32, (n_slc, Q), 0)
    allowed = blk * SLC_BLOCK <= t_b
    cur = t_b // SLC_BLOCK
    forced = (blk == 0) | (blk == cur) | (blk == cur - 1)
    score_t = jnp.where(forced & allowed, FORCE_BONUS, jnp.where(allowed, imp_t, NEG_INF))
    not_sel = 1.0 - _select_blocks(score_t).T

    tk = SEL_KEYS
    blk_w = ksa_ref.shape[3] - 2 * dk

    @pl.when(qb == 0)
    def _():
        k_all = ksa_ref[0, 0, :, blk_w:blk_w + dk].astype(F32)
        k_sq = jnp.max(jnp.sum(k_all * k_all, axis=-1, keepdims=True), axis=0, keepdims=True)
        kmax_scr[...] = jnp.broadcast_to(jnp.sqrt(k_sq), kmax_scr.shape)

    q_f = q2.astype(F32)
    q_sq = jnp.max(jnp.sum(q_f * q_f, axis=-1, keepdims=True), axis=0, keepdims=True)
    drop = (2.0 * jnp.sqrt(q_sq) * kmax_scr[0:1, 0:1] + MASK_MARGIN) * 1.02
    pieces = [not_sel] if blk_w == n_slc else [not_sel, jnp.zeros((Q, blk_w - n_slc), F32)]
    mask_cols = (jnp.concatenate(pieces, axis=1) * drop).astype(BF16)
    q_aug = jnp.concatenate(
        [jnp.concatenate([mask_cols] * R, axis=0), q2, jnp.zeros((R * Q, dk), BF16)], axis=1)
    last = q0 // tk
    va_w = vsa_ref.shape[3]

    def sel_tile(kt, carry, causal):
        ms, accs = carry
        base = pl.multiple_of(kt * tk, tk)
        s = _dot_nt(q_aug, ksa_ref[0, 0, pl.ds(base, tk), :])
        v_t = vsa_ref[0, 0, pl.ds(base, tk), :]
        if causal:
            ahead = (base + lax.broadcasted_iota(I32, (Q, tk), 1)
                     > q0 + lax.broadcasted_iota(I32, (Q, tk), 0))
        new_m, new_acc = [], []
        for r in range(R):
            s_r = s[r * Q:(r + 1) * Q]
            if causal:
                s_r = jnp.where(ahead, NEG_INF, s_r)
            m_new = jnp.maximum(ms[r], jnp.max(s_r, axis=-1, keepdims=True))
            alpha = jnp.exp2(ms[r] - m_new)
            p = jnp.exp2(s_r - m_new)
            new_m.append(m_new)
            new_acc.append(alpha * accs[r] + _dot(p.astype(BF16), v_t))
        return tuple(new_m), tuple(new_acc)

    init = (tuple(jnp.full((Q, 1), NEG_INF, F32) for _ in range(R)),
            tuple(jnp.zeros((Q, va_w), F32) for _ in range(R)))
    carry = lax.fori_loop(0, last, functools.partial(sel_tile, causal=False), init)
    _, accs = sel_tile(last, carry, causal=True)
    o_s = [accs[r][:, :dk] / jnp.maximum(accs[r][:, dk:dk + 1], TINY) for r in range(R)]

    gate = gate_ref[0, 0]
    for r in range(R):
        out = (gate[:, 3 * r:3 * r + 1] * o_c[r] + gate[:, 3 * r + 1:3 * r + 2] * o_s[r]
               + gate[:, 3 * r + 2:3 * r + 3] * o_w[r])
        o_ref[0, :, r * dk:(r + 1) * dk] = out.astype(BF16)


def _augment_selected_kv(ks, vs):
    B, G, S, dk = ks.shape
    n_slc = S // SLC_BLOCK
    blk_w = -(-n_slc // LANES) * LANES
    own = (jnp.arange(S)[:, None] // SLC_BLOCK) == jnp.arange(blk_w)[None, :]
    tab = jnp.where(own, -1.0, 0.0).astype(BF16)
    ksa = jnp.concatenate([jnp.broadcast_to(tab, (B, G, S, blk_w)), ks,
                           jnp.zeros((B, G, S, dk), BF16)], axis=-1)
    vsa = jnp.concatenate([vs, jnp.ones((B, G, S, 1), BF16),
                           jnp.zeros((B, G, S, LANES - dk - 1), BF16)], axis=-1)
    return ksa, vsa


def _nsa(q, ksa, vsa, kw, vw, kcmp, vcmp, gates, overlap_t):
    B, G, R, S, dk = q.shape
    n_cmp = kcmp.shape[2]
    n_slc = overlap_t.shape[0]
    kv_spec = pl.BlockSpec((1, 1, S, dk), lambda b, g, i: (b, g, 0, 0))
    cmp_spec = pl.BlockSpec((1, 1, n_cmp, dk), lambda b, g, i: (b, g, 0, 0))
    return pl.pallas_call(
        _nsa_kernel,
        grid=(B, G, S // NSA_QUERIES),
        in_specs=[
            pl.BlockSpec((1, 1, R, NSA_QUERIES, dk), lambda b, g, i: (b, g, 0, i, 0)),
            pl.BlockSpec((1, 1, S, ksa.shape[3]), lambda b, g, i: (b, g, 0, 0)),
            pl.BlockSpec((1, 1, S, vsa.shape[3]), lambda b, g, i: (b, g, 0, 0)),
            kv_spec, kv_spec,
            cmp_spec, cmp_spec,
            pl.BlockSpec((1, 1, NSA_QUERIES, LANES), lambda b, g, i: (b, g, i, 0)),
            pl.BlockSpec((n_slc, n_cmp), lambda b, g, i: (0, 0)),
        ],
        out_specs=pl.BlockSpec((1, NSA_QUERIES, R * dk), lambda b, g, i: (b, i, g)),
        out_shape=jax.ShapeDtypeStruct((B, S, G * R * dk), BF16),
        scratch_shapes=[pltpu.VMEM((SUBLANES, LANES), F32)],
        compiler_params=_params(3),
        name="nsa",
    )(q, ksa, vsa, kw, vw, kcmp, vcmp, gates, overlap_t)


def _mix_kernel(x_ref, onsa_ref, u_ref, v_ref, mg_ref, lng_ref, lnb_ref, ws_ref, bs_ref,
                wn_ref, wg_ref, wm_ref, h_ref):
    D = x_ref.shape[1]
    u = _gelu(u_ref[...])
    v = _gelu(v_ref[...])
    mu = jnp.mean(v, axis=-1, keepdims=True)
    var = jnp.mean(jnp.square(v - mu), axis=-1, keepdims=True)
    vn = ((v - mu) * lax.rsqrt(var + NORM_EPS) * lng_ref[...] + lnb_ref[...]).astype(BF16)
    C = GMLP_CHUNK
    gd = vn.shape[1] // GMLP_GROUPS
    rows = []
    for c in range(vn.shape[0] // C):
        cols = []
        for g in range(GMLP_GROUPS):
            cols.append(_dot(ws_ref[g], vn[c * C:(c + 1) * C, g * gd:(g + 1) * gd]) + bs_ref[g])
        rows.append(jnp.concatenate(cols, axis=1))
    o_gmlp = u * jnp.concatenate(rows, axis=0)
    mg = mg_ref[...]
    y = (mg[:, :D] * _dot(onsa_ref[...], wn_ref[...])
         + mg[:, D:] * _dot(o_gmlp.astype(BF16), wg_ref[...]))
    h_ref[...] = x_ref[...] + _dot(y.astype(BF16), wm_ref[...])


def _mix(x2, o_nsa, u_raw, v_raw, mg, ln_g, ln_b, ws_causal, bs_b, w_nsa_out, w_gmlp_out, w_mix_out):
    T, D = x2.shape
    tm = MIX_ROWS
    W = u_raw.shape[1]

    def tok(i):
        return (i, 0)

    def const2(i):
        return (0, 0)

    def const3(i):
        return (0, 0, 0)

    return pl.pallas_call(
        _mix_kernel,
        grid=(T // tm,),
        in_specs=[
            pl.BlockSpec((tm, D), tok),
            pl.BlockSpec((tm, o_nsa.shape[1]), tok),
            pl.BlockSpec((tm, W), tok),
            pl.BlockSpec((tm, W), tok),
            pl.BlockSpec((tm, 2 * D), tok),
            pl.BlockSpec((1, W), const2),
            pl.BlockSpec((1, W), const2),
            pl.BlockSpec(ws_causal.shape, const3),
            pl.BlockSpec(bs_b.shape, const3),
            pl.BlockSpec(w_nsa_out.shape, const2),
            pl.BlockSpec(w_gmlp_out.shape, const2),
            pl.BlockSpec(w_mix_out.shape, const2),
        ],
        out_specs=pl.BlockSpec((tm, D), tok),
        out_shape=jax.ShapeDtypeStruct((T, D), F32),
        compiler_params=_params(1),
        name="mix",
    )(x2, o_nsa, u_raw, v_raw, mg, ln_g.reshape(1, W), ln_b.reshape(1, W), ws_causal, bs_b,
      w_nsa_out, w_gmlp_out, w_mix_out)


def _mem_kv_kernel(mem_ref, g_ref, w_ref, k_ref, v_ref):
    D = mem_ref.shape[2]
    mn = _rms(mem_ref[0], g_ref[...]).astype(BF16)
    kv = _dot(mn, w_ref[...])
    k_ref[0] = kv[:, :D].astype(BF16)
    v_ref[0] = kv[:, D:].astype(BF16)


def _mem_kv(mem, g_mem, w_xkv):
    B, M, D = mem.shape
    spec = pl.BlockSpec((1, M, D), lambda b: (b, 0, 0))
    return pl.pallas_call(
        _mem_kv_kernel,
        grid=(B,),
        in_specs=[spec, pl.BlockSpec((1, D), lambda b: (0, 0)),
                  pl.BlockSpec((D, 2 * D), lambda b: (0, 0))],
        out_specs=(spec, spec),
        out_shape=(jax.ShapeDtypeStruct((B, M, D), BF16), jax.ShapeDtypeStruct((B, M, D), BF16)),
        compiler_params=_params(1),
        name="mem_kv",
    )(mem, g_mem.reshape(1, D), w_xkv)


def _xattn_kernel(h_ref, g_ref, wq_ref, k_ref, v_ref, wo_ref, o_ref):
    h = h_ref[...]
    D = h.shape[1]
    hd = D // XATTN_HEADS
    hn = _rms(h, g_ref[...]).astype(BF16)
    q = _dot(hn, wq_ref[...]) * (hd ** -0.5)
    outs = []
    for a in range(XATTN_HEADS):
        s = _dot_nt(q[:, a * hd:(a + 1) * hd].astype(BF16), k_ref[0, :, a * hd:(a + 1) * hd])
        m = jnp.max(s, axis=-1, keepdims=True)
        e = jnp.exp(s - m)
        p = e / jnp.sum(e, axis=-1, keepdims=True)
        outs.append(_dot(p.astype(BF16), v_ref[0, :, a * hd:(a + 1) * hd]))
    o = jnp.concatenate(outs, axis=1).astype(BF16)
    o_ref[...] = h + _dot(o, wo_ref[...])


def _xattn(h, g_xattn, w_xq, k_mem, v_mem, w_xo, S):
    T, D = h.shape
    tm = MIX_ROWS
    per_b = S // tm
    M = k_mem.shape[1]
    mem_spec = pl.BlockSpec((1, M, D), lambda i: (i // per_b, 0, 0))
    return pl.pallas_call(
        _xattn_kernel,
        grid=(T // tm,),
        in_specs=[
            pl.BlockSpec((tm, D), lambda i: (i, 0)),
            pl.BlockSpec((1, D), lambda i: (0, 0)),
            pl.BlockSpec((D, D), lambda i: (0, 0)),
            mem_spec, mem_spec,
            pl.BlockSpec((D, D), lambda i: (0, 0)),
        ],
        out_specs=pl.BlockSpec((tm, D), lambda i: (i, 0)),
        out_shape=jax.ShapeDtypeStruct((T, D), F32),
        compiler_params=_params(1),
        name="xattn",
    )(h, g_xattn.reshape(1, D), w_xq, k_mem, v_mem, w_xo)


def _top_rows(vals, k, row=None):
    if row is None:
        row = lax.broadcasted_iota(I32, vals.shape, 0).astype(F32)
    top_v, top_i = [], []
    for _ in range(k):
        m = jnp.max(vals, axis=0, keepdims=True)
        first = jnp.min(jnp.where(vals == m, row, 1.0e9), axis=0, keepdims=True)
        top_v.append(m)
        top_i.append(first)
        vals = jnp.where(row == first, -3.0e38, vals)
    return top_v, top_i


def _pair_candidates(s0_all, s1_all, K):
    L, n_tok = s0_all.shape
    sub = lax.broadcasted_iota(I32, (SUBLANES, n_tok), 0).astype(F32)
    vals, ids = [], []
    a = 0
    while a < L and min(K // (a + 1), L) > 1:
        nb = min(K // (a + 1), L)
        for b0 in range(0, nb, SUBLANES):
            v = s0_all[a:a + 1] + s1_all[b0:b0 + SUBLANES]
            vals.append(v if b0 + SUBLANES <= nb else jnp.where(sub < float(nb - b0), v, -3.0e38))
            ids.append(sub + float(a * K + b0))
        a += 1
    while a < L:
        vals.append(s0_all[a:a + SUBLANES] + s1_all[0:1])
        ids.append((sub + float(a)) * float(K))
        a += SUBLANES
    return jnp.concatenate(vals, axis=0), jnp.concatenate(ids, axis=0)


def _route_from_scores(sc0, sc1):
    K = PEER_TOPK
    n_tok = sc0.shape[1]
    s0, i0 = _top_rows(sc0, K)
    s1, i1 = _top_rows(sc1, K)
    s0_all = jnp.concatenate(s0, axis=0)
    s1_all = jnp.concatenate(s1, axis=0)
    i0_all = jnp.concatenate(i0, axis=0)
    i1_all = jnp.concatenate(i1, axis=0)
    cand, cand_id = _pair_candidates(s0_all, s1_all, K)
    best_s, pos = _top_rows(cand, K, cand_id)
    best = jnp.concatenate(best_s, axis=0)
    m = jnp.max(best, axis=0, keepdims=True)
    e = jnp.exp(best - m)
    gate = e / jnp.sum(e, axis=0, keepdims=True)
    rank = lax.broadcasted_iota(I32, (K, n_tok), 0).astype(F32)
    ei, ej = [], []
    for kk in range(K):
        a = jnp.floor(pos[kk] * (1.0 / K))
        b = pos[kk] - a * K
        ei.append(jnp.sum(jnp.where(rank == a, i0_all, 0.0), axis=0, keepdims=True))
        ej.append(jnp.sum(jnp.where(rank == b, i1_all, 0.0), axis=0, keepdims=True))
    return jnp.concatenate(ei, axis=0), jnp.concatenate(ej, axis=0), gate


def _route_head(qt_scr, sk_ref, hd):
    half_dim = sk_ref.shape[2]
    scores = []
    for p in range(2):
        off = pl.multiple_of((hd * 2 + p) * half_dim, half_dim)
        qs = qt_scr[pl.ds(off, half_dim), :].astype(BF16)
        scores.append(_dot(sk_ref[hd * 2 + p], qs))
    return scores


def _route_kernel(h_ref, g_ref, wq_ref, sk_ref, hn_ref, i_ref, j_ref, gate_ref,
                  qt_scr, i_scr, j_scr, gate_scr):
    K = PEER_TOPK
    hn = _rms(h_ref[...], g_ref[...]).astype(BF16)
    hn_ref[...] = hn
    qt_scr[...] = _dot_nt(wq_ref[...], hn)

    def head(hd, _):
        sc0, sc1 = _route_head(qt_scr, sk_ref, hd)
        row0 = pl.multiple_of(hd * K, K)
        ei, ej, gate = _route_from_scores(sc0, sc1)
        i_scr[pl.ds(row0, K), :] = ei
        j_scr[pl.ds(row0, K), :] = ej
        gate_scr[pl.ds(row0, K), :] = gate
        return 0

    lax.fori_loop(0, PEER_HEADS, head, 0)
    i_ref[...] = i_scr[...].T.astype(I32)
    j_ref[...] = j_scr[...].T.astype(I32)
    gate_ref[...] = gate_scr[...].T


def _route(h, g_peer, wq_t, sub_keys):
    T, D = h.shape
    tt = ROUTE_TOKENS
    n_sel = PEER_HEADS * PEER_TOPK
    tok = lambda i: (i, 0)
    return pl.pallas_call(
        _route_kernel,
        grid=(T // tt,),
        in_specs=[
            pl.BlockSpec((tt, D), tok),
            pl.BlockSpec((1, D), lambda i: (0, 0)),
            pl.BlockSpec(wq_t.shape, lambda i: (0, 0)),
            pl.BlockSpec(sub_keys.shape, lambda i: (0, 0, 0)),
        ],
        out_specs=(
            pl.BlockSpec((tt, D), tok),
            pl.BlockSpec((tt, n_sel), tok),
            pl.BlockSpec((tt, n_sel), tok),
            pl.BlockSpec((tt, n_sel), tok),
        ),
        out_shape=(
            jax.ShapeDtypeStruct((T, D), BF16),
            jax.ShapeDtypeStruct((T, n_sel), I32),
            jax.ShapeDtypeStruct((T, n_sel), I32),
            jax.ShapeDtypeStruct((T, n_sel), F32),
        ),
        scratch_shapes=[
            pltpu.VMEM((wq_t.shape[0], tt), F32),
            pltpu.VMEM((n_sel, tt), F32),
            pltpu.VMEM((n_sel, tt), F32),
            pltpu.VMEM((n_sel, tt), F32),
        ],
        compiler_params=_params(1),
        name="route",
    )(h, g_peer.reshape(1, D), wq_t, sub_keys)


def _peer_up_kernel(hn_ref, u_ref, i_ref, j_ref, a_ref):
    e = pl.program_id(1)

    @pl.when(e == 0)
    def _():
        a_ref[...] = jnp.zeros(a_ref.shape, F32)

    hn = hn_ref[...]
    ii = i_ref[...]
    jj = j_ref[...]
    acc = a_ref[...]
    for r0 in range(0, EXPERT_ROWS, 2):
        z = _dot_nt(hn, u_ref[r0 * PEER_KEYS:(r0 + 2) * PEER_KEYS, :])
        for r in range(r0, r0 + 2):
            picked = jnp.take_along_axis(
                z[:, (r - r0) * PEER_KEYS:(r - r0 + 1) * PEER_KEYS], jj, axis=1)
            acc = jnp.where(ii == e * EXPERT_ROWS + r, picked, acc)
    a_ref[...] = acc


def _peer_up(hn, u_bf, i_idx, j_idx):
    T, D = hn.shape
    tt = UP_TOKENS
    n_sel = i_idx.shape[1]
    rows = EXPERT_ROWS * PEER_KEYS
    tok = lambda t, e: (t, 0)
    return pl.pallas_call(
        _peer_up_kernel,
        grid=(T // tt, u_bf.shape[0] // rows),
        in_specs=[
            pl.BlockSpec((tt, D), tok),
            pl.BlockSpec((rows, D), lambda t, e: (e, 0)),
            pl.BlockSpec((tt, n_sel), tok),
            pl.BlockSpec((tt, n_sel), tok),
        ],
        out_specs=pl.BlockSpec((tt, n_sel), tok),
        out_shape=jax.ShapeDtypeStruct((T, n_sel), F32),
        compiler_params=_params(2),
        name="peer_up",
    )(hn, u_bf, i_idx, j_idx)


def _peer_down_kernel(a_ref, gate_ref, i_ref, j_ref, v_ref, h_ref, g_ref, o_ref,
                      w_scr, coef_scr, acc_scr):
    e = pl.program_id(1)
    n_tok = a_ref.shape[0]
    nk = PEER_KEYS

    @pl.when(e == 0)
    def _():
        coef_scr[...] = gate_ref[...] * _gelu(a_ref[...])
        acc_scr[...] = jnp.zeros(acc_scr.shape, F32)
        sub = lax.broadcasted_iota(I32, (nk, i_ref.shape[1]), 0)

        def expand(t, _):
            ii = i_ref[pl.ds(t, 1), :]
            jj = j_ref[pl.ds(t, 1), :]
            cc = coef_scr[pl.ds(t, 1), :]
            a = jnp.where(sub == ii, cc, 0.0).astype(BF16)
            bt = jnp.where(sub == jj, 1.0, 0.0).astype(BF16)
            base = pl.multiple_of(t * W_ROW_STRIDE, SUBLANES)
            w_scr[pl.ds(base, nk), :] = _dot_nt(a, bt)
            return 0

        lax.fori_loop(0, n_tok, expand, 0, unroll=EXPAND_UNROLL)

    parts = []
    for r in range(DOWN_EXPERT_ROWS):
        row = e * DOWN_EXPERT_ROWS + r
        parts.append(w_scr[pl.ds(row, n_tok, stride=W_ROW_STRIDE), :].astype(BF16))
    lhs = jnp.concatenate(parts, axis=1)
    acc_scr[...] += _dot(lhs, v_ref[...])

    @pl.when(e == pl.num_programs(1) - 1)
    def _():
        o_ref[...] = _rms(h_ref[...] + acc_scr[...], g_ref[...])


def _peer_down(a_pre, gate, i_idx, j_idx, v_bf, h, g_final):
    T, D = h.shape
    tt = DOWN_TOKENS
    n_sel = i_idx.shape[1]
    rows = DOWN_EXPERT_ROWS * PEER_KEYS
    tok = lambda t, e: (t, 0)
    sel_spec = pl.BlockSpec((tt, n_sel), tok)
    return pl.pallas_call(
        _peer_down_kernel,
        grid=(T // tt, v_bf.shape[0] // rows),
        in_specs=[
            sel_spec, sel_spec, sel_spec, sel_spec,
            pl.BlockSpec((rows, D), lambda t, e: (e, 0)),
            pl.BlockSpec((tt, D), tok),
            pl.BlockSpec((1, D), lambda t, e: (0, 0)),
        ],
        out_specs=pl.BlockSpec((tt, D), tok),
        out_shape=jax.ShapeDtypeStruct((T, D), F32),
        scratch_shapes=[
            pltpu.VMEM((tt * W_ROW_STRIDE, PEER_KEYS), F32),
            pltpu.VMEM((tt, n_sel), F32),
            pltpu.VMEM((tt, D), F32),
        ],
        compiler_params=_params(2),
        name="peer_down",
    )(a_pre, gate, i_idx, j_idx, v_bf, h, g_final.reshape(1, D))


def _overlap_table(n_cmp, n_slc):
    c0 = jnp.arange(n_cmp) * CMP_STRIDE
    j0 = jnp.arange(n_slc) * SLC_BLOCK
    ov = jnp.clip(jnp.minimum(c0[:, None] + CMP_BLOCK, j0[None, :] + SLC_BLOCK)
                  - jnp.maximum(c0[:, None], j0[None, :]), 0, None)
    return ov.astype(F32) / CMP_BLOCK


def kernel(x, mem, g_mix, w_in, cmp_pe_k, cmp_w1_k, cmp_b1_k, cmp_w2_k, cmp_pe_v, cmp_w1_v, cmp_b1_v, cmp_w2_v, gmlp_ln_g, gmlp_ln_b, gmlp_ws, gmlp_bs, w_nsa_out, w_gmlp_out, w_mix_out, g_xattn, g_mem, w_xq, w_xkv, w_xo, g_peer, w_peer_q, peer_sub_keys, peer_u, peer_v, g_final):
    B, S, D = x.shape
    T = B * S
    G, R, dk = NSA_GROUPS, NSA_REP, NSA_HEAD_DIM
    x2 = x.reshape(T, D)

    n_gate = NSA_HEADS * 3
    gate_cols = w_in[:, 1280:1280 + n_gate]
    per_g = n_gate // G
    gate_blocks = [jnp.pad(gate_cols[:, g * per_g:(g + 1) * per_g], ((0, 0), (0, LANES - per_g)))
                   for g in range(G)]
    w_re = jnp.concatenate([w_in[:, :1280]] + gate_blocks + [w_in[:, 1280 + n_gate:]],
                           axis=1).astype(BF16)
    half = CMP_STRIDE * dk
    pe = jnp.stack([cmp_pe_k.reshape(2, half), cmp_pe_v.reshape(2, half)])
    w1 = jnp.stack([cmp_w1_k, cmp_w1_v]).astype(BF16)
    b1 = jnp.stack([cmp_b1_k, cmp_b1_v])[:, None, :]
    w2 = jnp.stack([cmp_w2_k, cmp_w2_v]).astype(BF16)
    C = GMLP_CHUNK
    ws_causal = (gmlp_ws * jnp.tril(jnp.ones((C, C), F32))).astype(BF16)
    group_dim = gmlp_ln_g.shape[0] // GMLP_GROUPS
    bs_b = jnp.broadcast_to(gmlp_bs[:, :, None], (GMLP_GROUPS, C, group_dim))
    n_chunks = S // CMP_STRIDE
    overlap = _overlap_table(n_chunks, S // SLC_BLOCK)

    q, kvc, ks, vs, kw, vw, gates, u_raw, v_raw, mg = _in_proj(x2, g_mix, w_re, B, S)
    cmp_kv = _compress(kvc.reshape(2, B * G, n_chunks, half), pe, w1, b1, w2)
    cmp_kv = cmp_kv.reshape(2, B, G, n_chunks, dk)
    ksa, vsa = _augment_selected_kv(ks, vs)
    o_nsa = _nsa(q, ksa, vsa, kw, vw, cmp_kv[0], cmp_kv[1], gates, overlap.T.astype(BF16))
    h = _mix(x2, o_nsa.reshape(T, G * R * dk), u_raw, v_raw, mg, gmlp_ln_g, gmlp_ln_b,
             ws_causal, bs_b, w_nsa_out.astype(BF16), w_gmlp_out.astype(BF16),
             w_mix_out.astype(BF16))

    k_mem, v_mem = _mem_kv(mem, g_mem, w_xkv.astype(BF16))
    h = _xattn(h, g_xattn, w_xq.astype(BF16), k_mem, v_mem, w_xo.astype(BF16), S)

    half_dim = peer_sub_keys.shape[3]
    sk = peer_sub_keys.reshape(PEER_HEADS * 2, PEER_KEYS, half_dim).astype(BF16)
    hn, i_idx, j_idx, gate = _route(h, g_peer, w_peer_q.T.astype(BF16), sk)
    a_pre = _peer_up(hn, peer_u.astype(BF16), i_idx, j_idx)
    out = _peer_down(a_pre, gate, i_idx, j_idx, peer_v.astype(BF16), h, g_final)
    return out.reshape(B, S, D)
```

```python
import functools

import jax
import jax.numpy as jnp
from jax import lax
from jax.experimental import pallas as pl
from jax.experimental.pallas import tpu as pltpu

F32 = jnp.float32
BF16 = jnp.bfloat16
I32 = jnp.int32

NORM_EPS = 1e-6
NEG_INF = -1e30
TINY = 1e-30
LOG2_E = 1.4426950408889634
MASK_MARGIN = 300.0

NSA_HEADS = 8
NSA_HEAD_DIM = 64
NSA_GROUPS = 2
NSA_REP = NSA_HEADS // NSA_GROUPS
CMP_BLOCK = 32
CMP_STRIDE = 16
SLC_BLOCK = 64
SLC_TOPN = 16
WINDOW = 512
FORCE_BONUS = 1e4
GMLP_GROUPS = 4
GMLP_CHUNK = 128
XATTN_HEADS = 4
PEER_HEADS = 8
PEER_KEYS = 128
PEER_TOPK = 16

LANES = 128
SUBLANES = 8
VMEM_LIMIT_BYTES = 56 * 1024 * 1024

PROJ_ROWS = 512
MIX_ROWS = 512
NSA_QUERIES = 256
SEL_KEYS = 1024
ROUTE_TOKENS = 512
UP_TOKENS = 1024
DOWN_TOKENS = 512
EXPERT_ROWS = 32
DOWN_EXPERT_ROWS = 8
W_ROW_STRIDE = PEER_KEYS + SUBLANES
EXPAND_UNROLL = 64


def _params(n_axes):
    return pltpu.CompilerParams(
        dimension_semantics=("arbitrary",) * n_axes,
        vmem_limit_bytes=VMEM_LIMIT_BYTES,
    )


def _rms(x, g):
    return x * lax.rsqrt(jnp.mean(x * x, axis=-1, keepdims=True) + NORM_EPS) * g


def _gelu(x):
    return 0.5 * x * (1.0 + lax.erf(x * 0.7071067811865476))


def _sigmoid(x):
    return 1.0 / (1.0 + jnp.exp(-x))


def _dot(a, b):
    return jnp.dot(a, b, preferred_element_type=F32)


def _dot_nt(a, b):
    return lax.dot_general(a, b, (((1,), (1,)), ((), ())), preferred_element_type=F32)


def _masked_softmax(s, mask):
    s = jnp.where(mask, s, NEG_INF)
    m = jnp.max(s, axis=-1, keepdims=True)
    p = jnp.where(mask, jnp.exp(s - m), 0.0)
    return p / jnp.maximum(jnp.sum(p, axis=-1, keepdims=True), TINY)


_C_Q = 0
_C_KV = 512
_C_GATE = 1280
_C_U = 1536
_C_V = 2048
_C_MERGE = 2560
_C_END = 4608


def _in_proj_kernel(x_ref, g_ref, w_ref, q_ref, kvc_ref, ks_ref, vs_ref, kw_ref, vw_ref,
                    gate_ref, u_ref, v_ref, mg_ref, *, seq_len):
    hn = _rms(x_ref[...], g_ref[...]).astype(BF16)

    def proj(a, b):
        return _dot(hn, w_ref[:, a:b])

    dk = NSA_HEAD_DIM
    pq = proj(_C_Q, _C_KV) * (dk ** -0.5 * LOG2_E)
    for g in range(NSA_GROUPS):
        for r in range(NSA_REP):
            c = (g * NSA_REP + r) * dk
            q_ref[0, g, r] = pq[:, c:c + dk].astype(BF16)
    pk = proj(_C_KV, _C_GATE)
    tm = x_ref.shape[0]
    blk_w = ks_ref.shape[3] - 2 * dk
    pos = (pl.program_id(0) % (seq_len // tm)) * tm + lax.broadcasted_iota(I32, (tm, blk_w), 0)
    own_block = jnp.where(pos // SLC_BLOCK == lax.broadcasted_iota(I32, (tm, blk_w), 1),
                          -1.0, 0.0).astype(BF16)
    ones_col = jnp.where(lax.broadcasted_iota(I32, (tm, LANES - dk), 1) == 0, 1.0, 0.0).astype(BF16)
    for g in range(NSA_GROUPS):
        kvc_ref[0, 0, g] = pk[:, 0 * 128 + g * dk:0 * 128 + (g + 1) * dk]
        kvc_ref[1, 0, g] = pk[:, 1 * 128 + g * dk:1 * 128 + (g + 1) * dk]
        ks_ref[0, g, :, :blk_w] = own_block
        ks_ref[0, g, :, blk_w:blk_w + dk] = (
            pk[:, 2 * 128 + g * dk:2 * 128 + (g + 1) * dk].astype(BF16))
        ks_ref[0, g, :, blk_w + dk:] = jnp.zeros((tm, dk), BF16)
        vs_ref[0, g, :, :dk] = pk[:, 3 * 128 + g * dk:3 * 128 + (g + 1) * dk].astype(BF16)
        vs_ref[0, g, :, dk:] = ones_col
        kw_ref[0, g] = pk[:, 4 * 128 + g * dk:4 * 128 + (g + 1) * dk].astype(BF16)
        vw_ref[0, g] = pk[:, 5 * 128 + g * dk:5 * 128 + (g + 1) * dk].astype(BF16)
        gate_ref[0, g] = _sigmoid(proj(_C_GATE + g * 128, _C_GATE + (g + 1) * 128))
    u_ref[...] = proj(_C_U, _C_V)
    v_ref[...] = proj(_C_V, _C_MERGE)
    mg_ref[...] = _sigmoid(proj(_C_MERGE, _C_END))


def _in_proj(x2, g_mix, w_re, B, S):
    T, D = x2.shape
    tm = PROJ_ROWS
    per_b = S // tm
    G, R, dk = NSA_GROUPS, NSA_REP, NSA_HEAD_DIM
    blk_w = -(-(S // SLC_BLOCK) // LANES) * LANES

    def tok(i):
        return (i, 0)

    def bgs(i):
        return (i // per_b, 0, i % per_b, 0)

    out_shape = (
        jax.ShapeDtypeStruct((B, G, R, S, dk), BF16),
        jax.ShapeDtypeStruct((2, B, G, S, dk), F32),
        jax.ShapeDtypeStruct((B, G, S, blk_w + 2 * dk), BF16),
        jax.ShapeDtypeStruct((B, G, S, LANES), BF16),
        jax.ShapeDtypeStruct((B, G, S, dk), BF16),
        jax.ShapeDtypeStruct((B, G, S, dk), BF16),
        jax.ShapeDtypeStruct((B, G, S, LANES), F32),
        jax.ShapeDtypeStruct((T, 512), F32),
        jax.ShapeDtypeStruct((T, 512), F32),
        jax.ShapeDtypeStruct((T, 2 * D), F32),
    )
    kv_spec = pl.BlockSpec((1, G, tm, dk), bgs)
    out_specs = (
        pl.BlockSpec((1, G, R, tm, dk), lambda i: (i // per_b, 0, 0, i % per_b, 0)),
        pl.BlockSpec((2, 1, G, tm, dk), lambda i: (0, i // per_b, 0, i % per_b, 0)),
        pl.BlockSpec((1, G, tm, blk_w + 2 * dk), bgs),
        pl.BlockSpec((1, G, tm, LANES), bgs),
        kv_spec, kv_spec,
        pl.BlockSpec((1, G, tm, LANES), bgs),
        pl.BlockSpec((tm, 512), tok),
        pl.BlockSpec((tm, 512), tok),
        pl.BlockSpec((tm, 2 * D), tok),
    )
    return pl.pallas_call(
        functools.partial(_in_proj_kernel, seq_len=S),
        grid=(T // tm,),
        in_specs=[
            pl.BlockSpec((tm, D), tok),
            pl.BlockSpec((1, D), lambda i: (0, 0)),
            pl.BlockSpec((D, _C_END), lambda i: (0, 0)),
        ],
        out_specs=out_specs,
        out_shape=out_shape,
        compiler_params=_params(1),
        name="in_proj",
    )(x2, g_mix.reshape(1, D), w_re)


def _compress_kernel(x_ref, pe_ref, w1_ref, b1_ref, w2_ref, o_ref):
    x = x_ref[0, 0]
    half = x.shape[1]
    lo = (x + pe_ref[0, 0:1, :]).astype(BF16)
    hi = (x + pe_ref[0, 1:2, :]).astype(BF16)
    p = _dot(lo, w1_ref[0, :half, :])
    q = _dot(hi, w1_ref[0, half:, :])
    n = x.shape[0]
    h = p + pltpu.roll(q, n - 1, 0) + b1_ref[0]
    o_ref[0, 0] = _dot(_gelu(h).astype(BF16), w2_ref[0]).astype(BF16)


def _compress(kvc, pe, w1, b1, w2):
    _, BG, n_chunks, width = kvc.shape
    hidden = w1.shape[-1]
    dk = w2.shape[-1]
    return pl.pallas_call(
        _compress_kernel,
        grid=(2, BG),
        in_specs=[
            pl.BlockSpec((1, 1, n_chunks, width), lambda a, b: (a, b, 0, 0)),
            pl.BlockSpec((1, 2, width), lambda a, b: (a, 0, 0)),
            pl.BlockSpec((1, 2 * width, hidden), lambda a, b: (a, 0, 0)),
            pl.BlockSpec((1, 1, hidden), lambda a, b: (a, 0, 0)),
            pl.BlockSpec((1, hidden, dk), lambda a, b: (a, 0, 0)),
        ],
        out_specs=pl.BlockSpec((1, 1, n_chunks, dk), lambda a, b: (a, b, 0, 0)),
        out_shape=jax.ShapeDtypeStruct((2, BG, n_chunks, dk), BF16),
        compiler_params=_params(2),
        name="compress",
    )(kvc, pe, w1, b1, w2)


def _select_blocks(score_t):
    n = score_t.shape[0]
    row = lax.broadcasted_iota(I32, score_t.shape, 0).astype(F32)
    s = score_t
    for _ in range(SLC_TOPN):
        m = jnp.max(s, axis=0, keepdims=True)
        first = jnp.min(jnp.where(s == m, row, float(n)), axis=0, keepdims=True)
        s = jnp.where(row == first, -3.0e38, s)
    return jnp.where((s < -2.0e38) & (score_t > 0.5 * NEG_INF), 1.0, 0.0)


def _nsa_kernel(q_ref, ksa_ref, vsa_ref, kw_ref, vw_ref, kc_ref, vc_ref, gate_ref, ovt_ref, o_ref,
                kmax_scr):
    qb = pl.program_id(2)
    q0 = qb * NSA_QUERIES
    R, Q, dk = NSA_REP, NSA_QUERIES, NSA_HEAD_DIM
    q2 = q_ref[0, 0].reshape(R * Q, dk)

    kc = kc_ref[0, 0]
    n_cmp = kc.shape[0]
    s_c = _dot_nt(q2, kc)
    t_c = q0 + lax.broadcasted_iota(I32, (Q, n_cmp), 0)
    cmp_end = lax.broadcasted_iota(I32, (Q, n_cmp), 1) * CMP_STRIDE + (CMP_BLOCK - 1)
    mask_c = cmp_end <= t_c
    o_c = []
    p_sum = jnp.zeros((Q, n_cmp), F32)
    for r in range(R):
        s_r = jnp.where(mask_c, s_c[r * Q:(r + 1) * Q], NEG_INF)
        e = jnp.where(mask_c, jnp.exp2(s_r - jnp.max(s_r, axis=-1, keepdims=True)), 0.0)
        inv = 1.0 / jnp.maximum(jnp.sum(e, axis=-1, keepdims=True), TINY)
        p_sum = p_sum + e * inv
        o_c.append(_dot(e.astype(BF16), vc_ref[0, 0]) * inv)

    span = WINDOW + Q
    start = pl.multiple_of(jnp.maximum(q0 - WINDOW, 0), Q)
    kw = kw_ref[0, 0, pl.ds(start, span), :]
    vw = vw_ref[0, 0, pl.ds(start, span), :]
    s_w = _dot_nt(q2, kw)
    pos = start + lax.broadcasted_iota(I32, (Q, span), 1)
    diff = q0 + lax.broadcasted_iota(I32, (Q, span), 0) - pos
    mask_w = (diff >= 0) & (diff < WINDOW)
    o_w = []
    for r in range(R):
        s_r = jnp.where(mask_w, s_w[r * Q:(r + 1) * Q], NEG_INF)
        e = jnp.exp2(s_r - jnp.max(s_r, axis=-1, keepdims=True))
        inv = 1.0 / jnp.maximum(jnp.sum(e, axis=-1, keepdims=True), TINY)
        o_w.append(_dot(e.astype(BF16), vw) * inv)

    n_slc = ovt_ref.shape[0]
    ovt = ovt_ref[...]
    p_hi = p_sum.astype(BF16)
    rest = p_sum - p_hi.astype(F32)
    p_mid = rest.astype(BF16)
    p_lo = (rest - p_mid.astype(F32)).astype(BF16)
    imp_t = _dot_nt(ovt, p_hi) + _dot_nt(ovt, p_mid) + _dot_nt(ovt, p_lo)
    t_b = q0 + lax.broadcasted_iota(I32, (n_slc, Q), 1)
    blk = lax.broadcasted_iota(I32, (n_slc, Q), 0)
    allowed = blk * SLC_BLOCK <= t_b
    cur = t_b // SLC_BLOCK
    forced = (blk == 0) | (blk == cur) | (blk == cur - 1)
    score_t = jnp.where(forced & allowed, FORCE_BONUS, jnp.where(allowed, imp_t, NEG_INF))
    not_sel = 1.0 - _select_blocks(score_t).T

    tk = SEL_KEYS
    blk_w = ksa_ref.shape[3] - 2 * dk

    @pl.when(qb == 0)
    def _():
        k_all = ksa_ref[0, 0, :, blk_w:blk_w + dk].astype(F32)
        k_sq = jnp.max(jnp.sum(k_all * k_all, axis=-1, keepdims=True), axis=0, keepdims=True)
        kmax_scr[...] = jnp.broadcast_to(jnp.sqrt(k_sq), kmax_scr.shape)

    q_f = q2.astype(F32)
    q_sq = jnp.max(jnp.sum(q_f * q_f, axis=-1, keepdims=True), axis=0, keepdims=True)
    drop = (2.0 * jnp.sqrt(q_sq) * kmax_scr[0:1, 0:1] + MASK_MARGIN) * 1.02
    pieces = [not_sel] if blk_w == n_slc else [not_sel, jnp.zeros((Q, blk_w - n_slc), F32)]
    mask_cols = (jnp.concatenate(pieces, axis=1) * drop).astype(BF16)
    q_aug = jnp.concatenate(
        [jnp.concatenate([mask_cols] * R, axis=0), q2, jnp.zeros((R * Q, dk), BF16)], axis=1)
    last = q0 // tk
    va_w = vsa_ref.shape[3]

    def sel_tile(kt, carry, causal):
        ms, accs = carry
        base = pl.multiple_of(kt * tk, tk)
        s = _dot_nt(q_aug, ksa_ref[0, 0, pl.ds(base, tk), :])
        v_t = vsa_ref[0, 0, pl.ds(base, tk), :]
        if causal:
            ahead = (base + lax.broadcasted_iota(I32, (Q, tk), 1)
                     > q0 + lax.broadcasted_iota(I32, (Q, tk), 0))
        new_m, new_acc = [], []
        for r in range(R):
            s_r = s[r * Q:(r + 1) * Q]
            if causal:
                s_r = jnp.where(ahead, NEG_INF, s_r)
            m_new = jnp.maximum(ms[r], jnp.max(s_r, axis=-1, keepdims=True))
            alpha = jnp.exp2(ms[r] - m_new)
            p = jnp.exp2(s_r - m_new)
            new_m.append(m_new)
            new_acc.append(alpha * accs[r] + _dot(p.astype(BF16), v_t))
        return tuple(new_m), tuple(new_acc)

    init = (tuple(jnp.full((Q, 1), NEG_INF, F32) for _ in range(R)),
            tuple(jnp.zeros((Q, va_w), F32) for _ in range(R)))
    carry = lax.fori_loop(0, last, functools.partial(sel_tile, causal=False), init)
    _, accs = sel_tile(last, carry, causal=True)
    o_s = [accs[r][:, :dk] / jnp.maximum(accs[r][:, dk:dk + 1], TINY) for r in range(R)]

    gate = gate_ref[0, 0]
    for r in range(R):
        out = (gate[:, 3 * r:3 * r + 1] * o_c[r] + gate[:, 3 * r + 1:3 * r + 2] * o_s[r]
               + gate[:, 3 * r + 2:3 * r + 3] * o_w[r])
        o_ref[0, :, r * dk:(r + 1) * dk] = out.astype(BF16)


def _nsa(q, ksa, vsa, kw, vw, kcmp, vcmp, gates, overlap_t):
    B, G, R, S, dk = q.shape
    n_cmp = kcmp.shape[2]
    n_slc = overlap_t.shape[0]
    kv_spec = pl.BlockSpec((1, 1, S, dk), lambda b, g, i: (b, g, 0, 0))
    cmp_spec = pl.BlockSpec((1, 1, n_cmp, dk), lambda b, g, i: (b, g, 0, 0))
    return pl.pallas_call(
        _nsa_kernel,
        grid=(B, G, S // NSA_QUERIES),
        in_specs=[
            pl.BlockSpec((1, 1, R, NSA_QUERIES, dk), lambda b, g, i: (b, g, 0, i, 0)),
            pl.BlockSpec((1, 1, S, ksa.shape[3]), lambda b, g, i: (b, g, 0, 0)),
            pl.BlockSpec((1, 1, S, vsa.shape[3]), lambda b, g, i: (b, g, 0, 0)),
            kv_spec, kv_spec,
            cmp_spec, cmp_spec,
            pl.BlockSpec((1, 1, NSA_QUERIES, LANES), lambda b, g, i: (b, g, i, 0)),
            pl.BlockSpec((n_slc, n_cmp), lambda b, g, i: (0, 0)),
        ],
        out_specs=pl.BlockSpec((1, NSA_QUERIES, R * dk), lambda b, g, i: (b, i, g)),
        out_shape=jax.ShapeDtypeStruct((B, S, G * R * dk), BF16),
        scratch_shapes=[pltpu.VMEM((SUBLANES, LANES), F32)],
        compiler_params=_params(3),
        name="nsa",
    )(q, ksa, vsa, kw, vw, kcmp, vcmp, gates, overlap_t)


def _mix_kernel(x_ref, onsa_ref, u_ref, v_ref, mg_ref, lng_ref, lnb_ref, ws_ref, bs_ref,
                wn_ref, wg_ref, wm_ref, h_ref):
    D = x_ref.shape[1]
    u = _gelu(u_ref[...])
    v = _gelu(v_ref[...])
    mu = jnp.mean(v, axis=-1, keepdims=True)
    var = jnp.mean(jnp.square(v - mu), axis=-1, keepdims=True)
    vn = ((v - mu) * lax.rsqrt(var + NORM_EPS) * lng_ref[...] + lnb_ref[...]).astype(BF16)
    C = GMLP_CHUNK
    gd = vn.shape[1] // GMLP_GROUPS
    rows = []
    for c in range(vn.shape[0] // C):
        cols = []
        for g in range(GMLP_GROUPS):
            cols.append(_dot(ws_ref[g], vn[c * C:(c + 1) * C, g * gd:(g + 1) * gd]) + bs_ref[g])
        rows.append(jnp.concatenate(cols, axis=1))
    o_gmlp = u * jnp.concatenate(rows, axis=0)
    mg = mg_ref[...]
    y = (mg[:, :D] * _dot(onsa_ref[...], wn_ref[...])
         + mg[:, D:] * _dot(o_gmlp.astype(BF16), wg_ref[...]))
    h_ref[...] = x_ref[...] + _dot(y.astype(BF16), wm_ref[...])


def _mix(x2, o_nsa, u_raw, v_raw, mg, ln_g, ln_b, ws_causal, bs_b, w_nsa_out, w_gmlp_out, w_mix_out):
    T, D = x2.shape
    tm = MIX_ROWS
    W = u_raw.shape[1]

    def tok(i):
        return (i, 0)

    def const2(i):
        return (0, 0)

    def const3(i):
        return (0, 0, 0)

    return pl.pallas_call(
        _mix_kernel,
        grid=(T // tm,),
        in_specs=[
            pl.BlockSpec((tm, D), tok),
            pl.BlockSpec((tm, o_nsa.shape[1]), tok),
            pl.BlockSpec((tm, W), tok),
            pl.BlockSpec((tm, W), tok),
            pl.BlockSpec((tm, 2 * D), tok),
            pl.BlockSpec((1, W), const2),
            pl.BlockSpec((1, W), const2),
            pl.BlockSpec(ws_causal.shape, const3),
            pl.BlockSpec(bs_b.shape, const3),
            pl.BlockSpec(w_nsa_out.shape, const2),
            pl.BlockSpec(w_gmlp_out.shape, const2),
            pl.BlockSpec(w_mix_out.shape, const2),
        ],
        out_specs=pl.BlockSpec((tm, D), tok),
        out_shape=jax.ShapeDtypeStruct((T, D), F32),
        compiler_params=_params(1),
        name="mix",
    )(x2, o_nsa, u_raw, v_raw, mg, ln_g.reshape(1, W), ln_b.reshape(1, W), ws_causal, bs_b,
      w_nsa_out, w_gmlp_out, w_mix_out)


def _mem_kv_kernel(mem_ref, g_ref, w_ref, k_ref, v_ref):
    D = mem_ref.shape[2]
    mn = _rms(mem_ref[0], g_ref[...]).astype(BF16)
    kv = _dot(mn, w_ref[...])
    k_ref[0] = kv[:, :D].astype(BF16)
    v_ref[0] = kv[:, D:].astype(BF16)


def _mem_kv(mem, g_mem, w_xkv):
    B, M, D = mem.shape
    spec = pl.BlockSpec((1, M, D), lambda b: (b, 0, 0))
    return pl.pallas_call(
        _mem_kv_kernel,
        grid=(B,),
        in_specs=[spec, pl.BlockSpec((1, D), lambda b: (0, 0)),
                  pl.BlockSpec((D, 2 * D), lambda b: (0, 0))],
        out_specs=(spec, spec),
        out_shape=(jax.ShapeDtypeStruct((B, M, D), BF16), jax.ShapeDtypeStruct((B, M, D), BF16)),
        compiler_params=_params(1),
        name="mem_kv",
    )(mem, g_mem.reshape(1, D), w_xkv)


def _xattn_kernel(h_ref, g_ref, wq_ref, k_ref, v_ref, wo_ref, o_ref):
    h = h_ref[...]
    D = h.shape[1]
    hd = D // XATTN_HEADS
    hn = _rms(h, g_ref[...]).astype(BF16)
    q = _dot(hn, wq_ref[...]) * (hd ** -0.5)
    outs = []
    for a in range(XATTN_HEADS):
        s = _dot_nt(q[:, a * hd:(a + 1) * hd].astype(BF16), k_ref[0, :, a * hd:(a + 1) * hd])
        m = jnp.max(s, axis=-1, keepdims=True)
        e = jnp.exp(s - m)
        p = e / jnp.sum(e, axis=-1, keepdims=True)
        outs.append(_dot(p.astype(BF16), v_ref[0, :, a * hd:(a + 1) * hd]))
    o = jnp.concatenate(outs, axis=1).astype(BF16)
    o_ref[...] = h + _dot(o, wo_ref[...])


def _xattn(h, g_xattn, w_xq, k_mem, v_mem, w_xo, S):
    T, D = h.shape
    tm = MIX_ROWS
    per_b = S // tm
    M = k_mem.shape[1]
    mem_spec = pl.BlockSpec((1, M, D), lambda i: (i // per_b, 0, 0))
    return pl.pallas_call(
        _xattn_kernel,
        grid=(T // tm,),
        in_specs=[
            pl.BlockSpec((tm, D), lambda i: (i, 0)),
            pl.BlockSpec((1, D), lambda i: (0, 0)),
            pl.BlockSpec((D, D), lambda i: (0, 0)),
            mem_spec, mem_spec,
            pl.BlockSpec((D, D), lambda i: (0, 0)),
        ],
        out_specs=pl.BlockSpec((tm, D), lambda i: (i, 0)),
        out_shape=jax.ShapeDtypeStruct((T, D), F32),
        compiler_params=_params(1),
        name="xattn",
    )(h, g_xattn.reshape(1, D), w_xq, k_mem, v_mem, w_xo)


def _top_rows(vals, k, row=None):
    if row is None:
        row = lax.broadcasted_iota(I32, vals.shape, 0).astype(F32)
    top_v, top_i = [], []
    for _ in range(k):
        m = jnp.max(vals, axis=0, keepdims=True)
        first = jnp.min(jnp.where(vals == m, row, 1.0e9), axis=0, keepdims=True)
        top_v.append(m)
        top_i.append(first)
        vals = jnp.where(row == first, -3.0e38, vals)
    return top_v, top_i


def _pair_candidates(s0_all, s1_all, K):
    L, n_tok = s0_all.shape
    sub = lax.broadcasted_iota(I32, (SUBLANES, n_tok), 0).astype(F32)
    vals, ids = [], []
    a = 0
    while a < L and min(K // (a + 1), L) > 1:
        nb = min(K // (a + 1), L)
        for b0 in range(0, nb, SUBLANES):
            v = s0_all[a:a + 1] + s1_all[b0:b0 + SUBLANES]
            vals.append(v if b0 + SUBLANES <= nb else jnp.where(sub < float(nb - b0), v, -3.0e38))
            ids.append(sub + float(a * K + b0))
        a += 1
    while a < L:
        vals.append(s0_all[a:a + SUBLANES] + s1_all[0:1])
        ids.append((sub + float(a)) * float(K))
        a += SUBLANES
    return jnp.concatenate(vals, axis=0), jnp.concatenate(ids, axis=0)


def _route_from_scores(sc0, sc1):
    K = PEER_TOPK
    n_tok = sc0.shape[1]
    s0, i0 = _top_rows(sc0, K)
    s1, i1 = _top_rows(sc1, K)
    s0_all = jnp.concatenate(s0, axis=0)
    s1_all = jnp.concatenate(s1, axis=0)
    i0_all = jnp.concatenate(i0, axis=0)
    i1_all = jnp.concatenate(i1, axis=0)
    cand, cand_id = _pair_candidates(s0_all, s1_all, K)
    best_s, pos = _top_rows(cand, K, cand_id)
    best = jnp.concatenate(best_s, axis=0)
    m = jnp.max(best, axis=0, keepdims=True)
    e = jnp.exp(best - m)
    gate = e / jnp.sum(e, axis=0, keepdims=True)
    rank = lax.broadcasted_iota(I32, (K, n_tok), 0).astype(F32)
    ei, ej = [], []
    for kk in range(K):
        a = jnp.floor(pos[kk] * (1.0 / K))
        b = pos[kk] - a * K
        ei.append(jnp.sum(jnp.where(rank == a, i0_all, 0.0), axis=0, keepdims=True))
        ej.append(jnp.sum(jnp.where(rank == b, i1_all, 0.0), axis=0, keepdims=True))
    return jnp.concatenate(ei, axis=0), jnp.concatenate(ej, axis=0), gate


def _route_head(qt_scr, sk_ref, hd):
    half_dim = sk_ref.shape[2]
    scores = []
    for p in range(2):
        off = pl.multiple_of((hd * 2 + p) * half_dim, half_dim)
        qs = qt_scr[pl.ds(off, half_dim), :].astype(BF16)
        scores.append(_dot(sk_ref[hd * 2 + p], qs))
    return scores


def _route_kernel(h_ref, g_ref, wq_ref, sk_ref, hn_ref, i_ref, j_ref, gate_ref,
                  qt_scr, i_scr, j_scr, gate_scr):
    K = PEER_TOPK
    hn = _rms(h_ref[...], g_ref[...]).astype(BF16)
    hn_ref[...] = hn
    qt_scr[...] = _dot_nt(wq_ref[...], hn)

    def head(hd, _):
        sc0, sc1 = _route_head(qt_scr, sk_ref, hd)
        row0 = pl.multiple_of(hd * K, K)
        ei, ej, gate = _route_from_scores(sc0, sc1)
        i_scr[pl.ds(row0, K), :] = ei
        j_scr[pl.ds(row0, K), :] = ej
        gate_scr[pl.ds(row0, K), :] = gate
        return 0

    lax.fori_loop(0, PEER_HEADS, head, 0)
    i_ref[...] = i_scr[...].T.astype(I32)
    j_ref[...] = j_scr[...].T.astype(I32)
    gate_ref[...] = gate_scr[...].T


def _route(h, g_peer, wq_t, sub_keys):
    T, D = h.shape
    tt = ROUTE_TOKENS
    n_sel = PEER_HEADS * PEER_TOPK
    tok = lambda i: (i, 0)
    return pl.pallas_call(
        _route_kernel,
        grid=(T // tt,),
        in_specs=[
            pl.BlockSpec((tt, D), tok),
            pl.BlockSpec((1, D), lambda i: (0, 0)),
            pl.BlockSpec(wq_t.shape, lambda i: (0, 0)),
            pl.BlockSpec(sub_keys.shape, lambda i: (0, 0, 0)),
        ],
        out_specs=(
            pl.BlockSpec((tt, D), tok),
            pl.BlockSpec((tt, n_sel), tok),
            pl.BlockSpec((tt, n_sel), tok),
            pl.BlockSpec((tt, n_sel), tok),
        ),
        out_shape=(
            jax.ShapeDtypeStruct((T, D), BF16),
            jax.ShapeDtypeStruct((T, n_sel), I32),
            jax.ShapeDtypeStruct((T, n_sel), I32),
            jax.ShapeDtypeStruct((T, n_sel), F32),
        ),
        scratch_shapes=[
            pltpu.VMEM((wq_t.shape[0], tt), F32),
            pltpu.VMEM((n_sel, tt), F32),
            pltpu.VMEM((n_sel, tt), F32),
            pltpu.VMEM((n_sel, tt), F32),
        ],
        compiler_params=_params(1),
        name="route",
    )(h, g_peer.reshape(1, D), wq_t, sub_keys)


def _peer_up_kernel(hn_ref, u_ref, i_ref, j_ref, a_ref):
    e = pl.program_id(1)

    @pl.when(e == 0)
    def _():
        a_ref[...] = jnp.zeros(a_ref.shape, F32)

    hn = hn_ref[...]
    ii = i_ref[...]
    jj = j_ref[...]
    acc = a_ref[...]
    for r0 in range(0, EXPERT_ROWS, 2):
        z = _dot_nt(hn, u_ref[r0 * PEER_KEYS:(r0 + 2) * PEER_KEYS, :])
        for r in range(r0, r0 + 2):
            picked = jnp.take_along_axis(
                z[:, (r - r0) * PEER_KEYS:(r - r0 + 1) * PEER_KEYS], jj, axis=1)
            acc = jnp.where(ii == e * EXPERT_ROWS + r, picked, acc)
    a_ref[...] = acc


def _peer_up(hn, u_bf, i_idx, j_idx):
    T, D = hn.shape
    tt = UP_TOKENS
    n_sel = i_idx.shape[1]
    rows = EXPERT_ROWS * PEER_KEYS
    tok = lambda t, e: (t, 0)
    return pl.pallas_call(
        _peer_up_kernel,
        grid=(T // tt, u_bf.shape[0] // rows),
        in_specs=[
            pl.BlockSpec((tt, D), tok),
            pl.BlockSpec((rows, D), lambda t, e: (e, 0)),
            pl.BlockSpec((tt, n_sel), tok),
            pl.BlockSpec((tt, n_sel), tok),
        ],
        out_specs=pl.BlockSpec((tt, n_sel), tok),
        out_shape=jax.ShapeDtypeStruct((T, n_sel), F32),
        compiler_params=_params(2),
        name="peer_up",
    )(hn, u_bf, i_idx, j_idx)


def _peer_down_kernel(a_ref, gate_ref, i_ref, j_ref, v_ref, h_ref, g_ref, o_ref,
                      w_scr, coef_scr, acc_scr):
    e = pl.program_id(1)
    n_tok = a_ref.shape[0]
    nk = PEER_KEYS

    @pl.when(e == 0)
    def _():
        coef_scr[...] = gate_ref[...] * _gelu(a_ref[...])
        acc_scr[...] = jnp.zeros(acc_scr.shape, F32)
        sub = lax.broadcasted_iota(I32, (nk, i_ref.shape[1]), 0)

        def expand(t, _):
            ii = i_ref[pl.ds(t, 1), :]
            jj = j_ref[pl.ds(t, 1), :]
            cc = coef_scr[pl.ds(t, 1), :]
            a = jnp.where(sub == ii, cc, 0.0).astype(BF16)
            bt = jnp.where(sub == jj, 1.0, 0.0).astype(BF16)
            base = pl.multiple_of(t * W_ROW_STRIDE, SUBLANES)
            w_scr[pl.ds(base, nk), :] = _dot_nt(a, bt)
            return 0

        lax.fori_loop(0, n_tok, expand, 0, unroll=EXPAND_UNROLL)

    parts = []
    for r in range(DOWN_EXPERT_ROWS):
        row = e * DOWN_EXPERT_ROWS + r
        parts.append(w_scr[pl.ds(row, n_tok, stride=W_ROW_STRIDE), :].astype(BF16))
    lhs = jnp.concatenate(parts, axis=1)
    acc_scr[...] += _dot(lhs, v_ref[...])

    @pl.when(e == pl.num_programs(1) - 1)
    def _():
        o_ref[...] = _rms(h_ref[...] + acc_scr[...], g_ref[...])


def _peer_down(a_pre, gate, i_idx, j_idx, v_bf, h, g_final):
    T, D = h.shape
    tt = DOWN_TOKENS
    n_sel = i_idx.shape[1]
    rows = DOWN_EXPERT_ROWS * PEER_KEYS
    tok = lambda t, e: (t, 0)
    sel_spec = pl.BlockSpec((tt, n_sel), tok)
    return pl.pallas_call(
        _peer_down_kernel,
        grid=(T // tt, v_bf.shape[0] // rows),
        in_specs=[
            sel_spec, sel_spec, sel_spec, sel_spec,
            pl.BlockSpec((rows, D), lambda t, e: (e, 0)),
            pl.BlockSpec((tt, D), tok),
            pl.BlockSpec((1, D), lambda t, e: (0, 0)),
        ],
        out_specs=pl.BlockSpec((tt, D), tok),
        out_shape=jax.ShapeDtypeStruct((T, D), F32),
        scratch_shapes=[
            pltpu.VMEM((tt * W_ROW_STRIDE, PEER_KEYS), F32),
            pltpu.VMEM((tt, n_sel), F32),
            pltpu.VMEM((tt, D), F32),
        ],
        compiler_params=_params(2),
        name="peer_down",
    )(a_pre, gate, i_idx, j_idx, v_bf, h, g_final.reshape(1, D))


def _overlap_table(n_cmp, n_slc):
    c0 = jnp.arange(n_cmp) * CMP_STRIDE
    j0 = jnp.arange(n_slc) * SLC_BLOCK
    ov = jnp.clip(jnp.minimum(c0[:, None] + CMP_BLOCK, j0[None, :] + SLC_BLOCK)
                  - jnp.maximum(c0[:, None], j0[None, :]), 0, None)
    return ov.astype(F32) / CMP_BLOCK


def kernel(x, mem, g_mix, w_in, cmp_pe_k, cmp_w1_k, cmp_b1_k, cmp_w2_k, cmp_pe_v, cmp_w1_v, cmp_b1_v, cmp_w2_v, gmlp_ln_g, gmlp_ln_b, gmlp_ws, gmlp_bs, w_nsa_out, w_gmlp_out, w_mix_out, g_xattn, g_mem, w_xq, w_xkv, w_xo, g_peer, w_peer_q, peer_sub_keys, peer_u, peer_v, g_final):
    B, S, D = x.shape
    T = B * S
    G, R, dk = NSA_GROUPS, NSA_REP, NSA_HEAD_DIM
    x2 = x.reshape(T, D)

    n_gate = NSA_HEADS * 3
    gate_cols = w_in[:, 1280:1280 + n_gate]
    per_g = n_gate // G
    gate_blocks = [jnp.pad(gate_cols[:, g * per_g:(g + 1) * per_g], ((0, 0), (0, LANES - per_g)))
                   for g in range(G)]
    w_re = jnp.concatenate([w_in[:, :1280]] + gate_blocks + [w_in[:, 1280 + n_gate:]],
                           axis=1).astype(BF16)
    half = CMP_STRIDE * dk
    pe = jnp.stack([cmp_pe_k.reshape(2, half), cmp_pe_v.reshape(2, half)])
    w1 = jnp.stack([cmp_w1_k, cmp_w1_v]).astype(BF16)
    b1 = jnp.stack([cmp_b1_k, cmp_b1_v])[:, None, :]
    w2 = jnp.stack([cmp_w2_k, cmp_w2_v]).astype(BF16)
    C = GMLP_CHUNK
    ws_causal = (gmlp_ws * jnp.tril(jnp.ones((C, C), F32))).astype(BF16)
    group_dim = gmlp_ln_g.shape[0] // GMLP_GROUPS
    bs_b = jnp.broadcast_to(gmlp_bs[:, :, None], (GMLP_GROUPS, C, group_dim))
    n_chunks = S // CMP_STRIDE
    overlap = _overlap_table(n_chunks, S // SLC_BLOCK)

    q, kvc, ksa, vsa, kw, vw, gates, u_raw, v_raw, mg = _in_proj(x2, g_mix, w_re, B, S)
    cmp_kv = _compress(kvc.reshape(2, B * G, n_chunks, half), pe, w1, b1, w2)
    cmp_kv = cmp_kv.reshape(2, B, G, n_chunks, dk)
    o_nsa = _nsa(q, ksa, vsa, kw, vw, cmp_kv[0], cmp_kv[1], gates, overlap.T.astype(BF16))
    h = _mix(x2, o_nsa.reshape(T, G * R * dk), u_raw, v_raw, mg, gmlp_ln_g, gmlp_ln_b,
             ws_causal, bs_b, w_nsa_out.astype(BF16), w_gmlp_out.astype(BF16),
             w_mix_out.astype(BF16))

    k_mem, v_mem = _mem_kv(mem, g_mem, w_xkv.astype(BF16))
    h = _xattn(h, g_xattn, w_xq.astype(BF16), k_mem, v_mem, w_xo.astype(BF16), S)

    half_dim = peer_sub_keys.shape[3]
    sk = peer_sub_keys.reshape(PEER_HEADS * 2, PEER_KEYS, half_dim).astype(BF16)
    hn, i_idx, j_idx, gate = _route(h, g_peer, w_peer_q.T.astype(BF16), sk)
    a_pre = _peer_up(hn, peer_u.astype(BF16), i_idx, j_idx)
    out = _peer_down(a_pre, gate, i_idx, j_idx, peer_v.astype(BF16), h, g_final)
    return out.reshape(B, S, D)
```

```python
import functools

import jax
import jax.numpy as jnp
from jax import lax
from jax.experimental import pallas as pl
from jax.experimental.pallas import tpu as pltpu

F32 = jnp.float32
BF16 = jnp.bfloat16
I32 = jnp.int32

NORM_EPS = 1e-6
NEG_INF = -1e30
TINY = 1e-30
LOG2_E = 1.4426950408889634
MASK_MARGIN = 300.0

NSA_HEADS = 8
NSA_HEAD_DIM = 64
NSA_GROUPS = 2
NSA_REP = NSA_HEADS // NSA_GROUPS
CMP_BLOCK = 32
CMP_STRIDE = 16
SLC_BLOCK = 64
SLC_TOPN = 16
WINDOW = 512
FORCE_BONUS = 1e4
GMLP_GROUPS = 4
GMLP_CHUNK = 128
XATTN_HEADS = 4
PEER_HEADS = 8
PEER_KEYS = 128
PEER_TOPK = 16

LANES = 128
SUBLANES = 8
VMEM_LIMIT_BYTES = 56 * 1024 * 1024

PROJ_ROWS = 512
MIX_ROWS = 512
NSA_QUERIES = 256
SEL_KEYS = 1024
ROUTE_TOKENS = 512
UP_TOKENS = 1024
DOWN_TOKENS = 512
EXPERT_ROWS = 32
DOWN_EXPERT_ROWS = 8
W_ROW_STRIDE = PEER_KEYS + SUBLANES
EXPAND_UNROLL = 64


def _params(n_axes):
    return pltpu.CompilerParams(
        dimension_semantics=("arbitrary",) * n_axes,
        vmem_limit_bytes=VMEM_LIMIT_BYTES,
    )


def _rms(x, g):
    return x * lax.rsqrt(jnp.mean(x * x, axis=-1, keepdims=True) + NORM_EPS) * g


def _gelu(x):
    return 0.5 * x * (1.0 + lax.erf(x * 0.7071067811865476))


def _sigmoid(x):
    return 1.0 / (1.0 + jnp.exp(-x))


def _dot(a, b):
    return jnp.dot(a, b, preferred_element_type=F32)


def _dot_nt(a, b):
    return lax.dot_general(a, b, (((1,), (1,)), ((), ())), preferred_element_type=F32)


_C_Q = 0
_C_KV = 512
_C_GATE = 1280
_C_U = 1536


def _in_proj_kernel(x_ref, g_ref, w_ref, q_ref, kvc_ref, ks_ref, vs_ref, kw_ref, vw_ref,
                    gate_ref, *, seq_len):
    hn = _rms(x_ref[...], g_ref[...]).astype(BF16)

    def proj(a, b):
        return _dot(hn, w_ref[:, a:b])

    dk = NSA_HEAD_DIM
    pq = proj(_C_Q, _C_KV) * (dk ** -0.5 * LOG2_E)
    for g in range(NSA_GROUPS):
        for r in range(NSA_REP):
            c = (g * NSA_REP + r) * dk
            q_ref[0, g, r] = pq[:, c:c + dk].astype(BF16)
    pk = proj(_C_KV, _C_GATE)
    tm = x_ref.shape[0]
    blk_w = ks_ref.shape[3] - 2 * dk
    pos = (pl.program_id(0) % (seq_len // tm)) * tm + lax.broadcasted_iota(I32, (tm, blk_w), 0)
    own_block = jnp.where(pos // SLC_BLOCK == lax.broadcasted_iota(I32, (tm, blk_w), 1),
                          -1.0, 0.0).astype(BF16)
    ones_col = jnp.where(lax.broadcasted_iota(I32, (tm, LANES - dk), 1) == 0, 1.0, 0.0).astype(BF16)
    for g in range(NSA_GROUPS):
        kvc_ref[0, 0, g] = pk[:, 0 * 128 + g * dk:0 * 128 + (g + 1) * dk]
        kvc_ref[1, 0, g] = pk[:, 1 * 128 + g * dk:1 * 128 + (g + 1) * dk]
        ks_ref[0, g, :, :blk_w] = own_block
        ks_ref[0, g, :, blk_w:blk_w + dk] = (
            pk[:, 2 * 128 + g * dk:2 * 128 + (g + 1) * dk].astype(BF16))
        ks_ref[0, g, :, blk_w + dk:] = jnp.zeros((tm, dk), BF16)
        vs_ref[0, g, :, :dk] = pk[:, 3 * 128 + g * dk:3 * 128 + (g + 1) * dk].astype(BF16)
        vs_ref[0, g, :, dk:] = ones_col
        kw_ref[0, g] = pk[:, 4 * 128 + g * dk:4 * 128 + (g + 1) * dk].astype(BF16)
        vw_ref[0, g] = pk[:, 5 * 128 + g * dk:5 * 128 + (g + 1) * dk].astype(BF16)
        gate_ref[0, g] = _sigmoid(proj(_C_GATE + g * 128, _C_GATE + (g + 1) * 128))


def _in_proj(x2, g_mix, w_re, B, S):
    T, D = x2.shape
    tm = PROJ_ROWS
    per_b = S // tm
    G, R, dk = NSA_GROUPS, NSA_REP, NSA_HEAD_DIM
    blk_w = -(-(S // SLC_BLOCK) // LANES) * LANES

    def tok(i):
        return (i, 0)

    def bgs(i):
        return (i // per_b, 0, i % per_b, 0)

    out_shape = (
        jax.ShapeDtypeStruct((B, G, R, S, dk), BF16),
        jax.ShapeDtypeStruct((2, B, G, S, dk), F32),
        jax.ShapeDtypeStruct((B, G, S, blk_w + 2 * dk), BF16),
        jax.ShapeDtypeStruct((B, G, S, LANES), BF16),
        jax.ShapeDtypeStruct((B, G, S, dk), BF16),
        jax.ShapeDtypeStruct((B, G, S, dk), BF16),
        jax.ShapeDtypeStruct((B, G, S, LANES), F32),
    )
    kv_spec = pl.BlockSpec((1, G, tm, dk), bgs)
    out_specs = (
        pl.BlockSpec((1, G, R, tm, dk), lambda i: (i // per_b, 0, 0, i % per_b, 0)),
        pl.BlockSpec((2, 1, G, tm, dk), lambda i: (0, i // per_b, 0, i % per_b, 0)),
        pl.BlockSpec((1, G, tm, blk_w + 2 * dk), bgs),
        pl.BlockSpec((1, G, tm, LANES), bgs),
        kv_spec, kv_spec,
        pl.BlockSpec((1, G, tm, LANES), bgs),
    )
    return pl.pallas_call(
        functools.partial(_in_proj_kernel, seq_len=S),
        grid=(T // tm,),
        in_specs=[
            pl.BlockSpec((tm, D), tok),
            pl.BlockSpec((1, D), lambda i: (0, 0)),
            pl.BlockSpec((D, _C_U), lambda i: (0, 0)),
        ],
        out_specs=out_specs,
        out_shape=out_shape,
        compiler_params=_params(1),
        name="in_proj",
    )(x2, g_mix.reshape(1, D), w_re)


def _compress_kernel(x_ref, pe_ref, w1_ref, b1_ref, w2_ref, o_ref):
    x = x_ref[0, 0]
    half = x.shape[1]
    lo = (x + pe_ref[0, 0:1, :]).astype(BF16)
    hi = (x + pe_ref[0, 1:2, :]).astype(BF16)
    p = _dot(lo, w1_ref[0, :half, :])
    q = _dot(hi, w1_ref[0, half:, :])
    n = x.shape[0]
    h = p + pltpu.roll(q, n - 1, 0) + b1_ref[0]
    o_ref[0, 0] = _dot(_gelu(h).astype(BF16), w2_ref[0]).astype(BF16)


def _compress(kvc, pe, w1, b1, w2):
    _, BG, n_chunks, width = kvc.shape
    hidden = w1.shape[-1]
    dk = w2.shape[-1]
    return pl.pallas_call(
        _compress_kernel,
        grid=(2, BG),
        in_specs=[
            pl.BlockSpec((1, 1, n_chunks, width), lambda a, b: (a, b, 0, 0)),
            pl.BlockSpec((1, 2, width), lambda a, b: (a, 0, 0)),
            pl.BlockSpec((1, 2 * width, hidden), lambda a, b: (a, 0, 0)),
            pl.BlockSpec((1, 1, hidden), lambda a, b: (a, 0, 0)),
            pl.BlockSpec((1, hidden, dk), lambda a, b: (a, 0, 0)),
        ],
        out_specs=pl.BlockSpec((1, 1, n_chunks, dk), lambda a, b: (a, b, 0, 0)),
        out_shape=jax.ShapeDtypeStruct((2, BG, n_chunks, dk), BF16),
        compiler_params=_params(2),
        name="compress",
    )(kvc, pe, w1, b1, w2)


def _select_blocks(score_t):
    n = score_t.shape[0]
    row = lax.broadcasted_iota(I32, score_t.shape, 0).astype(F32)
    s = score_t
    for _ in range(SLC_TOPN):
        m = jnp.max(s, axis=0, keepdims=True)
        first = jnp.min(jnp.where(s == m, row, float(n)), axis=0, keepdims=True)
        s = jnp.where(row == first, -3.0e38, s)
    return jnp.where((s < -2.0e38) & (score_t > 0.5 * NEG_INF), 1.0, 0.0)


def _nsa_kernel(q_ref, ksa_ref, vsa_ref, kw_ref, vw_ref, kc_ref, vc_ref, gate_ref, ovt_ref, o_ref,
                kmax_scr):
    qb = pl.program_id(2)
    q0 = qb * NSA_QUERIES
    R, Q, dk = NSA_REP, NSA_QUERIES, NSA_HEAD_DIM
    q2 = q_ref[0, 0].reshape(R * Q, dk)

    kc = kc_ref[0, 0]
    n_cmp = kc.shape[0]
    s_c = _dot_nt(q2, kc)
    t_c = q0 + lax.broadcasted_iota(I32, (Q, n_cmp), 0)
    cmp_end = lax.broadcasted_iota(I32, (Q, n_cmp), 1) * CMP_STRIDE + (CMP_BLOCK - 1)
    mask_c = cmp_end <= t_c
    o_c = []
    p_sum = jnp.zeros((Q, n_cmp), F32)
    for r in range(R):
        s_r = jnp.where(mask_c, s_c[r * Q:(r + 1) * Q], NEG_INF)
        e = jnp.where(mask_c, jnp.exp2(s_r - jnp.max(s_r, axis=-1, keepdims=True)), 0.0)
        inv = 1.0 / jnp.maximum(jnp.sum(e, axis=-1, keepdims=True), TINY)
        p_sum = p_sum + e * inv
        o_c.append(_dot(e.astype(BF16), vc_ref[0, 0]) * inv)

    span = WINDOW + Q
    start = pl.multiple_of(jnp.maximum(q0 - WINDOW, 0), Q)
    kw = kw_ref[0, 0, pl.ds(start, span), :]
    vw = vw_ref[0, 0, pl.ds(start, span), :]
    s_w = _dot_nt(q2, kw)
    pos = start + lax.broadcasted_iota(I32, (Q, span), 1)
    diff = q0 + lax.broadcasted_iota(I32, (Q, span), 0) - pos
    mask_w = (diff >= 0) & (diff < WINDOW)
    o_w = []
    for r in range(R):
        s_r = jnp.where(mask_w, s_w[r * Q:(r + 1) * Q], NEG_INF)
        e = jnp.exp2(s_r - jnp.max(s_r, axis=-1, keepdims=True))
        inv = 1.0 / jnp.maximum(jnp.sum(e, axis=-1, keepdims=True), TINY)
        o_w.append(_dot(e.astype(BF16), vw) * inv)

    n_slc = ovt_ref.shape[0]
    ovt = ovt_ref[...]
    p_hi = p_sum.astype(BF16)
    rest = p_sum - p_hi.astype(F32)
    p_mid = rest.astype(BF16)
    p_lo = (rest - p_mid.astype(F32)).astype(BF16)
    imp_t = _dot_nt(ovt, p_hi) + _dot_nt(ovt, p_mid) + _dot_nt(ovt, p_lo)
    t_b = q0 + lax.broadcasted_iota(I32, (n_slc, Q), 1)
    blk = lax.broadcasted_iota(I32, (n_slc, Q), 0)
    allowed = blk * SLC_BLOCK <= t_b
    cur = t_b // SLC_BLOCK
    forced = (blk == 0) | (blk == cur) | (blk == cur - 1)
    score_t = jnp.where(forced & allowed, FORCE_BONUS, jnp.where(allowed, imp_t, NEG_INF))
    not_sel = 1.0 - _select_blocks(score_t).T

    tk = SEL_KEYS
    blk_w = ksa_ref.shape[3] - 2 * dk

    @pl.when(qb == 0)
    def _():
        k_all = ksa_ref[0, 0, :, blk_w:blk_w + dk].astype(F32)
        k_sq = jnp.max(jnp.sum(k_all * k_all, axis=-1, keepdims=True), axis=0, keepdims=True)
        kmax_scr[...] = jnp.broadcast_to(jnp.sqrt(k_sq), kmax_scr.shape)

    q_f = q2.astype(F32)
    q_sq = jnp.max(jnp.sum(q_f * q_f, axis=-1, keepdims=True), axis=0, keepdims=True)
    drop = (2.0 * jnp.sqrt(q_sq) * kmax_scr[0:1, 0:1] + MASK_MARGIN) * 1.02
    pieces = [not_sel] if blk_w == n_slc else [not_sel, jnp.zeros((Q, blk_w - n_slc), F32)]
    mask_cols = (jnp.concatenate(pieces, axis=1) * drop).astype(BF16)
    q_aug = jnp.concatenate(
        [jnp.concatenate([mask_cols] * R, axis=0), q2, jnp.zeros((R * Q, dk), BF16)], axis=1)
    last = q0 // tk
    va_w = vsa_ref.shape[3]

    def sel_tile(kt, carry, causal):
        ms, accs = carry
        base = pl.multiple_of(kt * tk, tk)
        s = _dot_nt(q_aug, ksa_ref[0, 0, pl.ds(base, tk), :])
        v_t = vsa_ref[0, 0, pl.ds(base, tk), :]
        if causal:
            ahead = (base + lax.broadcasted_iota(I32, (Q, tk), 1)
                     > q0 + lax.broadcasted_iota(I32, (Q, tk), 0))
        new_m, new_acc = [], []
        for r in range(R):
            s_r = s[r * Q:(r + 1) * Q]
            if causal:
                s_r = jnp.where(ahead, NEG_INF, s_r)
            m_new = jnp.maximum(ms[r], jnp.max(s_r, axis=-1, keepdims=True))
            alpha = jnp.exp2(ms[r] - m_new)
            p = jnp.exp2(s_r - m_new)
            new_m.append(m_new)
            new_acc.append(alpha * accs[r] + _dot(p.astype(BF16), v_t))
        return tuple(new_m), tuple(new_acc)

    init = (tuple(jnp.full((Q, 1), NEG_INF, F32) for _ in range(R)),
            tuple(jnp.zeros((Q, va_w), F32) for _ in range(R)))
    carry = lax.fori_loop(0, last, functools.partial(sel_tile, causal=False), init)
    _, accs = sel_tile(last, carry, causal=True)
    o_s = [accs[r][:, :dk] / jnp.maximum(accs[r][:, dk:dk + 1], TINY) for r in range(R)]

    gate = gate_ref[0, 0]
    for r in range(R):
        out = (gate[:, 3 * r:3 * r + 1] * o_c[r] + gate[:, 3 * r + 1:3 * r + 2] * o_s[r]
               + gate[:, 3 * r + 2:3 * r + 3] * o_w[r])
        o_ref[0, :, r * dk:(r + 1) * dk] = out.astype(BF16)


def _nsa(q, ksa, vsa, kw, vw, kcmp, vcmp, gates, overlap_t):
    B, G, R, S, dk = q.shape
    n_cmp = kcmp.shape[2]
    n_slc = overlap_t.shape[0]
    kv_spec = pl.BlockSpec((1, 1, S, dk), lambda b, g, i: (b, g, 0, 0))
    cmp_spec = pl.BlockSpec((1, 1, n_cmp, dk), lambda b, g, i: (b, g, 0, 0))
    return pl.pallas_call(
        _nsa_kernel,
        grid=(B, G, S // NSA_QUERIES),
        in_specs=[
            pl.BlockSpec((1, 1, R, NSA_QUERIES, dk), lambda b, g, i: (b, g, 0, i, 0)),
            pl.BlockSpec((1, 1, S, ksa.shape[3]), lambda b, g, i: (b, g, 0, 0)),
            pl.BlockSpec((1, 1, S, vsa.shape[3]), lambda b, g, i: (b, g, 0, 0)),
            kv_spec, kv_spec,
            cmp_spec, cmp_spec,
            pl.BlockSpec((1, 1, NSA_QUERIES, LANES), lambda b, g, i: (b, g, i, 0)),
            pl.BlockSpec((n_slc, n_cmp), lambda b, g, i: (0, 0)),
        ],
        out_specs=pl.BlockSpec((1, NSA_QUERIES, R * dk), lambda b, g, i: (b, i, g)),
        out_shape=jax.ShapeDtypeStruct((B, S, G * R * dk), BF16),
        scratch_shapes=[pltpu.VMEM((SUBLANES, LANES), F32)],
        compiler_params=_params(3),
        name="nsa",
    )(q, ksa, vsa, kw, vw, kcmp, vcmp, gates, overlap_t)


def _mix_kernel(x_ref, g_ref, wuvm_ref, onsa_ref, lng_ref, lnb_ref, ws_ref, bs_ref,
                wn_ref, wg_ref, wm_ref, gx_ref, wq_ref, k_ref, v_ref, wo_ref, h_ref):
    D = x_ref.shape[1]
    W = lng_ref.shape[1]
    hn = _rms(x_ref[...], g_ref[...]).astype(BF16)
    u = _gelu(_dot(hn, wuvm_ref[:, :W]))
    v = _gelu(_dot(hn, wuvm_ref[:, W:2 * W]))
    mu = jnp.mean(v, axis=-1, keepdims=True)
    var = jnp.mean(jnp.square(v - mu), axis=-1, keepdims=True)
    vn = ((v - mu) * lax.rsqrt(var + NORM_EPS) * lng_ref[...] + lnb_ref[...]).astype(BF16)
    C = GMLP_CHUNK
    gd = vn.shape[1] // GMLP_GROUPS
    rows = []
    for c in range(vn.shape[0] // C):
        cols = []
        for g in range(GMLP_GROUPS):
            cols.append(_dot(ws_ref[g], vn[c * C:(c + 1) * C, g * gd:(g + 1) * gd]) + bs_ref[g])
        rows.append(jnp.concatenate(cols, axis=1))
    o_gmlp = u * jnp.concatenate(rows, axis=0)
    mg = _sigmoid(_dot(hn, wuvm_ref[:, 2 * W:]))
    y = (mg[:, :D] * _dot(onsa_ref[...], wn_ref[...])
         + mg[:, D:] * _dot(o_gmlp.astype(BF16), wg_ref[...]))
    h = x_ref[...] + _dot(y.astype(BF16), wm_ref[...])
    h_ref[...] = _xattn_block(h, gx_ref, wq_ref, k_ref, v_ref, wo_ref)


def _mix(x2, g_mix, w_uvm, o_nsa, ln_g, ln_b, ws_causal, bs_b, w_nsa_out, w_gmlp_out, w_mix_out,
         g_xattn, w_xq, k_mem, v_mem, w_xo, S):
    T, D = x2.shape
    tm = MIX_ROWS
    W = ln_g.shape[0]
    per_b = S // tm
    mem_spec = pl.BlockSpec((1, k_mem.shape[1], D), lambda i: (i // per_b, 0, 0))

    def tok(i):
        return (i, 0)

    def const2(i):
        return (0, 0)

    def const3(i):
        return (0, 0, 0)

    return pl.pallas_call(
        _mix_kernel,
        grid=(T // tm,),
        in_specs=[
            pl.BlockSpec((tm, D), tok),
            pl.BlockSpec((1, D), const2),
            pl.BlockSpec(w_uvm.shape, const2),
            pl.BlockSpec((tm, o_nsa.shape[1]), tok),
            pl.BlockSpec((1, W), const2),
            pl.BlockSpec((1, W), const2),
            pl.BlockSpec(ws_causal.shape, const3),
            pl.BlockSpec(bs_b.shape, const3),
            pl.BlockSpec(w_nsa_out.shape, const2),
            pl.BlockSpec(w_gmlp_out.shape, const2),
            pl.BlockSpec(w_mix_out.shape, const2),
            pl.BlockSpec((1, D), const2),
            pl.BlockSpec((D, D), const2),
            mem_spec, mem_spec,
            pl.BlockSpec((D, D), const2),
        ],
        out_specs=pl.BlockSpec((tm, D), tok),
        out_shape=jax.ShapeDtypeStruct((T, D), F32),
        compiler_params=_params(1),
        name="mix",
    )(x2, g_mix.reshape(1, D), w_uvm, o_nsa, ln_g.reshape(1, W), ln_b.reshape(1, W), ws_causal,
      bs_b, w_nsa_out, w_gmlp_out, w_mix_out, g_xattn.reshape(1, D), w_xq, k_mem, v_mem, w_xo)


def _mem_kv_kernel(mem_ref, g_ref, w_ref, k_ref, v_ref):
    D = mem_ref.shape[2]
    mn = _rms(mem_ref[0], g_ref[...]).astype(BF16)
    kv = _dot(mn, w_ref[...])
    k_ref[0] = kv[:, :D].astype(BF16)
    v_ref[0] = kv[:, D:].astype(BF16)


def _mem_kv(mem, g_mem, w_xkv):
    B, M, D = mem.shape
    spec = pl.BlockSpec((1, M, D), lambda b: (b, 0, 0))
    return pl.pallas_call(
        _mem_kv_kernel,
        grid=(B,),
        in_specs=[spec, pl.BlockSpec((1, D), lambda b: (0, 0)),
                  pl.BlockSpec((D, 2 * D), lambda b: (0, 0))],
        out_specs=(spec, spec),
        out_shape=(jax.ShapeDtypeStruct((B, M, D), BF16), jax.ShapeDtypeStruct((B, M, D), BF16)),
        compiler_params=_params(1),
        name="mem_kv",
    )(mem, g_mem.reshape(1, D), w_xkv)


def _xattn_block(h, g_ref, wq_ref, k_ref, v_ref, wo_ref):
    D = h.shape[1]
    hd = D // XATTN_HEADS
    hn = _rms(h, g_ref[...]).astype(BF16)
    q = _dot(hn, wq_ref[...]) * (hd ** -0.5)
    outs = []
    for a in range(XATTN_HEADS):
        s = _dot_nt(q[:, a * hd:(a + 1) * hd].astype(BF16), k_ref[0, :, a * hd:(a + 1) * hd])
        m = jnp.max(s, axis=-1, keepdims=True)
        e = jnp.exp(s - m)
        p = e / jnp.sum(e, axis=-1, keepdims=True)
        outs.append(_dot(p.astype(BF16), v_ref[0, :, a * hd:(a + 1) * hd]))
    o = jnp.concatenate(outs, axis=1).astype(BF16)
    return h + _dot(o, wo_ref[...])


def _top_rows(vals, k, row=None):
    if row is None:
        row = lax.broadcasted_iota(I32, vals.shape, 0).astype(F32)
    top_v, top_i = [], []
    for _ in range(k):
        m = jnp.max(vals, axis=0, keepdims=True)
        first = jnp.min(jnp.where(vals == m, row, 1.0e9), axis=0, keepdims=True)
        top_v.append(m)
        top_i.append(first)
        vals = jnp.where(row == first, -3.0e38, vals)
    return top_v, top_i


def _pair_candidates(s0_all, s1_all, K):
    L, n_tok = s0_all.shape
    sub = lax.broadcasted_iota(I32, (SUBLANES, n_tok), 0).astype(F32)
    vals, ids = [], []
    a = 0
    while a < L and min(K // (a + 1), L) > 1:
        nb = min(K // (a + 1), L)
        for b0 in range(0, nb, SUBLANES):
            v = s0_all[a:a + 1] + s1_all[b0:b0 + SUBLANES]
            vals.append(v if b0 + SUBLANES <= nb else jnp.where(sub < float(nb - b0), v, -3.0e38))
            ids.append(sub + float(a * K + b0))
        a += 1
    while a < L:
        vals.append(s0_all[a:a + SUBLANES] + s1_all[0:1])
        ids.append((sub + float(a)) * float(K))
        a += SUBLANES
    return jnp.concatenate(vals, axis=0), jnp.concatenate(ids, axis=0)


def _route_from_scores(sc0, sc1):
    K = PEER_TOPK
    n_tok = sc0.shape[1]
    s0, i0 = _top_rows(sc0, K)
    s1, i1 = _top_rows(sc1, K)
    s0_all = jnp.concatenate(s0, axis=0)
    s1_all = jnp.concatenate(s1, axis=0)
    i0_all = jnp.concatenate(i0, axis=0)
    i1_all = jnp.concatenate(i1, axis=0)
    cand, cand_id = _pair_candidates(s0_all, s1_all, K)
    best_s, pos = _top_rows(cand, K, cand_id)
    best = jnp.concatenate(best_s, axis=0)
    m = jnp.max(best, axis=0, keepdims=True)
    e = jnp.exp(best - m)
    gate = e / jnp.sum(e, axis=0, keepdims=True)
    rank = lax.broadcasted_iota(I32, (K, n_tok), 0).astype(F32)
    ei, ej = [], []
    for kk in range(K):
        a = jnp.floor(pos[kk] * (1.0 / K))
        b = pos[kk] - a * K
        ei.append(jnp.sum(jnp.where(rank == a, i0_all, 0.0), axis=0, keepdims=True))
        ej.append(jnp.sum(jnp.where(rank == b, i1_all, 0.0), axis=0, keepdims=True))
    return jnp.concatenate(ei, axis=0), jnp.concatenate(ej, axis=0), gate


def _route_head(qt_scr, sk_ref, hd):
    half_dim = sk_ref.shape[2]
    scores = []
    for p in range(2):
        off = pl.multiple_of((hd * 2 + p) * half_dim, half_dim)
        qs = qt_scr[pl.ds(off, half_dim), :].astype(BF16)
        scores.append(_dot(sk_ref[hd * 2 + p], qs))
    return scores


def _route_kernel(h_ref, g_ref, wq_ref, sk_ref, hn_ref, i_ref, j_ref, gate_ref,
                  qt_scr, i_scr, j_scr, gate_scr):
    K = PEER_TOPK
    hn = _rms(h_ref[...], g_ref[...]).astype(BF16)
    hn_ref[...] = hn
    qt_scr[...] = _dot_nt(wq_ref[...], hn)

    def head(hd, _):
        sc0, sc1 = _route_head(qt_scr, sk_ref, hd)
        row0 = pl.multiple_of(hd * K, K)
        ei, ej, gate = _route_from_scores(sc0, sc1)
        i_scr[pl.ds(row0, K), :] = ei
        j_scr[pl.ds(row0, K), :] = ej
        gate_scr[pl.ds(row0, K), :] = gate
        return 0

    lax.fori_loop(0, PEER_HEADS, head, 0)
    i_ref[...] = i_scr[...].T.astype(I32)
    j_ref[...] = j_scr[...].T.astype(I32)
    gate_ref[...] = gate_scr[...].T


def _route(h, g_peer, wq_t, sub_keys):
    T, D = h.shape
    tt = ROUTE_TOKENS
    n_sel = PEER_HEADS * PEER_TOPK
    tok = lambda i: (i, 0)
    return pl.pallas_call(
        _route_kernel,
        grid=(T // tt,),
        in_specs=[
            pl.BlockSpec((tt, D), tok),
            pl.BlockSpec((1, D), lambda i: (0, 0)),
            pl.BlockSpec(wq_t.shape, lambda i: (0, 0)),
            pl.BlockSpec(sub_keys.shape, lambda i: (0, 0, 0)),
        ],
        out_specs=(
            pl.BlockSpec((tt, D), tok),
            pl.BlockSpec((tt, n_sel), tok),
            pl.BlockSpec((tt, n_sel), tok),
            pl.BlockSpec((tt, n_sel), tok),
        ),
        out_shape=(
            jax.ShapeDtypeStruct((T, D), BF16),
            jax.ShapeDtypeStruct((T, n_sel), I32),
            jax.ShapeDtypeStruct((T, n_sel), I32),
            jax.ShapeDtypeStruct((T, n_sel), F32),
        ),
        scratch_shapes=[
            pltpu.VMEM((wq_t.shape[0], tt), F32),
            pltpu.VMEM((n_sel, tt), F32),
            pltpu.VMEM((n_sel, tt), F32),
            pltpu.VMEM((n_sel, tt), F32),
        ],
        compiler_params=_params(1),
        name="route",
    )(h, g_peer.reshape(1, D), wq_t, sub_keys)


def _peer_up_kernel(hn_ref, u_ref, i_ref, j_ref, a_ref):
    e = pl.program_id(1)

    @pl.when(e == 0)
    def _():
        a_ref[...] = jnp.zeros(a_ref.shape, F32)

    hn = hn_ref[...]
    ii = i_ref[...]
    jj = j_ref[...]
    acc = a_ref[...]
    for r0 in range(0, EXPERT_ROWS, 2):
        z = _dot_nt(hn, u_ref[r0 * PEER_KEYS:(r0 + 2) * PEER_KEYS, :])
        for r in range(r0, r0 + 2):
            picked = jnp.take_along_axis(
                z[:, (r - r0) * PEER_KEYS:(r - r0 + 1) * PEER_KEYS], jj, axis=1)
            acc = jnp.where(ii == e * EXPERT_ROWS + r, picked, acc)
    a_ref[...] = acc


def _peer_up(hn, u_bf, i_idx, j_idx):
    T, D = hn.shape
    tt = UP_TOKENS
    n_sel = i_idx.shape[1]
    rows = EXPERT_ROWS * PEER_KEYS
    tok = lambda t, e: (t, 0)
    return pl.pallas_call(
        _peer_up_kernel,
        grid=(T // tt, u_bf.shape[0] // rows),
        in_specs=[
            pl.BlockSpec((tt, D), tok),
            pl.BlockSpec((rows, D), lambda t, e: (e, 0)),
            pl.BlockSpec((tt, n_sel), tok),
            pl.BlockSpec((tt, n_sel), tok),
        ],
        out_specs=pl.BlockSpec((tt, n_sel), tok),
        out_shape=jax.ShapeDtypeStruct((T, n_sel), F32),
        compiler_params=_params(2),
        name="peer_up",
    )(hn, u_bf, i_idx, j_idx)


def _peer_down_kernel(a_ref, gate_ref, i_ref, j_ref, v_ref, h_ref, g_ref, o_ref,
                      w_scr, coef_scr, acc_scr):
    e = pl.program_id(1)
    n_tok = a_ref.shape[0]
    nk = PEER_KEYS

    @pl.when(e == 0)
    def _():
        coef_scr[...] = gate_ref[...] * _gelu(a_ref[...])
        acc_scr[...] = jnp.zeros(acc_scr.shape, F32)
        sub = lax.broadcasted_iota(I32, (nk, i_ref.shape[1]), 0)

        def expand(t, _):
            ii = i_ref[pl.ds(t, 1), :]
            jj = j_ref[pl.ds(t, 1), :]
            cc = coef_scr[pl.ds(t, 1), :]
            a = jnp.where(sub == ii, cc, 0.0).astype(BF16)
            bt = jnp.where(sub == jj, 1.0, 0.0).astype(BF16)
            base = pl.multiple_of(t * W_ROW_STRIDE, SUBLANES)
            w_scr[pl.ds(base, nk), :] = _dot_nt(a, bt)
            return 0

        lax.fori_loop(0, n_tok, expand, 0, unroll=EXPAND_UNROLL)

    parts = []
    for r in range(DOWN_EXPERT_ROWS):
        row = e * DOWN_EXPERT_ROWS + r
        parts.append(w_scr[pl.ds(row, n_tok, stride=W_ROW_STRIDE), :].astype(BF16))
    lhs = jnp.concatenate(parts, axis=1)
    acc_scr[...] += _dot(lhs, v_ref[...])

    @pl.when(e == pl.num_programs(1) - 1)
    def _():
        o_ref[...] = _rms(h_ref[...] + acc_scr[...], g_ref[...])


def _peer_down(a_pre, gate, i_idx, j_idx, v_bf, h, g_final):
    T, D = h.shape
    tt = DOWN_TOKENS
    n_sel = i_idx.shape[1]
    rows = DOWN_EXPERT_ROWS * PEER_KEYS
    tok = lambda t, e: (t, 0)
    sel_spec = pl.BlockSpec((tt, n_sel), tok)
    return pl.pallas_call(
        _peer_down_kernel,
        grid=(T // tt, v_bf.shape[0] // rows),
        in_specs=[
            sel_spec, sel_spec, sel_spec, sel_spec,
            pl.BlockSpec((rows, D), lambda t, e: (e, 0)),
            pl.BlockSpec((tt, D), tok),
            pl.BlockSpec((1, D), lambda t, e: (0, 0)),
        ],
        out_specs=pl.BlockSpec((tt, D), tok),
        out_shape=jax.ShapeDtypeStruct((T, D), F32),
        scratch_shapes=[
            pltpu.VMEM((tt * W_ROW_STRIDE, PEER_KEYS), F32),
            pltpu.VMEM((tt, n_sel), F32),
            pltpu.VMEM((tt, D), F32),
        ],
        compiler_params=_params(2),
        name="peer_down",
    )(a_pre, gate, i_idx, j_idx, v_bf, h, g_final.reshape(1, D))


def _overlap_table(n_cmp, n_slc):
    c0 = jnp.arange(n_cmp) * CMP_STRIDE
    j0 = jnp.arange(n_slc) * SLC_BLOCK
    ov = jnp.clip(jnp.minimum(c0[:, None] + CMP_BLOCK, j0[None, :] + SLC_BLOCK)
                  - jnp.maximum(c0[:, None], j0[None, :]), 0, None)
    return ov.astype(F32) / CMP_BLOCK


def kernel(x, mem, g_mix, w_in, cmp_pe_k, cmp_w1_k, cmp_b1_k, cmp_w2_k, cmp_pe_v, cmp_w1_v, cmp_b1_v, cmp_w2_v, gmlp_ln_g, gmlp_ln_b, gmlp_ws, gmlp_bs, w_nsa_out, w_gmlp_out, w_mix_out, g_xattn, g_mem, w_xq, w_xkv, w_xo, g_peer, w_peer_q, peer_sub_keys, peer_u, peer_v, g_final):
    B, S, D = x.shape
    T = B * S
    G, R, dk = NSA_GROUPS, NSA_REP, NSA_HEAD_DIM
    x2 = x.reshape(T, D)

    n_gate = NSA_HEADS * 3
    gate_cols = w_in[:, 1280:1280 + n_gate]
    per_g = n_gate // G
    gate_blocks = [jnp.pad(gate_cols[:, g * per_g:(g + 1) * per_g], ((0, 0), (0, LANES - per_g)))
                   for g in range(G)]
    w_re = jnp.concatenate([w_in[:, :1280]] + gate_blocks + [w_in[:, 1280 + n_gate:]],
                           axis=1).astype(BF16)
    half = CMP_STRIDE * dk
    pe = jnp.stack([cmp_pe_k.reshape(2, half), cmp_pe_v.reshape(2, half)])
    w1 = jnp.stack([cmp_w1_k, cmp_w1_v]).astype(BF16)
    b1 = jnp.stack([cmp_b1_k, cmp_b1_v])[:, None, :]
    w2 = jnp.stack([cmp_w2_k, cmp_w2_v]).astype(BF16)
    C = GMLP_CHUNK
    ws_causal = (gmlp_ws * jnp.tril(jnp.ones((C, C), F32))).astype(BF16)
    group_dim = gmlp_ln_g.shape[0] // GMLP_GROUPS
    bs_b = jnp.broadcast_to(gmlp_bs[:, :, None], (GMLP_GROUPS, C, group_dim))
    n_chunks = S // CMP_STRIDE
    overlap = _overlap_table(n_chunks, S // SLC_BLOCK)

    q, kvc, ksa, vsa, kw, vw, gates = _in_proj(x2, g_mix, w_re[:, :_C_U], B, S)
    cmp_kv = _compress(kvc.reshape(2, B * G, n_chunks, half), pe, w1, b1, w2)
    cmp_kv = cmp_kv.reshape(2, B, G, n_chunks, dk)
    o_nsa = _nsa(q, ksa, vsa, kw, vw, cmp_kv[0], cmp_kv[1], gates, overlap.T.astype(BF16))
    k_mem, v_mem = _mem_kv(mem, g_mem, w_xkv.astype(BF16))
    h = _mix(x2, g_mix, w_re[:, _C_U:], o_nsa.reshape(T, G * R * dk), gmlp_ln_g, gmlp_ln_b,
             ws_causal, bs_b, w_nsa_out.astype(BF16), w_gmlp_out.astype(BF16),
             w_mix_out.astype(BF16), g_xattn, w_xq.astype(BF16), k_mem, v_mem,
             w_xo.astype(BF16), S)

    half_dim = peer_sub_keys.shape[3]
    sk = peer_sub_keys.reshape(PEER_HEADS * 2, PEER_KEYS, half_dim).astype(BF16)
    hn, i_idx, j_idx, gate = _route(h, g_peer, w_peer_q.T.astype(BF16), sk)
    a_pre = _peer_up(hn, peer_u.astype(BF16), i_idx, j_idx)
    out = _peer_down(a_pre, gate, i_idx, j_idx, peer_v.astype(BF16), h, g_final)
    return out.reshape(B, S, D)
```

```python
import functools

import jax
import jax.numpy as jnp
from jax import lax
from jax.experimental import pallas as pl
from jax.experimental.pallas import tpu as pltpu

F32 = jnp.float32
BF16 = jnp.bfloat16
I32 = jnp.int32

NORM_EPS = 1e-6
NEG_INF = -1e30
TINY = 1e-30
LOG2_E = 1.4426950408889634
MASK_MARGIN = 300.0

NSA_HEADS = 8
NSA_HEAD_DIM = 64
NSA_GROUPS = 2
NSA_REP = NSA_HEADS // NSA_GROUPS
CMP_BLOCK = 32
CMP_STRIDE = 16
SLC_BLOCK = 64
SLC_TOPN = 16
WINDOW = 512
FORCE_BONUS = 1e4
GMLP_GROUPS = 4
GMLP_CHUNK = 128
XATTN_HEADS = 4
PEER_HEADS = 8
PEER_KEYS = 128
PEER_TOPK = 16

LANES = 128
SUBLANES = 8
VMEM_LIMIT_BYTES = 56 * 1024 * 1024

PROJ_ROWS = 512
MIX_ROWS = 512
NSA_QUERIES = 256
SEL_KEYS = 1024
NSA_SPAN = 2048
ROUTE_TOKENS = 512
UP_TOKENS = 1024
DOWN_TOKENS = 512
EXPERT_ROWS = 32
DOWN_EXPERT_ROWS = 8
W_ROW_STRIDE = PEER_KEYS + SUBLANES
EXPAND_UNROLL = 64


def _params(n_axes):
    return pltpu.CompilerParams(
        dimension_semantics=("arbitrary",) * n_axes,
        vmem_limit_bytes=VMEM_LIMIT_BYTES,
    )


def _rms(x, g):
    return x * lax.rsqrt(jnp.mean(x * x, axis=-1, keepdims=True) + NORM_EPS) * g


def _gelu(x):
    return 0.5 * x * (1.0 + lax.erf(x * 0.7071067811865476))


def _sigmoid(x):
    return 1.0 / (1.0 + jnp.exp(-x))


def _dot(a, b):
    return jnp.dot(a, b, preferred_element_type=F32)


def _dot_nt(a, b):
    return lax.dot_general(a, b, (((1,), (1,)), ((), ())), preferred_element_type=F32)


_C_Q = 0
_C_KV = 512
_C_GATE = 1280
_C_U = 1536


def _in_proj_kernel(x_ref, g_ref, w_ref, q_ref, kvc_ref, ks_ref, vs_ref, kw_ref, vw_ref,
                    gate_ref, *, seq_len):
    hn = _rms(x_ref[...], g_ref[...]).astype(BF16)

    def proj(a, b):
        return _dot(hn, w_ref[:, a:b])

    dk = NSA_HEAD_DIM
    pq = proj(_C_Q, _C_KV) * (dk ** -0.5 * LOG2_E)
    for g in range(NSA_GROUPS):
        for r in range(NSA_REP):
            c = (g * NSA_REP + r) * dk
            q_ref[0, g, r] = pq[:, c:c + dk].astype(BF16)
    pk = proj(_C_KV, _C_GATE)
    tm = x_ref.shape[0]
    blk_w = ks_ref.shape[3] - 2 * dk
    pos = (pl.program_id(0) % (seq_len // tm)) * tm + lax.broadcasted_iota(I32, (tm, blk_w), 0)
    own_block = jnp.where(pos // SLC_BLOCK == lax.broadcasted_iota(I32, (tm, blk_w), 1),
                          -1.0, 0.0).astype(BF16)
    ones_col = jnp.where(lax.broadcasted_iota(I32, (tm, LANES - dk), 1) == 0, 1.0, 0.0).astype(BF16)
    for g in range(NSA_GROUPS):
        kvc_ref[0, 0, g] = pk[:, 0 * 128 + g * dk:0 * 128 + (g + 1) * dk]
        kvc_ref[1, 0, g] = pk[:, 1 * 128 + g * dk:1 * 128 + (g + 1) * dk]
        ks_ref[0, g, :, :blk_w] = own_block
        ks_ref[0, g, :, blk_w:blk_w + dk] = (
            pk[:, 2 * 128 + g * dk:2 * 128 + (g + 1) * dk].astype(BF16))
        ks_ref[0, g, :, blk_w + dk:] = jnp.zeros((tm, dk), BF16)
        vs_ref[0, g, :, :dk] = pk[:, 3 * 128 + g * dk:3 * 128 + (g + 1) * dk].astype(BF16)
        vs_ref[0, g, :, dk:] = ones_col
        kw_ref[0, g] = pk[:, 4 * 128 + g * dk:4 * 128 + (g + 1) * dk].astype(BF16)
        vw_ref[0, g] = pk[:, 5 * 128 + g * dk:5 * 128 + (g + 1) * dk].astype(BF16)
        gate_ref[0, g] = _sigmoid(proj(_C_GATE + g * 128, _C_GATE + (g + 1) * 128))


def _in_proj(x2, g_mix, w_re, B, S):
    T, D = x2.shape
    tm = PROJ_ROWS
    per_b = S // tm
    G, R, dk = NSA_GROUPS, NSA_REP, NSA_HEAD_DIM
    blk_w = -(-(S // SLC_BLOCK) // LANES) * LANES

    def tok(i):
        return (i, 0)

    def bgs(i):
        return (i // per_b, 0, i % per_b, 0)

    out_shape = (
        jax.ShapeDtypeStruct((B, G, R, S, dk), BF16),
        jax.ShapeDtypeStruct((2, B, G, S, dk), F32),
        jax.ShapeDtypeStruct((B, G, S, blk_w + 2 * dk), BF16),
        jax.ShapeDtypeStruct((B, G, S, LANES), BF16),
        jax.ShapeDtypeStruct((B, G, S, dk), BF16),
        jax.ShapeDtypeStruct((B, G, S, dk), BF16),
        jax.ShapeDtypeStruct((B, G, S, LANES), F32),
    )
    kv_spec = pl.BlockSpec((1, G, tm, dk), bgs)
    out_specs = (
        pl.BlockSpec((1, G, R, tm, dk), lambda i: (i // per_b, 0, 0, i % per_b, 0)),
        pl.BlockSpec((2, 1, G, tm, dk), lambda i: (0, i // per_b, 0, i % per_b, 0)),
        pl.BlockSpec((1, G, tm, blk_w + 2 * dk), bgs),
        pl.BlockSpec((1, G, tm, LANES), bgs),
        kv_spec, kv_spec,
        pl.BlockSpec((1, G, tm, LANES), bgs),
    )
    return pl.pallas_call(
        functools.partial(_in_proj_kernel, seq_len=S),
        grid=(T // tm,),
        in_specs=[
            pl.BlockSpec((tm, D), tok),
            pl.BlockSpec((1, D), lambda i: (0, 0)),
            pl.BlockSpec((D, _C_U), lambda i: (0, 0)),
        ],
        out_specs=out_specs,
        out_shape=out_shape,
        compiler_params=_params(1),
        name="in_proj",
    )(x2, g_mix.reshape(1, D), w_re)


def _compress_kernel(x_ref, pe_ref, w1_ref, b1_ref, w2_ref, o_ref):
    x = x_ref[0, 0]
    half = x.shape[1]
    lo = (x + pe_ref[0, 0:1, :]).astype(BF16)
    hi = (x + pe_ref[0, 1:2, :]).astype(BF16)
    p = _dot(lo, w1_ref[0, :half, :])
    q = _dot(hi, w1_ref[0, half:, :])
    n = x.shape[0]
    h = p + pltpu.roll(q, n - 1, 0) + b1_ref[0]
    o_ref[0, 0] = _dot(_gelu(h).astype(BF16), w2_ref[0]).astype(BF16)


def _compress(kvc, pe, w1, b1, w2):
    _, BG, n_chunks, width = kvc.shape
    hidden = w1.shape[-1]
    dk = w2.shape[-1]
    return pl.pallas_call(
        _compress_kernel,
        grid=(2, BG),
        in_specs=[
            pl.BlockSpec((1, 1, n_chunks, width), lambda a, b: (a, b, 0, 0)),
            pl.BlockSpec((1, 2, width), lambda a, b: (a, 0, 0)),
            pl.BlockSpec((1, 2 * width, hidden), lambda a, b: (a, 0, 0)),
            pl.BlockSpec((1, 1, hidden), lambda a, b: (a, 0, 0)),
            pl.BlockSpec((1, hidden, dk), lambda a, b: (a, 0, 0)),
        ],
        out_specs=pl.BlockSpec((1, 1, n_chunks, dk), lambda a, b: (a, b, 0, 0)),
        out_shape=jax.ShapeDtypeStruct((2, BG, n_chunks, dk), BF16),
        compiler_params=_params(2),
        name="compress",
    )(kvc, pe, w1, b1, w2)


def _select_blocks(score_t):
    n = score_t.shape[0]
    row = lax.broadcasted_iota(I32, score_t.shape, 0).astype(F32)
    s = score_t
    for _ in range(SLC_TOPN):
        m = jnp.max(s, axis=0, keepdims=True)
        first = jnp.min(jnp.where(s == m, row, float(n)), axis=0, keepdims=True)
        s = jnp.where(row == first, -3.0e38, s)
    return jnp.where((s < -2.0e38) & (score_t > 0.5 * NEG_INF), 1.0, 0.0)


def _nsa_kernel(q_ref, ksa_ref, vsa_ref, kw_ref, vw_ref, kc_ref, vc_ref, gate_ref, ovt_ref, o_ref,
                kmax_scr, *, q_base):
    qb = pl.program_id(2)
    q0 = q_base + qb * NSA_QUERIES
    R, Q, dk = NSA_REP, NSA_QUERIES, NSA_HEAD_DIM
    q2 = q_ref[0, 0].reshape(R * Q, dk)

    kc = kc_ref[0, 0]
    n_cmp = kc.shape[0]
    s_c = _dot_nt(q2, kc)
    t_c = q0 + lax.broadcasted_iota(I32, (Q, n_cmp), 0)
    cmp_end = lax.broadcasted_iota(I32, (Q, n_cmp), 1) * CMP_STRIDE + (CMP_BLOCK - 1)
    mask_c = cmp_end <= t_c
    o_c = []
    p_sum = jnp.zeros((Q, n_cmp), F32)
    for r in range(R):
        s_r = jnp.where(mask_c, s_c[r * Q:(r + 1) * Q], NEG_INF)
        e = jnp.where(mask_c, jnp.exp2(s_r - jnp.max(s_r, axis=-1, keepdims=True)), 0.0)
        inv = 1.0 / jnp.maximum(jnp.sum(e, axis=-1, keepdims=True), TINY)
        p_sum = p_sum + e * inv
        o_c.append(_dot(e.astype(BF16), vc_ref[0, 0]) * inv)

    span = WINDOW + Q
    start = pl.multiple_of(jnp.maximum(q0 - WINDOW, 0), Q)
    kw = kw_ref[0, 0, pl.ds(start, span), :]
    vw = vw_ref[0, 0, pl.ds(start, span), :]
    s_w = _dot_nt(q2, kw)
    pos = start + lax.broadcasted_iota(I32, (Q, span), 1)
    diff = q0 + lax.broadcasted_iota(I32, (Q, span), 0) - pos
    mask_w = (diff >= 0) & (diff < WINDOW)
    o_w = []
    for r in range(R):
        s_r = jnp.where(mask_w, s_w[r * Q:(r + 1) * Q], NEG_INF)
        e = jnp.exp2(s_r - jnp.max(s_r, axis=-1, keepdims=True))
        inv = 1.0 / jnp.maximum(jnp.sum(e, axis=-1, keepdims=True), TINY)
        o_w.append(_dot(e.astype(BF16), vw) * inv)

    n_slc = ovt_ref.shape[0]
    ovt = ovt_ref[...]
    p_hi = p_sum.astype(BF16)
    rest = p_sum - p_hi.astype(F32)
    p_mid = rest.astype(BF16)
    p_lo = (rest - p_mid.astype(F32)).astype(BF16)
    imp_t = _dot_nt(ovt, p_hi) + _dot_nt(ovt, p_mid) + _dot_nt(ovt, p_lo)
    t_b = q0 + lax.broadcasted_iota(I32, (n_slc, Q), 1)
    blk = lax.broadcasted_iota(I32, (n_slc, Q), 0)
    allowed = blk * SLC_BLOCK <= t_b
    cur = t_b // SLC_BLOCK
    forced = (blk == 0) | (blk == cur) | (blk == cur - 1)
    score_t = jnp.where(forced & allowed, FORCE_BONUS, jnp.where(allowed, imp_t, NEG_INF))
    sel_t = _select_blocks(score_t)

    tk = SEL_KEYS
    blk_w = ksa_ref.shape[3] - 2 * dk

    @pl.when(qb == 0)
    def _():
        k_all = ksa_ref[0, 0, :, blk_w:blk_w + dk].astype(F32)
        k_sq = jnp.max(jnp.sum(k_all * k_all, axis=-1, keepdims=True), axis=0, keepdims=True)
        kmax_scr[...] = jnp.broadcast_to(jnp.sqrt(k_sq), kmax_scr.shape)

    q_f = q2.astype(F32)
    q_sq = jnp.max(jnp.sum(q_f * q_f, axis=-1, keepdims=True), axis=0, keepdims=True)
    drop = (2.0 * jnp.sqrt(q_sq) * kmax_scr[0:1, 0:1] + MASK_MARGIN) * 1.02
    if blk_w > n_slc:
        sel_t = jnp.concatenate([sel_t, jnp.zeros((blk_w - n_slc, Q), F32)], axis=0)
    mask_cols = ((1.0 - sel_t.T) * drop).astype(BF16)
    q_aug = jnp.concatenate(
        [jnp.concatenate([mask_cols] * R, axis=0), q2, jnp.zeros((R * Q, dk), BF16)], axis=1)
    last = q0 // tk
    va_w = vsa_ref.shape[3]

    def sel_tile(kt, carry, causal):
        ms, accs = carry
        base = pl.multiple_of(kt * tk, tk)
        s = _dot_nt(q_aug, ksa_ref[0, 0, pl.ds(base, tk), :])
        v_t = vsa_ref[0, 0, pl.ds(base, tk), :]
        if causal:
            ahead = (base + lax.broadcasted_iota(I32, (Q, tk), 1)
                     > q0 + lax.broadcasted_iota(I32, (Q, tk), 0))
        new_m, new_acc = [], []
        for r in range(R):
            s_r = s[r * Q:(r + 1) * Q]
            if causal:
                s_r = jnp.where(ahead, NEG_INF, s_r)
            m_new = jnp.maximum(ms[r], jnp.max(s_r, axis=-1, keepdims=True))
            alpha = jnp.exp2(ms[r] - m_new)
            p = jnp.exp2(s_r - m_new)
            new_m.append(m_new)
            new_acc.append(alpha * accs[r] + _dot(p.astype(BF16), v_t))
        return tuple(new_m), tuple(new_acc)

    init = (tuple(jnp.full((Q, 1), NEG_INF, F32) for _ in range(R)),
            tuple(jnp.zeros((Q, va_w), F32) for _ in range(R)))
    carry = lax.fori_loop(0, last, functools.partial(sel_tile, causal=False), init)
    _, accs = sel_tile(last, carry, causal=True)
    o_s = [accs[r][:, :dk] / jnp.maximum(accs[r][:, dk:dk + 1], TINY) for r in range(R)]

    gate = gate_ref[0, 0]
    for r in range(R):
        out = (gate[:, 3 * r:3 * r + 1] * o_c[r] + gate[:, 3 * r + 1:3 * r + 2] * o_s[r]
               + gate[:, 3 * r + 2:3 * r + 3] * o_w[r])
        o_ref[0, :, r * dk:(r + 1) * dk] = out.astype(BF16)


def _nsa_span(q, ksa, vsa, kw, vw, kcmp, vcmp, gates, overlap_t, q_base, q_count):
    B, G, R, S, dk = q.shape
    prefix = q_base + q_count
    n_cmp = -(-(prefix // CMP_STRIDE) // LANES) * LANES
    n_cmp = min(n_cmp, kcmp.shape[2])
    n_slc = prefix // SLC_BLOCK
    first = q_base // NSA_QUERIES
    kv_spec = pl.BlockSpec((1, 1, prefix, dk), lambda b, g, i: (b, g, 0, 0))
    cmp_spec = pl.BlockSpec((1, 1, n_cmp, dk), lambda b, g, i: (b, g, 0, 0))
    return pl.pallas_call(
        functools.partial(_nsa_kernel, q_base=q_base),
        grid=(B, G, q_count // NSA_QUERIES),
        in_specs=[
            pl.BlockSpec((1, 1, R, NSA_QUERIES, dk), lambda b, g, i: (b, g, 0, first + i, 0)),
            pl.BlockSpec((1, 1, prefix, ksa.shape[3]), lambda b, g, i: (b, g, 0, 0)),
            pl.BlockSpec((1, 1, prefix, vsa.shape[3]), lambda b, g, i: (b, g, 0, 0)),
            kv_spec, kv_spec,
            cmp_spec, cmp_spec,
            pl.BlockSpec((1, 1, NSA_QUERIES, LANES), lambda b, g, i: (b, g, first + i, 0)),
            pl.BlockSpec((n_slc, n_cmp), lambda b, g, i: (0, 0)),
        ],
        out_specs=pl.BlockSpec((1, NSA_QUERIES, R * dk), lambda b, g, i: (b, i, g)),
        out_shape=jax.ShapeDtypeStruct((B, q_count, G * R * dk), BF16),
        scratch_shapes=[pltpu.VMEM((SUBLANES, LANES), F32)],
        compiler_params=_params(3),
        name="nsa",
    )(q, ksa, vsa, kw, vw, kcmp, vcmp, gates, overlap_t[:n_slc, :n_cmp])


def _nsa(q, ksa, vsa, kw, vw, kcmp, vcmp, gates, overlap_t):
    S = q.shape[3]
    span = min(NSA_SPAN, S)
    outs = [_nsa_span(q, ksa, vsa, kw, vw, kcmp, vcmp, gates, overlap_t, base, span)
            for base in range(0, S, span)]
    return outs[0] if len(outs) == 1 else jnp.concatenate(outs, axis=1)


def _mix_kernel(x_ref, g_ref, wuvm_ref, onsa_ref, lng_ref, lnb_ref, ws_ref, bs_ref,
                wn_ref, wg_ref, wm_ref, gx_ref, wq_ref, k_ref, v_ref, wo_ref, h_ref):
    D = x_ref.shape[1]
    W = lng_ref.shape[1]
    hn = _rms(x_ref[...], g_ref[...]).astype(BF16)
    u = _gelu(_dot(hn, wuvm_ref[:, :W]))
    v = _gelu(_dot(hn, wuvm_ref[:, W:2 * W]))
    mu = jnp.mean(v, axis=-1, keepdims=True)
    var = jnp.mean(jnp.square(v - mu), axis=-1, keepdims=True)
    vn = ((v - mu) * lax.rsqrt(var + NORM_EPS) * lng_ref[...] + lnb_ref[...]).astype(BF16)
    C = GMLP_CHUNK
    gd = vn.shape[1] // GMLP_GROUPS
    rows = []
    for c in range(vn.shape[0] // C):
        cols = []
        for g in range(GMLP_GROUPS):
            cols.append(_dot(ws_ref[g], vn[c * C:(c + 1) * C, g * gd:(g + 1) * gd]) + bs_ref[g])
        rows.append(jnp.concatenate(cols, axis=1))
    o_gmlp = u * jnp.concatenate(rows, axis=0)
    mg = _sigmoid(_dot(hn, wuvm_ref[:, 2 * W:]))
    y = (mg[:, :D] * _dot(onsa_ref[...], wn_ref[...])
         + mg[:, D:] * _dot(o_gmlp.astype(BF16), wg_ref[...]))
    h = x_ref[...] + _dot(y.astype(BF16), wm_ref[...])
    h_ref[...] = _xattn_block(h, gx_ref, wq_ref, k_ref, v_ref, wo_ref)


def _mix(x2, g_mix, w_uvm, o_nsa, ln_g, ln_b, ws_causal, bs_b, w_nsa_out, w_gmlp_out, w_mix_out,
         g_xattn, w_xq, k_mem, v_mem, w_xo, S):
    T, D = x2.shape
    tm = MIX_ROWS
    W = ln_g.shape[0]
    per_b = S // tm
    mem_spec = pl.BlockSpec((1, k_mem.shape[1], D), lambda i: (i // per_b, 0, 0))

    def tok(i):
        return (i, 0)

    def const2(i):
        return (0, 0)

    def const3(i):
        return (0, 0, 0)

    return pl.pallas_call(
        _mix_kernel,
        grid=(T // tm,),
        in_specs=[
            pl.BlockSpec((tm, D), tok),
            pl.BlockSpec((1, D), const2),
            pl.BlockSpec(w_uvm.shape, const2),
            pl.BlockSpec((tm, o_nsa.shape[1]), tok),
            pl.BlockSpec((1, W), const2),
            pl.BlockSpec((1, W), const2),
            pl.BlockSpec(ws_causal.shape, const3),
            pl.BlockSpec(bs_b.shape, const3),
            pl.BlockSpec(w_nsa_out.shape, const2),
            pl.BlockSpec(w_gmlp_out.shape, const2),
            pl.BlockSpec(w_mix_out.shape, const2),
            pl.BlockSpec((1, D), const2),
            pl.BlockSpec((D, D), const2),
            mem_spec, mem_spec,
            pl.BlockSpec((D, D), const2),
        ],
        out_specs=pl.BlockSpec((tm, D), tok),
        out_shape=jax.ShapeDtypeStruct((T, D), F32),
        compiler_params=_params(1),
        name="mix",
    )(x2, g_mix.reshape(1, D), w_uvm, o_nsa, ln_g.reshape(1, W), ln_b.reshape(1, W), ws_causal,
      bs_b, w_nsa_out, w_gmlp_out, w_mix_out, g_xattn.reshape(1, D), w_xq, k_mem, v_mem, w_xo)


def _mem_kv_kernel(mem_ref, g_ref, w_ref, k_ref, v_ref):
    D = mem_ref.shape[2]
    mn = _rms(mem_ref[0], g_ref[...]).astype(BF16)
    kv = _dot(mn, w_ref[...])
    k_ref[0] = kv[:, :D].astype(BF16)
    v_ref[0] = kv[:, D:].astype(BF16)


def _mem_kv(mem, g_mem, w_xkv):
    B, M, D = mem.shape
    spec = pl.BlockSpec((1, M, D), lambda b: (b, 0, 0))
    return pl.pallas_call(
        _mem_kv_kernel,
        grid=(B,),
        in_specs=[spec, pl.BlockSpec((1, D), lambda b: (0, 0)),
                  pl.BlockSpec((D, 2 * D), lambda b: (0, 0))],
        out_specs=(spec, spec),
        out_shape=(jax.ShapeDtypeStruct((B, M, D), BF16), jax.ShapeDtypeStruct((B, M, D), BF16)),
        compiler_params=_params(1),
        name="mem_kv",
    )(mem, g_mem.reshape(1, D), w_xkv)


def _xattn_block(h, g_ref, wq_ref, k_ref, v_ref, wo_ref):
    D = h.shape[1]
    hd = D // XATTN_HEADS
    hn = _rms(h, g_ref[...]).astype(BF16)
    q = _dot(hn, wq_ref[...]) * (hd ** -0.5)
    outs = []
    for a in range(XATTN_HEADS):
        s = _dot_nt(q[:, a * hd:(a + 1) * hd].astype(BF16), k_ref[0, :, a * hd:(a + 1) * hd])
        m = jnp.max(s, axis=-1, keepdims=True)
        e = jnp.exp(s - m)
        p = e / jnp.sum(e, axis=-1, keepdims=True)
        outs.append(_dot(p.astype(BF16), v_ref[0, :, a * hd:(a + 1) * hd]))
    o = jnp.concatenate(outs, axis=1).astype(BF16)
    return h + _dot(o, wo_ref[...])


def _top_rows(vals, k, row=None):
    if row is None:
        row = lax.broadcasted_iota(I32, vals.shape, 0).astype(F32)
    top_v, top_i = [], []
    for _ in range(k):
        m = jnp.max(vals, axis=0, keepdims=True)
        first = jnp.min(jnp.where(vals == m, row, 1.0e9), axis=0, keepdims=True)
        top_v.append(m)
        top_i.append(first)
        vals = jnp.where(row == first, -3.0e38, vals)
    return top_v, top_i


def _pair_candidates(s0_all, s1_all, K):
    L, n_tok = s0_all.shape
    sub = lax.broadcasted_iota(I32, (SUBLANES, n_tok), 0).astype(F32)
    vals, ids = [], []
    a = 0
    while a < L and min(K // (a + 1), L) > 1:
        nb = min(K // (a + 1), L)
        for b0 in range(0, nb, SUBLANES):
            v = s0_all[a:a + 1] + s1_all[b0:b0 + SUBLANES]
            vals.append(v if b0 + SUBLANES <= nb else jnp.where(sub < float(nb - b0), v, -3.0e38))
            ids.append(sub + float(a * K + b0))
        a += 1
    while a < L:
        vals.append(s0_all[a:a + SUBLANES] + s1_all[0:1])
        ids.append((sub + float(a)) * float(K))
        a += SUBLANES
    return jnp.concatenate(vals, axis=0), jnp.concatenate(ids, axis=0)


def _route_from_scores(sc0, sc1):
    K = PEER_TOPK
    n_tok = sc0.shape[1]
    s0, i0 = _top_rows(sc0, K)
    s1, i1 = _top_rows(sc1, K)
    s0_all = jnp.concatenate(s0, axis=0)
    s1_all = jnp.concatenate(s1, axis=0)
    i0_all = jnp.concatenate(i0, axis=0)
    i1_all = jnp.concatenate(i1, axis=0)
    cand, cand_id = _pair_candidates(s0_all, s1_all, K)
    best_s, pos = _top_rows(cand, K, cand_id)
    best = jnp.concatenate(best_s, axis=0)
    m = jnp.max(best, axis=0, keepdims=True)
    e = jnp.exp(best - m)
    gate = e / jnp.sum(e, axis=0, keepdims=True)
    rank = lax.broadcasted_iota(I32, (K, n_tok), 0).astype(F32)
    ei, ej = [], []
    for kk in range(K):
        a = jnp.floor(pos[kk] * (1.0 / K))
        b = pos[kk] - a * K
        ei.append(jnp.sum(jnp.where(rank == a, i0_all, 0.0), axis=0, keepdims=True))
        ej.append(jnp.sum(jnp.where(rank == b, i1_all, 0.0), axis=0, keepdims=True))
    return jnp.concatenate(ei, axis=0), jnp.concatenate(ej, axis=0), gate


def _route_head(qt_scr, sk_ref, hd):
    half_dim = sk_ref.shape[2]
    scores = []
    for p in range(2):
        off = pl.multiple_of((hd * 2 + p) * half_dim, half_dim)
        qs = qt_scr[pl.ds(off, half_dim), :].astype(BF16)
        scores.append(_dot(sk_ref[hd * 2 + p], qs))
    return scores


def _route_kernel(h_ref, g_ref, wq_ref, sk_ref, hn_ref, i_ref, j_ref, gate_ref,
                  qt_scr, i_scr, j_scr, gate_scr):
    K = PEER_TOPK
    hn = _rms(h_ref[...], g_ref[...]).astype(BF16)
    hn_ref[...] = hn
    qt_scr[...] = _dot_nt(wq_ref[...], hn)

    def head(hd, _):
        sc0, sc1 = _route_head(qt_scr, sk_ref, hd)
        row0 = pl.multiple_of(hd * K, K)
        ei, ej, gate = _route_from_scores(sc0, sc1)
        i_scr[pl.ds(row0, K), :] = ei
        j_scr[pl.ds(row0, K), :] = ej
        gate_scr[pl.ds(row0, K), :] = gate
        return 0

    lax.fori_loop(0, PEER_HEADS, head, 0)
    i_ref[...] = i_scr[...].T.astype(I32)
    j_ref[...] = j_scr[...].T.astype(I32)
    gate_ref[...] = gate_scr[...].T


def _route(h, g_peer, wq_t, sub_keys):
    T, D = h.shape
    tt = ROUTE_TOKENS
    n_sel = PEER_HEADS * PEER_TOPK
    tok = lambda i: (i, 0)
    return pl.pallas_call(
        _route_kernel,
        grid=(T // tt,),
        in_specs=[
            pl.BlockSpec((tt, D), tok),
            pl.BlockSpec((1, D), lambda i: (0, 0)),
            pl.BlockSpec(wq_t.shape, lambda i: (0, 0)),
            pl.BlockSpec(sub_keys.shape, lambda i: (0, 0, 0)),
        ],
        out_specs=(
            pl.BlockSpec((tt, D), tok),
            pl.BlockSpec((tt, n_sel), tok),
            pl.BlockSpec((tt, n_sel), tok),
            pl.BlockSpec((tt, n_sel), tok),
        ),
        out_shape=(
            jax.ShapeDtypeStruct((T, D), BF16),
            jax.ShapeDtypeStruct((T, n_sel), I32),
            jax.ShapeDtypeStruct((T, n_sel), I32),
            jax.ShapeDtypeStruct((T, n_sel), F32),
        ),
        scratch_shapes=[
            pltpu.VMEM((wq_t.shape[0], tt), F32),
            pltpu.VMEM((n_sel, tt), F32),
            pltpu.VMEM((n_sel, tt), F32),
            pltpu.VMEM((n_sel, tt), F32),
        ],
        compiler_params=_params(1),
        name="route",
    )(h, g_peer.reshape(1, D), wq_t, sub_keys)


def _peer_up_kernel(hn_ref, u_ref, i_ref, j_ref, a_ref):
    e = pl.program_id(1)

    @pl.when(e == 0)
    def _():
        a_ref[...] = jnp.zeros(a_ref.shape, F32)

    hn = hn_ref[...]
    ii = i_ref[...]
    jj = j_ref[...]
    acc = a_ref[...]
    for r0 in range(0, EXPERT_ROWS, 2):
        z = _dot_nt(hn, u_ref[r0 * PEER_KEYS:(r0 + 2) * PEER_KEYS, :])
        for r in range(r0, r0 + 2):
            picked = jnp.take_along_axis(
                z[:, (r - r0) * PEER_KEYS:(r - r0 + 1) * PEER_KEYS], jj, axis=1)
            acc = jnp.where(ii == e * EXPERT_ROWS + r, picked, acc)
    a_ref[...] = acc


def _peer_up(hn, u_bf, i_idx, j_idx):
    T, D = hn.shape
    tt = UP_TOKENS
    n_sel = i_idx.shape[1]
    rows = EXPERT_ROWS * PEER_KEYS
    tok = lambda t, e: (t, 0)
    return pl.pallas_call(
        _peer_up_kernel,
        grid=(T // tt, u_bf.shape[0] // rows),
        in_specs=[
            pl.BlockSpec((tt, D), tok),
            pl.BlockSpec((rows, D), lambda t, e: (e, 0)),
            pl.BlockSpec((tt, n_sel), tok),
            pl.BlockSpec((tt, n_sel), tok),
        ],
        out_specs=pl.BlockSpec((tt, n_sel), tok),
        out_shape=jax.ShapeDtypeStruct((T, n_sel), F32),
        compiler_params=_params(2),
        name="peer_up",
    )(hn, u_bf, i_idx, j_idx)


def _peer_down_kernel(a_ref, gate_ref, i_ref, j_ref, v_ref, h_ref, g_ref, o_ref,
                      w_scr, coef_scr, acc_scr):
    e = pl.program_id(1)
    n_tok = a_ref.shape[0]
    nk = PEER_KEYS

    @pl.when(e == 0)
    def _():
        coef_scr[...] = gate_ref[...] * _gelu(a_ref[...])
        acc_scr[...] = jnp.zeros(acc_scr.shape, F32)
        sub = lax.broadcasted_iota(I32, (nk, i_ref.shape[1]), 0)

        def expand(t, _):
            ii = i_ref[pl.ds(t, 1), :]
            jj = j_ref[pl.ds(t, 1), :]
            cc = coef_scr[pl.ds(t, 1), :]
            a = jnp.where(sub == ii, cc, 0.0).astype(BF16)
            bt = jnp.where(sub == jj, 1.0, 0.0).astype(BF16)
            base = pl.multiple_of(t * W_ROW_STRIDE, SUBLANES)
            w_scr[pl.ds(base, nk), :] = _dot_nt(a, bt)
            return 0

        lax.fori_loop(0, n_tok, expand, 0, unroll=EXPAND_UNROLL)

    parts = []
    for r in range(DOWN_EXPERT_ROWS):
        row = e * DOWN_EXPERT_ROWS + r
        parts.append(w_scr[pl.ds(row, n_tok, stride=W_ROW_STRIDE), :].astype(BF16))
    lhs = jnp.concatenate(parts, axis=1)
    acc_scr[...] += _dot(lhs, v_ref[...])

    @pl.when(e == pl.num_programs(1) - 1)
    def _():
        o_ref[...] = _rms(h_ref[...] + acc_scr[...], g_ref[...])


def _peer_down(a_pre, gate, i_idx, j_idx, v_bf, h, g_final):
    T, D = h.shape
    tt = DOWN_TOKENS
    n_sel = i_idx.shape[1]
    rows = DOWN_EXPERT_ROWS * PEER_KEYS
    tok = lambda t, e: (t, 0)
    sel_spec = pl.BlockSpec((tt, n_sel), tok)
    return pl.pallas_call(
        _peer_down_kernel,
        grid=(T // tt, v_bf.shape[0] // rows),
        in_specs=[
            sel_spec, sel_spec, sel_spec, sel_spec,
            pl.BlockSpec((rows, D), lambda t, e: (e, 0)),
            pl.BlockSpec((tt, D), tok),
            pl.BlockSpec((1, D), lambda t, e: (0, 0)),
        ],
        out_specs=pl.BlockSpec((tt, D), tok),
        out_shape=jax.ShapeDtypeStruct((T, D), F32),
        scratch_shapes=[
            pltpu.VMEM((tt * W_ROW_STRIDE, PEER_KEYS), F32),
            pltpu.VMEM((tt, n_sel), F32),
            pltpu.VMEM((tt, D), F32),
        ],
        compiler_params=_params(2),
        name="peer_down",
    )(a_pre, gate, i_idx, j_idx, v_bf, h, g_final.reshape(1, D))


def _overlap_table(n_cmp, n_slc):
    c0 = jnp.arange(n_cmp) * CMP_STRIDE
    j0 = jnp.arange(n_slc) * SLC_BLOCK
    ov = jnp.clip(jnp.minimum(c0[:, None] + CMP_BLOCK, j0[None, :] + SLC_BLOCK)
                  - jnp.maximum(c0[:, None], j0[None, :]), 0, None)
    return ov.astype(F32) / CMP_BLOCK


def kernel(x, mem, g_mix, w_in, cmp_pe_k, cmp_w1_k, cmp_b1_k, cmp_w2_k, cmp_pe_v, cmp_w1_v, cmp_b1_v, cmp_w2_v, gmlp_ln_g, gmlp_ln_b, gmlp_ws, gmlp_bs, w_nsa_out, w_gmlp_out, w_mix_out, g_xattn, g_mem, w_xq, w_xkv, w_xo, g_peer, w_peer_q, peer_sub_keys, peer_u, peer_v, g_final):
    B, S, D = x.shape
    T = B * S
    G, R, dk = NSA_GROUPS, NSA_REP, NSA_HEAD_DIM
    x2 = x.reshape(T, D)

    n_gate = NSA_HEADS * 3
    gate_cols = w_in[:, 1280:1280 + n_gate]
    per_g = n_gate // G
    gate_blocks = [jnp.pad(gate_cols[:, g * per_g:(g + 1) * per_g], ((0, 0), (0, LANES - per_g)))
                   for g in range(G)]
    w_re = jnp.concatenate([w_in[:, :1280]] + gate_blocks + [w_in[:, 1280 + n_gate:]],
                           axis=1).astype(BF16)
    half = CMP_STRIDE * dk
    pe = jnp.stack([cmp_pe_k.reshape(2, half), cmp_pe_v.reshape(2, half)])
    w1 = jnp.stack([cmp_w1_k, cmp_w1_v]).astype(BF16)
    b1 = jnp.stack([cmp_b1_k, cmp_b1_v])[:, None, :]
    w2 = jnp.stack([cmp_w2_k, cmp_w2_v]).astype(BF16)
    C = GMLP_CHUNK
    ws_causal = (gmlp_ws * jnp.tril(jnp.ones((C, C), F32))).astype(BF16)
    group_dim = gmlp_ln_g.shape[0] // GMLP_GROUPS
    bs_b = jnp.broadcast_to(gmlp_bs[:, :, None], (GMLP_GROUPS, C, group_dim))
    n_chunks = S // CMP_STRIDE
    overlap = _overlap_table(n_chunks, S // SLC_BLOCK)

    q, kvc, ksa, vsa, kw, vw, gates = _in_proj(x2, g_mix, w_re[:, :_C_U], B, S)
    cmp_kv = _compress(kvc.reshape(2, B * G, n_chunks, half), pe, w1, b1, w2)
    cmp_kv = cmp_kv.reshape(2, B, G, n_chunks, dk)
    o_nsa = _nsa(q, ksa, vsa, kw, vw, cmp_kv[0], cmp_kv[1], gates, overlap.T.astype(BF16))
    k_mem, v_mem = _mem_kv(mem, g_mem, w_xkv.astype(BF16))
    h = _mix(x2, g_mix, w_re[:, _C_U:], o_nsa.reshape(T, G * R * dk), gmlp_ln_g, gmlp_ln_b,
             ws_causal, bs_b, w_nsa_out.astype(BF16), w_gmlp_out.astype(BF16),
             w_mix_out.astype(BF16), g_xattn, w_xq.astype(BF16), k_mem, v_mem,
             w_xo.astype(BF16), S)

    half_dim = peer_sub_keys.shape[3]
    sk = peer_sub_keys.reshape(PEER_HEADS * 2, PEER_KEYS, half_dim).astype(BF16)
    hn, i_idx, j_idx, gate = _route(h, g_peer, w_peer_q.T.astype(BF16), sk)
    a_pre = _peer_up(hn, peer_u.astype(BF16), i_idx, j_idx)
    out = _peer_down(a_pre, gate, i_idx, j_idx, peer_v.astype(BF16), h, g_final)
    return out.reshape(B, S, D)
```

```python
import functools

import jax
import jax.numpy as jnp
from jax import lax
from jax.experimental import pallas as pl
from jax.experimental.pallas import tpu as pltpu

F32 = jnp.float32
BF16 = jnp.bfloat16
I32 = jnp.int32

NORM_EPS = 1e-6
NEG_INF = -1e30
TINY = 1e-30
LOG2_E = 1.4426950408889634
MASK_MARGIN = 300.0

NSA_HEADS = 8
NSA_HEAD_DIM = 64
NSA_GROUPS = 2
NSA_REP = NSA_HEADS // NSA_GROUPS
CMP_BLOCK = 32
CMP_STRIDE = 16
SLC_BLOCK = 64
SLC_TOPN = 16
WINDOW = 512
FORCE_BONUS = 1e4
GMLP_GROUPS = 4
GMLP_CHUNK = 128
XATTN_HEADS = 4
PEER_HEADS = 8
PEER_KEYS = 128
PEER_TOPK = 16

LANES = 128
SUBLANES = 8
VMEM_LIMIT_BYTES = 56 * 1024 * 1024

PROJ_ROWS = 512
MIX_ROWS = 512
NSA_QUERIES = 256
SEL_KEYS = 1024
NSA_SPAN = 2048
ROUTE_TOKENS = 512
UP_TOKENS = 1024
DOWN_TOKENS = 512
EXPERT_ROWS = 32
DOWN_EXPERT_ROWS = 8
W_ROW_STRIDE = PEER_KEYS + SUBLANES
EXPAND_UNROLL = 64


def _params(n_axes):
    return pltpu.CompilerParams(
        dimension_semantics=("arbitrary",) * n_axes,
        vmem_limit_bytes=VMEM_LIMIT_BYTES,
    )


def _rms(x, g):
    return x * lax.rsqrt(jnp.mean(x * x, axis=-1, keepdims=True) + NORM_EPS) * g


def _gelu(x):
    return 0.5 * x * (1.0 + lax.erf(x * 0.7071067811865476))


def _sigmoid(x):
    return 1.0 / (1.0 + jnp.exp(-x))


def _dot(a, b):
    return jnp.dot(a, b, preferred_element_type=F32)


def _dot_nt(a, b):
    return lax.dot_general(a, b, (((1,), (1,)), ((), ())), preferred_element_type=F32)


_C_Q = 0
_C_KV = 512
_C_GATE = 1280
_C_U = 1536


def _in_proj_kernel(x_ref, g_ref, w_ref, q_ref, kvc_ref, ks_ref, vs_ref, kw_ref, vw_ref,
                    gate_ref, *, seq_len):
    hn = _rms(x_ref[...], g_ref[...]).astype(BF16)

    def proj(a, b):
        return _dot(hn, w_ref[:, a:b])

    dk = NSA_HEAD_DIM
    pq = proj(_C_Q, _C_KV) * (dk ** -0.5 * LOG2_E)
    for g in range(NSA_GROUPS):
        for r in range(NSA_REP):
            c = (g * NSA_REP + r) * dk
            q_ref[0, g, r] = pq[:, c:c + dk].astype(BF16)
    pk = proj(_C_KV, _C_GATE)
    tm = x_ref.shape[0]
    blk_w = ks_ref.shape[3] - 2 * dk
    pos = (pl.program_id(0) % (seq_len // tm)) * tm + lax.broadcasted_iota(I32, (tm, blk_w), 0)
    own_block = jnp.where(pos // SLC_BLOCK == lax.broadcasted_iota(I32, (tm, blk_w), 1),
                          -1.0, 0.0).astype(BF16)
    ones_col = jnp.where(lax.broadcasted_iota(I32, (tm, LANES - dk), 1) == 0, 1.0, 0.0).astype(BF16)
    for g in range(NSA_GROUPS):
        kvc_ref[0, 0, g] = pk[:, 0 * 128 + g * dk:0 * 128 + (g + 1) * dk]
        kvc_ref[1, 0, g] = pk[:, 1 * 128 + g * dk:1 * 128 + (g + 1) * dk]
        ks_ref[0, g, :, :blk_w] = own_block
        ks_ref[0, g, :, blk_w:blk_w + dk] = (
            pk[:, 2 * 128 + g * dk:2 * 128 + (g + 1) * dk].astype(BF16))
        ks_ref[0, g, :, blk_w + dk:] = jnp.zeros((tm, dk), BF16)
        vs_ref[0, g, :, :dk] = pk[:, 3 * 128 + g * dk:3 * 128 + (g + 1) * dk].astype(BF16)
        vs_ref[0, g, :, dk:] = ones_col
        kw_ref[0, g] = pk[:, 4 * 128 + g * dk:4 * 128 + (g + 1) * dk].astype(BF16)
        vw_ref[0, g] = pk[:, 5 * 128 + g * dk:5 * 128 + (g + 1) * dk].astype(BF16)
        gate_ref[0, g] = _sigmoid(proj(_C_GATE + g * 128, _C_GATE + (g + 1) * 128))


def _in_proj(x2, g_mix, w_re, B, S):
    T, D = x2.shape
    tm = PROJ_ROWS
    per_b = S // tm
    G, R, dk = NSA_GROUPS, NSA_REP, NSA_HEAD_DIM
    blk_w = -(-(S // SLC_BLOCK) // LANES) * LANES

    def tok(i):
        return (i, 0)

    def bgs(i):
        return (i // per_b, 0, i % per_b, 0)

    out_shape = (
        jax.ShapeDtypeStruct((B, G, R, S, dk), BF16),
        jax.ShapeDtypeStruct((2, B, G, S, dk), F32),
        jax.ShapeDtypeStruct((B, G, S, blk_w + 2 * dk), BF16),
        jax.ShapeDtypeStruct((B, G, S, LANES), BF16),
        jax.ShapeDtypeStruct((B, G, S, dk), BF16),
        jax.ShapeDtypeStruct((B, G, S, dk), BF16),
        jax.ShapeDtypeStruct((B, G, S, LANES), F32),
    )
    kv_spec = pl.BlockSpec((1, G, tm, dk), bgs)
    out_specs = (
        pl.BlockSpec((1, G, R, tm, dk), lambda i: (i // per_b, 0, 0, i % per_b, 0)),
        pl.BlockSpec((2, 1, G, tm, dk), lambda i: (0, i // per_b, 0, i % per_b, 0)),
        pl.BlockSpec((1, G, tm, blk_w + 2 * dk), bgs),
        pl.BlockSpec((1, G, tm, LANES), bgs),
        kv_spec, kv_spec,
        pl.BlockSpec((1, G, tm, LANES), bgs),
    )
    return pl.pallas_call(
        functools.partial(_in_proj_kernel, seq_len=S),
        grid=(T // tm,),
        in_specs=[
            pl.BlockSpec((tm, D), tok),
            pl.BlockSpec((1, D), lambda i: (0, 0)),
            pl.BlockSpec((D, _C_U), lambda i: (0, 0)),
        ],
        out_specs=out_specs,
        out_shape=out_shape,
        compiler_params=_params(1),
        name="in_proj",
    )(x2, g_mix.reshape(1, D), w_re)


def _compress_kernel(x_ref, pe_ref, w1_ref, b1_ref, w2_ref, o_ref):
    x = x_ref[0, 0]
    half = x.shape[1]
    lo = (x + pe_ref[0, 0:1, :]).astype(BF16)
    hi = (x + pe_ref[0, 1:2, :]).astype(BF16)
    p = _dot(lo, w1_ref[0, :half, :])
    q = _dot(hi, w1_ref[0, half:, :])
    n = x.shape[0]
    h = p + pltpu.roll(q, n - 1, 0) + b1_ref[0]
    o_ref[0, 0] = _dot(_gelu(h).astype(BF16), w2_ref[0]).astype(BF16)


def _compress(kvc, pe, w1, b1, w2):
    _, BG, n_chunks, width = kvc.shape
    hidden = w1.shape[-1]
    dk = w2.shape[-1]
    return pl.pallas_call(
        _compress_kernel,
        grid=(2, BG),
        in_specs=[
            pl.BlockSpec((1, 1, n_chunks, width), lambda a, b: (a, b, 0, 0)),
            pl.BlockSpec((1, 2, width), lambda a, b: (a, 0, 0)),
            pl.BlockSpec((1, 2 * width, hidden), lambda a, b: (a, 0, 0)),
            pl.BlockSpec((1, 1, hidden), lambda a, b: (a, 0, 0)),
            pl.BlockSpec((1, hidden, dk), lambda a, b: (a, 0, 0)),
        ],
        out_specs=pl.BlockSpec((1, 1, n_chunks, dk), lambda a, b: (a, b, 0, 0)),
        out_shape=jax.ShapeDtypeStruct((2, BG, n_chunks, dk), BF16),
        compiler_params=_params(2),
        name="compress",
    )(kvc, pe, w1, b1, w2)


def _select_blocks(score_t):
    n = score_t.shape[0]
    row = lax.broadcasted_iota(I32, score_t.shape, 0).astype(F32)
    s = score_t
    for _ in range(SLC_TOPN):
        m = jnp.max(s, axis=0, keepdims=True)
        first = jnp.min(jnp.where(s == m, row, float(n)), axis=0, keepdims=True)
        s = jnp.where(row == first, -3.0e38, s)
    return jnp.where((s < -2.0e38) & (score_t > 0.5 * NEG_INF), 1.0, 0.0)


def _nsa_kernel(q_ref, ksa_ref, vsa_ref, kw_ref, vw_ref, kc_ref, vc_ref, gate_ref, ovt_ref, o_ref,
                kmax_scr, *, q_base):
    qb = pl.program_id(2)
    q0 = q_base + qb * NSA_QUERIES
    R, Q, dk = NSA_REP, NSA_QUERIES, NSA_HEAD_DIM
    q2 = q_ref[0, 0].reshape(R * Q, dk)

    kc = kc_ref[0, 0]
    n_cmp = kc.shape[0]
    s_c = _dot_nt(q2, kc)
    t_c = q0 + lax.broadcasted_iota(I32, (Q, n_cmp), 0)
    cmp_end = lax.broadcasted_iota(I32, (Q, n_cmp), 1) * CMP_STRIDE + (CMP_BLOCK - 1)
    mask_c = cmp_end <= t_c
    o_c = []
    p_sum = jnp.zeros((Q, n_cmp), F32)
    for r in range(R):
        s_r = jnp.where(mask_c, s_c[r * Q:(r + 1) * Q], NEG_INF)
        e = jnp.where(mask_c, jnp.exp2(s_r - jnp.max(s_r, axis=-1, keepdims=True)), 0.0)
        inv = 1.0 / jnp.maximum(jnp.sum(e, axis=-1, keepdims=True), TINY)
        p_sum = p_sum + e * inv
        o_c.append(_dot(e.astype(BF16), vc_ref[0, 0]) * inv)

    span = WINDOW + Q
    start = pl.multiple_of(jnp.maximum(q0 - WINDOW, 0), Q)
    kw = kw_ref[0, 0, pl.ds(start, span), :]
    vw = vw_ref[0, 0, pl.ds(start, span), :]
    s_w = _dot_nt(q2, kw)
    pos = start + lax.broadcasted_iota(I32, (Q, span), 1)
    diff = q0 + lax.broadcasted_iota(I32, (Q, span), 0) - pos
    mask_w = (diff >= 0) & (diff < WINDOW)
    o_w = []
    for r in range(R):
        s_r = jnp.where(mask_w, s_w[r * Q:(r + 1) * Q], NEG_INF)
        e = jnp.exp2(s_r - jnp.max(s_r, axis=-1, keepdims=True))
        inv = 1.0 / jnp.maximum(jnp.sum(e, axis=-1, keepdims=True), TINY)
        o_w.append(_dot(e.astype(BF16), vw) * inv)

    n_slc = ovt_ref.shape[0]
    ovt = ovt_ref[...]
    p_hi = p_sum.astype(BF16)
    rest = p_sum - p_hi.astype(F32)
    p_mid = rest.astype(BF16)
    p_lo = (rest - p_mid.astype(F32)).astype(BF16)
    imp_t = _dot_nt(ovt, p_hi) + _dot_nt(ovt, p_mid) + _dot_nt(ovt, p_lo)
    t_b = q0 + lax.broadcasted_iota(I32, (n_slc, Q), 1)
    blk = lax.broadcasted_iota(I32, (n_slc, Q), 0)
    allowed = blk * SLC_BLOCK <= t_b
    cur = t_b // SLC_BLOCK
    forced = (blk == 0) | (blk == cur) | (blk == cur - 1)
    score_t = jnp.where(forced & allowed, FORCE_BONUS, jnp.where(allowed, imp_t, NEG_INF))
    sel_t = _select_blocks(score_t)

    tk = SEL_KEYS
    blk_w = ksa_ref.shape[3] - 2 * dk

    @pl.when(qb == 0)
    def _():
        k_all = ksa_ref[0, 0, :, blk_w:blk_w + dk].astype(F32)
        k_sq = jnp.max(jnp.sum(k_all * k_all, axis=-1, keepdims=True), axis=0, keepdims=True)
        kmax_scr[...] = jnp.broadcast_to(jnp.sqrt(k_sq), kmax_scr.shape)

    q_f = q2.astype(F32)
    q_sq = jnp.max(jnp.sum(q_f * q_f, axis=-1, keepdims=True), axis=0, keepdims=True)
    drop = (2.0 * jnp.sqrt(q_sq) * kmax_scr[0:1, 0:1] + MASK_MARGIN) * 1.02
    if blk_w > n_slc:
        sel_t = jnp.concatenate([sel_t, jnp.zeros((blk_w - n_slc, Q), F32)], axis=0)
    mask_cols = ((1.0 - sel_t.T) * drop).astype(BF16)
    q_aug = jnp.concatenate(
        [jnp.concatenate([mask_cols] * R, axis=0), q2, jnp.zeros((R * Q, dk), BF16)], axis=1)
    last = q0 // tk
    va_w = vsa_ref.shape[3]

    def sel_tile(kt, carry, causal):
        ms, accs = carry
        base = pl.multiple_of(kt * tk, tk)
        s = _dot_nt(q_aug, ksa_ref[0, 0, pl.ds(base, tk), :])
        v_t = vsa_ref[0, 0, pl.ds(base, tk), :]
        if causal:
            ahead = (base + lax.broadcasted_iota(I32, (Q, tk), 1)
                     > q0 + lax.broadcasted_iota(I32, (Q, tk), 0))
        new_m, new_acc = [], []
        for r in range(R):
            s_r = s[r * Q:(r + 1) * Q]
            if causal:
                s_r = jnp.where(ahead, NEG_INF, s_r)
            m_new = jnp.maximum(ms[r], jnp.max(s_r, axis=-1, keepdims=True))
            alpha = jnp.exp2(ms[r] - m_new)
            p = jnp.exp2(s_r - m_new)
            new_m.append(m_new)
            new_acc.append(alpha * accs[r] + _dot(p.astype(BF16), v_t))
        return tuple(new_m), tuple(new_acc)

    init = (tuple(jnp.full((Q, 1), NEG_INF, F32) for _ in range(R)),
            tuple(jnp.zeros((Q, va_w), F32) for _ in range(R)))
    carry = lax.fori_loop(0, last, functools.partial(sel_tile, causal=False), init)
    _, accs = sel_tile(last, carry, causal=True)
    o_s = [accs[r][:, :dk] / jnp.maximum(accs[r][:, dk:dk + 1], TINY) for r in range(R)]

    gate = gate_ref[0, 0]
    for r in range(R):
        out = (gate[:, 3 * r:3 * r + 1] * o_c[r] + gate[:, 3 * r + 1:3 * r + 2] * o_s[r]
               + gate[:, 3 * r + 2:3 * r + 3] * o_w[r])
        o_ref[0, :, r * dk:(r + 1) * dk] = out.astype(BF16)


def _nsa_span(q, ksa, vsa, kw, vw, kcmp, vcmp, gates, overlap_t, q_base, q_count, prev):
    B, G, R, S, dk = q.shape
    prefix = q_base + q_count
    n_cmp = -(-(prefix // CMP_STRIDE) // LANES) * LANES
    n_cmp = min(n_cmp, kcmp.shape[2])
    n_slc = prefix // SLC_BLOCK
    first = q_base // NSA_QUERIES
    kv_spec = pl.BlockSpec((1, 1, prefix, dk), lambda b, g, i: (b, g, 0, 0))
    cmp_spec = pl.BlockSpec((1, 1, n_cmp, dk), lambda b, g, i: (b, g, 0, 0))
    in_specs = [
        pl.BlockSpec((1, 1, R, NSA_QUERIES, dk), lambda b, g, i: (b, g, 0, first + i, 0)),
        pl.BlockSpec((1, 1, prefix, ksa.shape[3]), lambda b, g, i: (b, g, 0, 0)),
        pl.BlockSpec((1, 1, prefix, vsa.shape[3]), lambda b, g, i: (b, g, 0, 0)),
        kv_spec, kv_spec,
        cmp_spec, cmp_spec,
        pl.BlockSpec((1, 1, NSA_QUERIES, LANES), lambda b, g, i: (b, g, first + i, 0)),
        pl.BlockSpec((n_slc, n_cmp), lambda b, g, i: (0, 0)),
    ]
    operands = [q, ksa, vsa, kw, vw, kcmp, vcmp, gates, overlap_t[:n_slc, :n_cmp]]
    n_in = len(operands)

    def body(*refs):
        _nsa_kernel(*refs[:n_in], *refs[len(refs) - 2:], q_base=q_base)

    aliases = {}
    if prev is not None:
        in_specs.append(pl.BlockSpec(memory_space=pl.ANY))
        operands.append(prev)
        aliases = {n_in: 0}
    return pl.pallas_call(
        body,
        grid=(B, G, q_count // NSA_QUERIES),
        in_specs=in_specs,
        out_specs=pl.BlockSpec((1, NSA_QUERIES, R * dk), lambda b, g, i: (b, first + i, g)),
        out_shape=jax.ShapeDtypeStruct((B, S, G * R * dk), BF16),
        scratch_shapes=[pltpu.VMEM((SUBLANES, LANES), F32)],
        input_output_aliases=aliases,
        compiler_params=_params(3),
        name="nsa",
    )(*operands)


def _nsa(q, ksa, vsa, kw, vw, kcmp, vcmp, gates, overlap_t):
    S = q.shape[3]
    span = min(NSA_SPAN, S)
    out = None
    for base in range(0, S, span):
        out = _nsa_span(q, ksa, vsa, kw, vw, kcmp, vcmp, gates, overlap_t, base, span, out)
    return out


def _mix_kernel(x_ref, g_ref, wuvm_ref, onsa_ref, lng_ref, lnb_ref, ws_ref, bs_ref,
                wn_ref, wg_ref, wm_ref, gx_ref, wq_ref, k_ref, v_ref, wo_ref, h_ref):
    D = x_ref.shape[1]
    W = lng_ref.shape[1]
    hn = _rms(x_ref[...], g_ref[...]).astype(BF16)
    u = _gelu(_dot(hn, wuvm_ref[:, :W]))
    v = _gelu(_dot(hn, wuvm_ref[:, W:2 * W]))
    mu = jnp.mean(v, axis=-1, keepdims=True)
    var = jnp.mean(jnp.square(v - mu), axis=-1, keepdims=True)
    vn = ((v - mu) * lax.rsqrt(var + NORM_EPS) * lng_ref[...] + lnb_ref[...]).astype(BF16)
    C = GMLP_CHUNK
    gd = vn.shape[1] // GMLP_GROUPS
    rows = []
    for c in range(vn.shape[0] // C):
        cols = []
        for g in range(GMLP_GROUPS):
            cols.append(_dot(ws_ref[g], vn[c * C:(c + 1) * C, g * gd:(g + 1) * gd]) + bs_ref[g])
        rows.append(jnp.concatenate(cols, axis=1))
    o_gmlp = u * jnp.concatenate(rows, axis=0)
    mg = _sigmoid(_dot(hn, wuvm_ref[:, 2 * W:]))
    y = (mg[:, :D] * _dot(onsa_ref[...], wn_ref[...])
         + mg[:, D:] * _dot(o_gmlp.astype(BF16), wg_ref[...]))
    h = x_ref[...] + _dot(y.astype(BF16), wm_ref[...])
    h_ref[...] = _xattn_block(h, gx_ref, wq_ref, k_ref, v_ref, wo_ref)


def _mix(x2, g_mix, w_uvm, o_nsa, ln_g, ln_b, ws_causal, bs_b, w_nsa_out, w_gmlp_out, w_mix_out,
         g_xattn, w_xq, k_mem, v_mem, w_xo, S):
    T, D = x2.shape
    tm = MIX_ROWS
    W = ln_g.shape[0]
    per_b = S // tm
    mem_spec = pl.BlockSpec((1, k_mem.shape[1], D), lambda i: (i // per_b, 0, 0))

    def tok(i):
        return (i, 0)

    def const2(i):
        return (0, 0)

    def const3(i):
        return (0, 0, 0)

    return pl.pallas_call(
        _mix_kernel,
        grid=(T // tm,),
        in_specs=[
            pl.BlockSpec((tm, D), tok),
            pl.BlockSpec((1, D), const2),
            pl.BlockSpec(w_uvm.shape, const2),
            pl.BlockSpec((tm, o_nsa.shape[1]), tok),
            pl.BlockSpec((1, W), const2),
            pl.BlockSpec((1, W), const2),
            pl.BlockSpec(ws_causal.shape, const3),
            pl.BlockSpec(bs_b.shape, const3),
            pl.BlockSpec(w_nsa_out.shape, const2),
            pl.BlockSpec(w_gmlp_out.shape, const2),
            pl.BlockSpec(w_mix_out.shape, const2),
            pl.BlockSpec((1, D), const2),
            pl.BlockSpec((D, D), const2),
            mem_spec, mem_spec,
            pl.BlockSpec((D, D), const2),
        ],
        out_specs=pl.BlockSpec((tm, D), tok),
        out_shape=jax.ShapeDtypeStruct((T, D), F32),
        compiler_params=_params(1),
        name="mix",
    )(x2, g_mix.reshape(1, D), w_uvm, o_nsa, ln_g.reshape(1, W), ln_b.reshape(1, W), ws_causal,
      bs_b, w_nsa_out, w_gmlp_out, w_mix_out, g_xattn.reshape(1, D), w_xq, k_mem, v_mem, w_xo)


def _mem_kv_kernel(mem_ref, g_ref, w_ref, k_ref, v_ref):
    D = mem_ref.shape[2]
    mn = _rms(mem_ref[0], g_ref[...]).astype(BF16)
    kv = _dot(mn, w_ref[...])
    k_ref[0] = kv[:, :D].astype(BF16)
    v_ref[0] = kv[:, D:].astype(BF16)


def _mem_kv(mem, g_mem, w_xkv):
    B, M, D = mem.shape
    spec = pl.BlockSpec((1, M, D), lambda b: (b, 0, 0))
    return pl.pallas_call(
        _mem_kv_kernel,
        grid=(B,),
        in_specs=[spec, pl.BlockSpec((1, D), lambda b: (0, 0)),
                  pl.BlockSpec((D, 2 * D), lambda b: (0, 0))],
        out_specs=(spec, spec),
        out_shape=(jax.ShapeDtypeStruct((B, M, D), BF16), jax.ShapeDtypeStruct((B, M, D), BF16)),
        compiler_params=_params(1),
        name="mem_kv",
    )(mem, g_mem.reshape(1, D), w_xkv)


def _xattn_block(h, g_ref, wq_ref, k_ref, v_ref, wo_ref):
    D = h.shape[1]
    hd = D // XATTN_HEADS
    hn = _rms(h, g_ref[...]).astype(BF16)
    q = _dot(hn, wq_ref[...]) * (hd ** -0.5)
    outs = []
    for a in range(XATTN_HEADS):
        s = _dot_nt(q[:, a * hd:(a + 1) * hd].astype(BF16), k_ref[0, :, a * hd:(a + 1) * hd])
        m = jnp.max(s, axis=-1, keepdims=True)
        e = jnp.exp(s - m)
        p = e / jnp.sum(e, axis=-1, keepdims=True)
        outs.append(_dot(p.astype(BF16), v_ref[0, :, a * hd:(a + 1) * hd]))
    o = jnp.concatenate(outs, axis=1).astype(BF16)
    return h + _dot(o, wo_ref[...])


def _top_rows(vals, k, row=None):
    if row is None:
        row = lax.broadcasted_iota(I32, vals.shape, 0).astype(F32)
    top_v, top_i = [], []
    for _ in range(k):
        m = jnp.max(vals, axis=0, keepdims=True)
        first = jnp.min(jnp.where(vals == m, row, 1.0e9), axis=0, keepdims=True)
        top_v.append(m)
        top_i.append(first)
        vals = jnp.where(row == first, -3.0e38, vals)
    return top_v, top_i


def _pair_candidates(s0_all, s1_all, K):
    L, n_tok = s0_all.shape
    sub = lax.broadcasted_iota(I32, (SUBLANES, n_tok), 0).astype(F32)
    vals, ids = [], []
    a = 0
    while a < L and min(K // (a + 1), L) > 1:
        nb = min(K // (a + 1), L)
        for b0 in range(0, nb, SUBLANES):
            v = s0_all[a:a + 1] + s1_all[b0:b0 + SUBLANES]
            vals.append(v if b0 + SUBLANES <= nb else jnp.where(sub < float(nb - b0), v, -3.0e38))
            ids.append(sub + float(a * K + b0))
        a += 1
    while a < L:
        vals.append(s0_all[a:a + SUBLANES] + s1_all[0:1])
        ids.append((sub + float(a)) * float(K))
        a += SUBLANES
    return jnp.concatenate(vals, axis=0), jnp.concatenate(ids, axis=0)


def _route_from_scores(sc0, sc1):
    K = PEER_TOPK
    n_tok = sc0.shape[1]
    s0, i0 = _top_rows(sc0, K)
    s1, i1 = _top_rows(sc1, K)
    s0_all = jnp.concatenate(s0, axis=0)
    s1_all = jnp.concatenate(s1, axis=0)
    i0_all = jnp.concatenate(i0, axis=0)
    i1_all = jnp.concatenate(i1, axis=0)
    cand, cand_id = _pair_candidates(s0_all, s1_all, K)
    best_s, pos = _top_rows(cand, K, cand_id)
    best = jnp.concatenate(best_s, axis=0)
    m = jnp.max(best, axis=0, keepdims=True)
    e = jnp.exp(best - m)
    gate = e / jnp.sum(e, axis=0, keepdims=True)
    rank = lax.broadcasted_iota(I32, (K, n_tok), 0).astype(F32)
    ei, ej = [], []
    for kk in range(K):
        a = jnp.floor(pos[kk] * (1.0 / K))
        b = pos[kk] - a * K
        ei.append(jnp.sum(jnp.where(rank == a, i0_all, 0.0), axis=0, keepdims=True))
        ej.append(jnp.sum(jnp.where(rank == b, i1_all, 0.0), axis=0, keepdims=True))
    return jnp.concatenate(ei, axis=0), jnp.concatenate(ej, axis=0), gate


def _route_head(qt_scr, sk_ref, hd):
    half_dim = sk_ref.shape[2]
    scores = []
    for p in range(2):
        off = pl.multiple_of((hd * 2 + p) * half_dim, half_dim)
        qs = qt_scr[pl.ds(off, half_dim), :].astype(BF16)
        scores.append(_dot(sk_ref[hd * 2 + p], qs))
    return scores


def _route_kernel(h_ref, g_ref, wq_ref, sk_ref, hn_ref, i_ref, j_ref, gate_ref,
                  qt_scr, i_scr, j_scr, gate_scr):
    K = PEER_TOPK
    hn = _rms(h_ref[...], g_ref[...]).astype(BF16)
    hn_ref[...] = hn
    qt_scr[...] = _dot_nt(wq_ref[...], hn)

    def head(hd, _):
        sc0, sc1 = _route_head(qt_scr, sk_ref, hd)
        row0 = pl.multiple_of(hd * K, K)
        ei, ej, gate = _route_from_scores(sc0, sc1)
        i_scr[pl.ds(row0, K), :] = ei
        j_scr[pl.ds(row0, K), :] = ej
        gate_scr[pl.ds(row0, K), :] = gate
        return 0

    lax.fori_loop(0, PEER_HEADS, head, 0)
    i_ref[...] = i_scr[...].T.astype(I32)
    j_ref[...] = j_scr[...].T.astype(I32)
    gate_ref[...] = gate_scr[...].T


def _route(h, g_peer, wq_t, sub_keys):
    T, D = h.shape
    tt = ROUTE_TOKENS
    n_sel = PEER_HEADS * PEER_TOPK
    tok = lambda i: (i, 0)
    return pl.pallas_call(
        _route_kernel,
        grid=(T // tt,),
        in_specs=[
            pl.BlockSpec((tt, D), tok),
            pl.BlockSpec((1, D), lambda i: (0, 0)),
            pl.BlockSpec(wq_t.shape, lambda i: (0, 0)),
            pl.BlockSpec(sub_keys.shape, lambda i: (0, 0, 0)),
        ],
        out_specs=(
            pl.BlockSpec((tt, D), tok),
            pl.BlockSpec((tt, n_sel), tok),
            pl.BlockSpec((tt, n_sel), tok),
            pl.BlockSpec((tt, n_sel), tok),
        ),
        out_shape=(
            jax.ShapeDtypeStruct((T, D), BF16),
            jax.ShapeDtypeStruct((T, n_sel), I32),
            jax.ShapeDtypeStruct((T, n_sel), I32),
            jax.ShapeDtypeStruct((T, n_sel), F32),
        ),
        scratch_shapes=[
            pltpu.VMEM((wq_t.shape[0], tt), F32),
            pltpu.VMEM((n_sel, tt), F32),
            pltpu.VMEM((n_sel, tt), F32),
            pltpu.VMEM((n_sel, tt), F32),
        ],
        compiler_params=_params(1),
        name="route",
    )(h, g_peer.reshape(1, D), wq_t, sub_keys)


def _peer_up_kernel(hn_ref, u_ref, i_ref, j_ref, a_ref):
    e = pl.program_id(1)

    @pl.when(e == 0)
    def _():
        a_ref[...] = jnp.zeros(a_ref.shape, F32)

    hn = hn_ref[...]
    ii = i_ref[...]
    jj = j_ref[...]
    acc = a_ref[...]
    for r0 in range(0, EXPERT_ROWS, 2):
        z = _dot_nt(hn, u_ref[r0 * PEER_KEYS:(r0 + 2) * PEER_KEYS, :])
        for r in range(r0, r0 + 2):
            picked = jnp.take_along_axis(
                z[:, (r - r0) * PEER_KEYS:(r - r0 + 1) * PEER_KEYS], jj, axis=1)
            acc = jnp.where(ii == e * EXPERT_ROWS + r, picked, acc)
    a_ref[...] = acc


def _peer_up(hn, u_bf, i_idx, j_idx):
    T, D = hn.shape
    tt = UP_TOKENS
    n_sel = i_idx.shape[1]
    rows = EXPERT_ROWS * PEER_KEYS
    tok = lambda t, e: (t, 0)
    return pl.pallas_call(
        _peer_up_kernel,
        grid=(T // tt, u_bf.shape[0] // rows),
        in_specs=[
            pl.BlockSpec((tt, D), tok),
            pl.BlockSpec((rows, D), lambda t, e: (e, 0)),
            pl.BlockSpec((tt, n_sel), tok),
            pl.BlockSpec((tt, n_sel), tok),
        ],
        out_specs=pl.BlockSpec((tt, n_sel), tok),
        out_shape=jax.ShapeDtypeStruct((T, n_sel), F32),
        compiler_params=_params(2),
        name="peer_up",
    )(hn, u_bf, i_idx, j_idx)


def _peer_down_kernel(a_ref, gate_ref, i_ref, j_ref, v_ref, h_ref, g_ref, o_ref,
                      w_scr, coef_scr, acc_scr):
    e = pl.program_id(1)
    n_tok = a_ref.shape[0]
    nk = PEER_KEYS

    @pl.when(e == 0)
    def _():
        coef_scr[...] = gate_ref[...] * _gelu(a_ref[...])
        acc_scr[...] = jnp.zeros(acc_scr.shape, F32)
        sub = lax.broadcasted_iota(I32, (nk, i_ref.shape[1]), 0)

        def expand(t, _):
            ii = i_ref[pl.ds(t, 1), :]
            jj = j_ref[pl.ds(t, 1), :]
            cc = coef_scr[pl.ds(t, 1), :]
            a = jnp.where(sub == ii, cc, 0.0).astype(BF16)
            bt = jnp.where(sub == jj, 1.0, 0.0).astype(BF16)
            base = pl.multiple_of(t * W_ROW_STRIDE, SUBLANES)
            w_scr[pl.ds(base, nk), :] = _dot_nt(a, bt)
            return 0

        lax.fori_loop(0, n_tok, expand, 0, unroll=EXPAND_UNROLL)

    parts = []
    for r in range(DOWN_EXPERT_ROWS):
        row = e * DOWN_EXPERT_ROWS + r
        parts.append(w_scr[pl.ds(row, n_tok, stride=W_ROW_STRIDE), :].astype(BF16))
    lhs = jnp.concatenate(parts, axis=1)
    acc_scr[...] += _dot(lhs, v_ref[...])

    @pl.when(e == pl.num_programs(1) - 1)
    def _():
        o_ref[...] = _rms(h_ref[...] + acc_scr[...], g_ref[...])


def _peer_down(a_pre, gate, i_idx, j_idx, v_bf, h, g_final):
    T, D = h.shape
    tt = DOWN_TOKENS
    n_sel = i_idx.shape[1]
    rows = DOWN_EXPERT_ROWS * PEER_KEYS
    tok = lambda t, e: (t, 0)
    sel_spec = pl.BlockSpec((tt, n_sel), tok)
    return pl.pallas_call(
        _peer_down_kernel,
        grid=(T // tt, v_bf.shape[0] // rows),
        in_specs=[
            sel_spec, sel_spec, sel_spec, sel_spec,
            pl.BlockSpec((rows, D), lambda t, e: (e, 0)),
            pl.BlockSpec((tt, D), tok),
            pl.BlockSpec((1, D), lambda t, e: (0, 0)),
        ],
        out_specs=pl.BlockSpec((tt, D), tok),
        out_shape=jax.ShapeDtypeStruct((T, D), F32),
        scratch_shapes=[
            pltpu.VMEM((tt * W_ROW_STRIDE, PEER_KEYS), F32),
            pltpu.VMEM((tt, n_sel), F32),
            pltpu.VMEM((tt, D), F32),
        ],
        compiler_params=_params(2),
        name="peer_down",
    )(a_pre, gate, i_idx, j_idx, v_bf, h, g_final.reshape(1, D))


def _overlap_table(n_cmp, n_slc):
    c0 = jnp.arange(n_cmp) * CMP_STRIDE
    j0 = jnp.arange(n_slc) * SLC_BLOCK
    ov = jnp.clip(jnp.minimum(c0[:, None] + CMP_BLOCK, j0[None, :] + SLC_BLOCK)
                  - jnp.maximum(c0[:, None], j0[None, :]), 0, None)
    return ov.astype(F32) / CMP_BLOCK


def kernel(x, mem, g_mix, w_in, cmp_pe_k, cmp_w1_k, cmp_b1_k, cmp_w2_k, cmp_pe_v, cmp_w1_v, cmp_b1_v, cmp_w2_v, gmlp_ln_g, gmlp_ln_b, gmlp_ws, gmlp_bs, w_nsa_out, w_gmlp_out, w_mix_out, g_xattn, g_mem, w_xq, w_xkv, w_xo, g_peer, w_peer_q, peer_sub_keys, peer_u, peer_v, g_final):
    B, S, D = x.shape
    T = B * S
    G, R, dk = NSA_GROUPS, NSA_REP, NSA_HEAD_DIM
    x2 = x.reshape(T, D)

    n_gate = NSA_HEADS * 3
    gate_cols = w_in[:, 1280:1280 + n_gate]
    per_g = n_gate // G
    gate_blocks = [jnp.pad(gate_cols[:, g * per_g:(g + 1) * per_g], ((0, 0), (0, LANES - per_g)))
                   for g in range(G)]
    w_re = jnp.concatenate([w_in[:, :1280]] + gate_blocks + [w_in[:, 1280 + n_gate:]],
                           axis=1).astype(BF16)
    half = CMP_STRIDE * dk
    pe = jnp.stack([cmp_pe_k.reshape(2, half), cmp_pe_v.reshape(2, half)])
    w1 = jnp.stack([cmp_w1_k, cmp_w1_v]).astype(BF16)
    b1 = jnp.stack([cmp_b1_k, cmp_b1_v])[:, None, :]
    w2 = jnp.stack([cmp_w2_k, cmp_w2_v]).astype(BF16)
    C = GMLP_CHUNK
    ws_causal = (gmlp_ws * jnp.tril(jnp.ones((C, C), F32))).astype(BF16)
    group_dim = gmlp_ln_g.shape[0] // GMLP_GROUPS
    bs_b = jnp.broadcast_to(gmlp_bs[:, :, None], (GMLP_GROUPS, C, group_dim))
    n_chunks = S // CMP_STRIDE
    overlap = _overlap_table(n_chunks, S // SLC_BLOCK)

    q, kvc, ksa, vsa, kw, vw, gates = _in_proj(x2, g_mix, w_re[:, :_C_U], B, S)
    cmp_kv = _compress(kvc.reshape(2, B * G, n_chunks, half), pe, w1, b1, w2)
    cmp_kv = cmp_kv.reshape(2, B, G, n_chunks, dk)
    o_nsa = _nsa(q, ksa, vsa, kw, vw, cmp_kv[0], cmp_kv[1], gates, overlap.T.astype(BF16))
    k_mem, v_mem = _mem_kv(mem, g_mem, w_xkv.astype(BF16))
    h = _mix(x2, g_mix, w_re[:, _C_U:], o_nsa.reshape(T, G * R * dk), gmlp_ln_g, gmlp_ln_b,
             ws_causal, bs_b, w_nsa_out.astype(BF16), w_gmlp_out.astype(BF16),
             w_mix_out.astype(BF16), g_xattn, w_xq.astype(BF16), k_mem, v_mem,
             w_xo.astype(BF16), S)

    half_dim = peer_sub_keys.shape[3]
    sk = peer_sub_keys.reshape(PEER_HEADS * 2, PEER_KEYS, half_dim).astype(BF16)
    hn, i_idx, j_idx, gate = _route(h, g_peer, w_peer_q.T.astype(BF16), sk)
    a_pre = _peer_up(hn, peer_u.astype(BF16), i_idx, j_idx)
    out = _peer_down(a_pre, gate, i_idx, j_idx, peer_v.astype(BF16), h, g_final)
    return out.reshape(B, S, D)
```

```python
import functools

import jax
import jax.numpy as jnp
from jax import lax
from jax.experimental import pallas as pl
from jax.experimental.pallas import tpu as pltpu

F32 = jnp.float32
BF16 = jnp.bfloat16
I32 = jnp.int32

NORM_EPS = 1e-6
NEG_INF = -1e30
TINY = 1e-30
LOG2_E = 1.4426950408889634
MASK_MARGIN = 300.0

NSA_HEADS = 8
NSA_HEAD_DIM = 64
NSA_GROUPS = 2
NSA_REP = NSA_HEADS // NSA_GROUPS
CMP_BLOCK = 32
CMP_STRIDE = 16
SLC_BLOCK = 64
SLC_TOPN = 16
WINDOW = 512
FORCE_BONUS = 1e4
GMLP_GROUPS = 4
GMLP_CHUNK = 128
XATTN_HEADS = 4
PEER_HEADS = 8
PEER_KEYS = 128
PEER_TOPK = 16

LANES = 128
SUBLANES = 8
VMEM_LIMIT_BYTES = 56 * 1024 * 1024

PROJ_ROWS = 512
MIX_ROWS = 512
NSA_QUERIES = 256
SEL_KEYS = 1024
NSA_SPAN = 2048
ROUTE_TOKENS = 512
UP_TOKENS = 1024
DOWN_TOKENS = 512
EXPERT_ROWS = 32
DOWN_EXPERT_ROWS = 8
W_ROW_STRIDE = PEER_KEYS + SUBLANES
EXPAND_UNROLL = 64


def _params(n_axes):
    return pltpu.CompilerParams(
        dimension_semantics=("arbitrary",) * n_axes,
        vmem_limit_bytes=VMEM_LIMIT_BYTES,
    )


def _rms(x, g):
    return x * lax.rsqrt(jnp.mean(x * x, axis=-1, keepdims=True) + NORM_EPS) * g


def _gelu(x):
    return 0.5 * x * (1.0 + lax.erf(x * 0.7071067811865476))


def _sigmoid(x):
    return 1.0 / (1.0 + jnp.exp(-x))


def _dot(a, b):
    return jnp.dot(a, b, preferred_element_type=F32)


def _dot_nt(a, b):
    return lax.dot_general(a, b, (((1,), (1,)), ((), ())), preferred_element_type=F32)


_C_Q = 0
_C_KV = 512
_C_GATE = 1280
_C_U = 1536


def _in_proj_kernel(x_ref, g_ref, w_ref, q_ref, kvc_ref, ks_ref, vs_ref, kw_ref, vw_ref,
                    gate_ref, *, seq_len):
    hn = _rms(x_ref[...], g_ref[...]).astype(BF16)

    def proj(a, b):
        return _dot(hn, w_ref[:, a:b])

    dk = NSA_HEAD_DIM
    pq = proj(_C_Q, _C_KV) * (dk ** -0.5 * LOG2_E)
    for g in range(NSA_GROUPS):
        for r in range(NSA_REP):
            c = (g * NSA_REP + r) * dk
            q_ref[0, g, r] = pq[:, c:c + dk].astype(BF16)
    pk = proj(_C_KV, _C_GATE)
    tm = x_ref.shape[0]
    blk_w = ks_ref.shape[3] - 2 * dk
    pos = (pl.program_id(0) % (seq_len // tm)) * tm + lax.broadcasted_iota(I32, (tm, blk_w), 0)
    own_block = jnp.where(pos // SLC_BLOCK == lax.broadcasted_iota(I32, (tm, blk_w), 1),
                          -1.0, 0.0).astype(BF16)
    ones_col = jnp.where(lax.broadcasted_iota(I32, (tm, LANES - dk), 1) == 0, 1.0, 0.0).astype(BF16)
    for g in range(NSA_GROUPS):
        kvc_ref[0, 0, g] = pk[:, 0 * 128 + g * dk:0 * 128 + (g + 1) * dk]
        kvc_ref[1, 0, g] = pk[:, 1 * 128 + g * dk:1 * 128 + (g + 1) * dk]
        ks_ref[0, g, :, :blk_w] = own_block
        ks_ref[0, g, :, blk_w:blk_w + dk] = (
            pk[:, 2 * 128 + g * dk:2 * 128 + (g + 1) * dk].astype(BF16))
        ks_ref[0, g, :, blk_w + dk:] = jnp.zeros((tm, dk), BF16)
        vs_ref[0, g, :, :dk] = pk[:, 3 * 128 + g * dk:3 * 128 + (g + 1) * dk].astype(BF16)
        vs_ref[0, g, :, dk:] = ones_col
        kw_ref[0, g] = pk[:, 4 * 128 + g * dk:4 * 128 + (g + 1) * dk].astype(BF16)
        vw_ref[0, g, :, :dk] = pk[:, 5 * 128 + g * dk:5 * 128 + (g + 1) * dk].astype(BF16)
        vw_ref[0, g, :, dk:] = ones_col
        gate_ref[0, g] = _sigmoid(proj(_C_GATE + g * 128, _C_GATE + (g + 1) * 128))


def _in_proj(x2, g_mix, w_re, B, S):
    T, D = x2.shape
    tm = PROJ_ROWS
    per_b = S // tm
    G, R, dk = NSA_GROUPS, NSA_REP, NSA_HEAD_DIM
    blk_w = -(-(S // SLC_BLOCK) // LANES) * LANES

    def tok(i):
        return (i, 0)

    def bgs(i):
        return (i // per_b, 0, i % per_b, 0)

    out_shape = (
        jax.ShapeDtypeStruct((B, G, R, S, dk), BF16),
        jax.ShapeDtypeStruct((2, B, G, S, dk), F32),
        jax.ShapeDtypeStruct((B, G, S, blk_w + 2 * dk), BF16),
        jax.ShapeDtypeStruct((B, G, S, LANES), BF16),
        jax.ShapeDtypeStruct((B, G, S, dk), BF16),
        jax.ShapeDtypeStruct((B, G, S, LANES), BF16),
        jax.ShapeDtypeStruct((B, G, S, LANES), F32),
    )
    kv_spec = pl.BlockSpec((1, G, tm, dk), bgs)
    out_specs = (
        pl.BlockSpec((1, G, R, tm, dk), lambda i: (i // per_b, 0, 0, i % per_b, 0)),
        pl.BlockSpec((2, 1, G, tm, dk), lambda i: (0, i // per_b, 0, i % per_b, 0)),
        pl.BlockSpec((1, G, tm, blk_w + 2 * dk), bgs),
        pl.BlockSpec((1, G, tm, LANES), bgs),
        kv_spec,
        pl.BlockSpec((1, G, tm, LANES), bgs),
        pl.BlockSpec((1, G, tm, LANES), bgs),
    )
    return pl.pallas_call(
        functools.partial(_in_proj_kernel, seq_len=S),
        grid=(T // tm,),
        in_specs=[
            pl.BlockSpec((tm, D), tok),
            pl.BlockSpec((1, D), lambda i: (0, 0)),
            pl.BlockSpec((D, _C_U), lambda i: (0, 0)),
        ],
        out_specs=out_specs,
        out_shape=out_shape,
        compiler_params=_params(1),
        name="in_proj",
    )(x2, g_mix.reshape(1, D), w_re)


def _compress_kernel(x_ref, pe_ref, w1_ref, b1_ref, w2_ref, o_ref):
    x = x_ref[0, 0]
    half = x.shape[1]
    lo = (x + pe_ref[0, 0:1, :]).astype(BF16)
    hi = (x + pe_ref[0, 1:2, :]).astype(BF16)
    p = _dot(lo, w1_ref[0, :half, :])
    q = _dot(hi, w1_ref[0, half:, :])
    n = x.shape[0]
    h = p + pltpu.roll(q, n - 1, 0) + b1_ref[0]
    o_ref[0, 0] = _dot(_gelu(h).astype(BF16), w2_ref[0]).astype(BF16)


def _compress(kvc, pe, w1, b1, w2):
    _, BG, n_chunks, width = kvc.shape
    hidden = w1.shape[-1]
    dk = w2.shape[-1]
    return pl.pallas_call(
        _compress_kernel,
        grid=(2, BG),
        in_specs=[
            pl.BlockSpec((1, 1, n_chunks, width), lambda a, b: (a, b, 0, 0)),
            pl.BlockSpec((1, 2, width), lambda a, b: (a, 0, 0)),
            pl.BlockSpec((1, 2 * width, hidden), lambda a, b: (a, 0, 0)),
            pl.BlockSpec((1, 1, hidden), lambda a, b: (a, 0, 0)),
            pl.BlockSpec((1, hidden, dk), lambda a, b: (a, 0, 0)),
        ],
        out_specs=pl.BlockSpec((1, 1, n_chunks, dk), lambda a, b: (a, b, 0, 0)),
        out_shape=jax.ShapeDtypeStruct((2, BG, n_chunks, dk), BF16),
        compiler_params=_params(2),
        name="compress",
    )(kvc, pe, w1, b1, w2)


def _select_blocks(score_t):
    n = score_t.shape[0]
    row = lax.broadcasted_iota(I32, score_t.shape, 0).astype(F32)
    s = score_t
    for _ in range(SLC_TOPN):
        m = jnp.max(s, axis=0, keepdims=True)
        first = jnp.min(jnp.where(s == m, row, float(n)), axis=0, keepdims=True)
        s = jnp.where(row == first, -3.0e38, s)
    return jnp.where((s < -2.0e38) & (score_t > 0.5 * NEG_INF), 1.0, 0.0)


def _nsa_kernel(q_ref, ksa_ref, vsa_ref, kw_ref, vw_ref, kc_ref, vc_ref, gate_ref, ovt_ref, o_ref,
                kmax_scr, *, q_base):
    qb = pl.program_id(2)
    q0 = q_base + qb * NSA_QUERIES
    R, Q, dk = NSA_REP, NSA_QUERIES, NSA_HEAD_DIM
    q2 = q_ref[0, 0].reshape(R * Q, dk)

    kc = kc_ref[0, 0]
    n_cmp = kc.shape[0]
    s_c = _dot_nt(q2, kc)
    t_c = q0 + lax.broadcasted_iota(I32, (Q, n_cmp), 0)
    cmp_end = lax.broadcasted_iota(I32, (Q, n_cmp), 1) * CMP_STRIDE + (CMP_BLOCK - 1)
    mask_c = cmp_end <= t_c
    o_c = []
    p_sum = jnp.zeros((Q, n_cmp), F32)
    for r in range(R):
        s_r = jnp.where(mask_c, s_c[r * Q:(r + 1) * Q], NEG_INF)
        e = jnp.where(mask_c, jnp.exp2(s_r - jnp.max(s_r, axis=-1, keepdims=True)), 0.0)
        inv = 1.0 / jnp.maximum(jnp.sum(e, axis=-1, keepdims=True), TINY)
        p_sum = p_sum + e * inv
        o_c.append(_dot(e.astype(BF16), vc_ref[0, 0]) * inv)

    span = WINDOW + Q
    start = pl.multiple_of(jnp.maximum(q0 - WINDOW, 0), Q)
    kw = kw_ref[0, 0, pl.ds(start, span), :]
    vw = vw_ref[0, 0, pl.ds(start, span), :]
    s_w = _dot_nt(q2, kw)
    pos = start + lax.broadcasted_iota(I32, (Q, span), 1)
    diff = q0 + lax.broadcasted_iota(I32, (Q, span), 0) - pos
    mask_w = (diff >= 0) & (diff < WINDOW)
    o_w = []
    for r in range(R):
        s_r = jnp.where(mask_w, s_w[r * Q:(r + 1) * Q], NEG_INF)
        e = jnp.exp2(s_r - jnp.max(s_r, axis=-1, keepdims=True))
        o = _dot(e.astype(BF16), vw)
        o_w.append(o[:, :dk] / jnp.maximum(o[:, dk:dk + 1], TINY))

    n_slc = ovt_ref.shape[0]
    ovt = ovt_ref[...]
    p_hi = p_sum.astype(BF16)
    rest = p_sum - p_hi.astype(F32)
    p_mid = rest.astype(BF16)
    p_lo = (rest - p_mid.astype(F32)).astype(BF16)
    imp_t = _dot_nt(ovt, p_hi) + _dot_nt(ovt, p_mid) + _dot_nt(ovt, p_lo)
    t_b = q0 + lax.broadcasted_iota(I32, (n_slc, Q), 1)
    blk = lax.broadcasted_iota(I32, (n_slc, Q), 0)
    allowed = blk * SLC_BLOCK <= t_b
    cur = t_b // SLC_BLOCK
    forced = (blk == 0) | (blk == cur) | (blk == cur - 1)
    score_t = jnp.where(forced & allowed, FORCE_BONUS, jnp.where(allowed, imp_t, NEG_INF))
    sel_t = _select_blocks(score_t)

    tk = SEL_KEYS
    blk_w = ksa_ref.shape[3] - 2 * dk

    @pl.when(qb == 0)
    def _():
        k_all = ksa_ref[0, 0, :, blk_w:blk_w + dk].astype(F32)
        k_sq = jnp.max(jnp.sum(k_all * k_all, axis=-1, keepdims=True), axis=0, keepdims=True)
        kmax_scr[...] = jnp.broadcast_to(jnp.sqrt(k_sq), kmax_scr.shape)

    q_f = q2.astype(F32)
    q_sq = jnp.max(jnp.sum(q_f * q_f, axis=-1, keepdims=True), axis=0, keepdims=True)
    drop = (2.0 * jnp.sqrt(q_sq) * kmax_scr[0:1, 0:1] + MASK_MARGIN) * 1.02
    if blk_w > n_slc:
        sel_t = jnp.concatenate([sel_t, jnp.zeros((blk_w - n_slc, Q), F32)], axis=0)
    mask_cols = ((1.0 - sel_t.T) * drop).astype(BF16)
    q_aug = jnp.concatenate(
        [jnp.concatenate([mask_cols] * R, axis=0), q2, jnp.zeros((R * Q, dk), BF16)], axis=1)
    last = q0 // tk
    va_w = vsa_ref.shape[3]

    def sel_tile(kt, carry, causal):
        ms, accs = carry
        base = pl.multiple_of(kt * tk, tk)
        s = _dot_nt(q_aug, ksa_ref[0, 0, pl.ds(base, tk), :])
        v_t = vsa_ref[0, 0, pl.ds(base, tk), :]
        if causal:
            ahead = (base + lax.broadcasted_iota(I32, (Q, tk), 1)
                     > q0 + lax.broadcasted_iota(I32, (Q, tk), 0))
        new_m, new_acc = [], []
        for r in range(R):
            s_r = s[r * Q:(r + 1) * Q]
            if causal:
                s_r = jnp.where(ahead, NEG_INF, s_r)
            m_new = jnp.maximum(ms[r], jnp.max(s_r, axis=-1, keepdims=True))
            alpha = jnp.exp2(ms[r] - m_new)
            p = jnp.exp2(s_r - m_new)
            new_m.append(m_new)
            new_acc.append(alpha * accs[r] + _dot(p.astype(BF16), v_t))
        return tuple(new_m), tuple(new_acc)

    init = (tuple(jnp.full((Q, 1), NEG_INF, F32) for _ in range(R)),
            tuple(jnp.zeros((Q, va_w), F32) for _ in range(R)))
    carry = lax.fori_loop(0, last, functools.partial(sel_tile, causal=False), init)
    _, accs = sel_tile(last, carry, causal=True)
    o_s = [accs[r][:, :dk] / jnp.maximum(accs[r][:, dk:dk + 1], TINY) for r in range(R)]

    gate = gate_ref[0, 0]
    for r in range(R):
        out = (gate[:, 3 * r:3 * r + 1] * o_c[r] + gate[:, 3 * r + 1:3 * r + 2] * o_s[r]
               + gate[:, 3 * r + 2:3 * r + 3] * o_w[r])
        o_ref[0, :, r * dk:(r + 1) * dk] = out.astype(BF16)


def _nsa_span(q, ksa, vsa, kw, vw, kcmp, vcmp, gates, overlap_t, q_base, q_count, prev):
    B, G, R, S, dk = q.shape
    prefix = q_base + q_count
    n_cmp = -(-(prefix // CMP_STRIDE) // LANES) * LANES
    n_cmp = min(n_cmp, kcmp.shape[2])
    n_slc = prefix // SLC_BLOCK
    first = q_base // NSA_QUERIES
    kv_spec = pl.BlockSpec((1, 1, prefix, dk), lambda b, g, i: (b, g, 0, 0))
    cmp_spec = pl.BlockSpec((1, 1, n_cmp, dk), lambda b, g, i: (b, g, 0, 0))
    in_specs = [
        pl.BlockSpec((1, 1, R, NSA_QUERIES, dk), lambda b, g, i: (b, g, 0, first + i, 0)),
        pl.BlockSpec((1, 1, prefix, ksa.shape[3]), lambda b, g, i: (b, g, 0, 0)),
        pl.BlockSpec((1, 1, prefix, vsa.shape[3]), lambda b, g, i: (b, g, 0, 0)),
        kv_spec,
        pl.BlockSpec((1, 1, prefix, vw.shape[3]), lambda b, g, i: (b, g, 0, 0)),
        cmp_spec, cmp_spec,
        pl.BlockSpec((1, 1, NSA_QUERIES, LANES), lambda b, g, i: (b, g, first + i, 0)),
        pl.BlockSpec((n_slc, n_cmp), lambda b, g, i: (0, 0)),
    ]
    operands = [q, ksa, vsa, kw, vw, kcmp, vcmp, gates, overlap_t[:n_slc, :n_cmp]]
    n_in = len(operands)

    def body(*refs):
        _nsa_kernel(*refs[:n_in], *refs[len(refs) - 2:], q_base=q_base)

    aliases = {}
    if prev is not None:
        in_specs.append(pl.BlockSpec(memory_space=pl.ANY))
        operands.append(prev)
        aliases = {n_in: 0}
    return pl.pallas_call(
        body,
        grid=(B, G, q_count // NSA_QUERIES),
        in_specs=in_specs,
        out_specs=pl.BlockSpec((1, NSA_QUERIES, R * dk), lambda b, g, i: (b, first + i, g)),
        out_shape=jax.ShapeDtypeStruct((B, S, G * R * dk), BF16),
        scratch_shapes=[pltpu.VMEM((SUBLANES, LANES), F32)],
        input_output_aliases=aliases,
        compiler_params=_params(3),
        name="nsa",
    )(*operands)


def _nsa(q, ksa, vsa, kw, vw, kcmp, vcmp, gates, overlap_t):
    S = q.shape[3]
    span = min(NSA_SPAN, S)
    out = None
    for base in range(0, S, span):
        out = _nsa_span(q, ksa, vsa, kw, vw, kcmp, vcmp, gates, overlap_t, base, span, out)
    return out


def _mix_kernel(x_ref, g_ref, wuvm_ref, onsa_ref, lng_ref, lnb_ref, ws_ref, bs_ref,
                wn_ref, wg_ref, wm_ref, gx_ref, wq_ref, k_ref, v_ref, wo_ref, h_ref):
    D = x_ref.shape[1]
    W = lng_ref.shape[1]
    hn = _rms(x_ref[...], g_ref[...]).astype(BF16)
    u = _gelu(_dot(hn, wuvm_ref[:, :W]))
    v = _gelu(_dot(hn, wuvm_ref[:, W:2 * W]))
    mu = jnp.mean(v, axis=-1, keepdims=True)
    var = jnp.mean(jnp.square(v - mu), axis=-1, keepdims=True)
    vn = ((v - mu) * lax.rsqrt(var + NORM_EPS) * lng_ref[...] + lnb_ref[...]).astype(BF16)
    C = GMLP_CHUNK
    gd = vn.shape[1] // GMLP_GROUPS
    rows = []
    for c in range(vn.shape[0] // C):
        cols = []
        for g in range(GMLP_GROUPS):
            cols.append(_dot(ws_ref[g], vn[c * C:(c + 1) * C, g * gd:(g + 1) * gd]) + bs_ref[g])
        rows.append(jnp.concatenate(cols, axis=1))
    o_gmlp = u * jnp.concatenate(rows, axis=0)
    mg = _sigmoid(_dot(hn, wuvm_ref[:, 2 * W:]))
    y = (mg[:, :D] * _dot(onsa_ref[...], wn_ref[...])
         + mg[:, D:] * _dot(o_gmlp.astype(BF16), wg_ref[...]))
    h = x_ref[...] + _dot(y.astype(BF16), wm_ref[...])
    h_ref[...] = _xattn_block(h, gx_ref, wq_ref, k_ref, v_ref, wo_ref)


def _mix(x2, g_mix, w_uvm, o_nsa, ln_g, ln_b, ws_causal, bs_b, w_nsa_out, w_gmlp_out, w_mix_out,
         g_xattn, w_xq, k_mem, v_mem, w_xo, S):
    T, D = x2.shape
    tm = MIX_ROWS
    W = ln_g.shape[0]
    per_b = S // tm
    mem_spec = pl.BlockSpec((1, k_mem.shape[1], D), lambda i: (i // per_b, 0, 0))

    def tok(i):
        return (i, 0)

    def const2(i):
        return (0, 0)

    def const3(i):
        return (0, 0, 0)

    return pl.pallas_call(
        _mix_kernel,
        grid=(T // tm,),
        in_specs=[
            pl.BlockSpec((tm, D), tok),
            pl.BlockSpec((1, D), const2),
            pl.BlockSpec(w_uvm.shape, const2),
            pl.BlockSpec((tm, o_nsa.shape[1]), tok),
            pl.BlockSpec((1, W), const2),
            pl.BlockSpec((1, W), const2),
            pl.BlockSpec(ws_causal.shape, const3),
            pl.BlockSpec(bs_b.shape, const3),
            pl.BlockSpec(w_nsa_out.shape, const2),
            pl.BlockSpec(w_gmlp_out.shape, const2),
            pl.BlockSpec(w_mix_out.shape, const2),
            pl.BlockSpec((1, D), const2),
            pl.BlockSpec((D, D), const2),
            mem_spec, mem_spec,
            pl.BlockSpec((D, D), const2),
        ],
        out_specs=pl.BlockSpec((tm, D), tok),
        out_shape=jax.ShapeDtypeStruct((T, D), F32),
        compiler_params=_params(1),
        name="mix",
    )(x2, g_mix.reshape(1, D), w_uvm, o_nsa, ln_g.reshape(1, W), ln_b.reshape(1, W), ws_causal,
      bs_b, w_nsa_out, w_gmlp_out, w_mix_out, g_xattn.reshape(1, D), w_xq, k_mem, v_mem, w_xo)


def _mem_kv_kernel(mem_ref, g_ref, w_ref, k_ref, v_ref):
    D = mem_ref.shape[2]
    mn = _rms(mem_ref[0], g_ref[...]).astype(BF16)
    kv = _dot(mn, w_ref[...])
    k_ref[0] = kv[:, :D].astype(BF16)
    v_ref[0] = kv[:, D:].astype(BF16)


def _mem_kv(mem, g_mem, w_xkv):
    B, M, D = mem.shape
    spec = pl.BlockSpec((1, M, D), lambda b: (b, 0, 0))
    return pl.pallas_call(
        _mem_kv_kernel,
        grid=(B,),
        in_specs=[spec, pl.BlockSpec((1, D), lambda b: (0, 0)),
                  pl.BlockSpec((D, 2 * D), lambda b: (0, 0))],
        out_specs=(spec, spec),
        out_shape=(jax.ShapeDtypeStruct((B, M, D), BF16), jax.ShapeDtypeStruct((B, M, D), BF16)),
        compiler_params=_params(1),
        name="mem_kv",
    )(mem, g_mem.reshape(1, D), w_xkv)


def _xattn_block(h, g_ref, wq_ref, k_ref, v_ref, wo_ref):
    D = h.shape[1]
    hd = D // XATTN_HEADS
    hn = _rms(h, g_ref[...]).astype(BF16)
    q = _dot(hn, wq_ref[...]) * (hd ** -0.5)
    outs = []
    for a in range(XATTN_HEADS):
        s = _dot_nt(q[:, a * hd:(a + 1) * hd].astype(BF16), k_ref[0, :, a * hd:(a + 1) * hd])
        m = jnp.max(s, axis=-1, keepdims=True)
        e = jnp.exp(s - m)
        p = e / jnp.sum(e, axis=-1, keepdims=True)
        outs.append(_dot(p.astype(BF16), v_ref[0, :, a * hd:(a + 1) * hd]))
    o = jnp.concatenate(outs, axis=1).astype(BF16)
    return h + _dot(o, wo_ref[...])


def _top_rows(vals, k, row=None):
    if row is None:
        row = lax.broadcasted_iota(I32, vals.shape, 0).astype(F32)
    top_v, top_i = [], []
    for _ in range(k):
        m = jnp.max(vals, axis=0, keepdims=True)
        first = jnp.min(jnp.where(vals == m, row, 1.0e9), axis=0, keepdims=True)
        top_v.append(m)
        top_i.append(first)
        vals = jnp.where(row == first, -3.0e38, vals)
    return top_v, top_i


def _pair_candidates(s0_all, s1_all, K):
    L, n_tok = s0_all.shape
    sub = lax.broadcasted_iota(I32, (SUBLANES, n_tok), 0).astype(F32)
    vals, ids = [], []
    a = 0
    while a < L and min(K // (a + 1), L) > 1:
        nb = min(K // (a + 1), L)
        for b0 in range(0, nb, SUBLANES):
            v = s0_all[a:a + 1] + s1_all[b0:b0 + SUBLANES]
            vals.append(v if b0 + SUBLANES <= nb else jnp.where(sub < float(nb - b0), v, -3.0e38))
            ids.append(sub + float(a * K + b0))
        a += 1
    while a < L:
        vals.append(s0_all[a:a + SUBLANES] + s1_all[0:1])
        ids.append((sub + float(a)) * float(K))
        a += SUBLANES
    return jnp.concatenate(vals, axis=0), jnp.concatenate(ids, axis=0)


def _route_from_scores(sc0, sc1):
    K = PEER_TOPK
    n_tok = sc0.shape[1]
    s0, i0 = _top_rows(sc0, K)
    s1, i1 = _top_rows(sc1, K)
    s0_all = jnp.concatenate(s0, axis=0)
    s1_all = jnp.concatenate(s1, axis=0)
    i0_all = jnp.concatenate(i0, axis=0)
    i1_all = jnp.concatenate(i1, axis=0)
    cand, cand_id = _pair_candidates(s0_all, s1_all, K)
    best_s, pos = _top_rows(cand, K, cand_id)
    best = jnp.concatenate(best_s, axis=0)
    m = jnp.max(best, axis=0, keepdims=True)
    e = jnp.exp(best - m)
    gate = e / jnp.sum(e, axis=0, keepdims=True)
    rank = lax.broadcasted_iota(I32, (K, n_tok), 0).astype(F32)
    ei, ej = [], []
    for kk in range(K):
        a = jnp.floor(pos[kk] * (1.0 / K))
        b = pos[kk] - a * K
        ei.append(jnp.sum(jnp.where(rank == a, i0_all, 0.0), axis=0, keepdims=True))
        ej.append(jnp.sum(jnp.where(rank == b, i1_all, 0.0), axis=0, keepdims=True))
    return jnp.concatenate(ei, axis=0), jnp.concatenate(ej, axis=0), gate


def _route_head(qt_scr, sk_ref, hd):
    half_dim = sk_ref.shape[2]
    scores = []
    for p in range(2):
        off = pl.multiple_of((hd * 2 + p) * half_dim, half_dim)
        qs = qt_scr[pl.ds(off, half_dim), :].astype(BF16)
        scores.append(_dot(sk_ref[hd * 2 + p], qs))
    return scores


def _route_kernel(h_ref, g_ref, wq_ref, sk_ref, hn_ref, i_ref, j_ref, gate_ref,
                  qt_scr, i_scr, j_scr, gate_scr):
    K = PEER_TOPK
    hn = _rms(h_ref[...], g_ref[...]).astype(BF16)
    hn_ref[...] = hn
    qt_scr[...] = _dot_nt(wq_ref[...], hn)

    def head(hd, _):
        sc0, sc1 = _route_head(qt_scr, sk_ref, hd)
        row0 = pl.multiple_of(hd * K, K)
        ei, ej, gate = _route_from_scores(sc0, sc1)
        i_scr[pl.ds(row0, K), :] = ei
        j_scr[pl.ds(row0, K), :] = ej
        gate_scr[pl.ds(row0, K), :] = gate
        return 0

    lax.fori_loop(0, PEER_HEADS, head, 0)
    i_ref[...] = i_scr[...].T.astype(I32)
    j_ref[...] = j_scr[...].T.astype(I32)
    gate_ref[...] = gate_scr[...].T


def _route(h, g_peer, wq_t, sub_keys):
    T, D = h.shape
    tt = ROUTE_TOKENS
    n_sel = PEER_HEADS * PEER_TOPK
    tok = lambda i: (i, 0)
    return pl.pallas_call(
        _route_kernel,
        grid=(T // tt,),
        in_specs=[
            pl.BlockSpec((tt, D), tok),
            pl.BlockSpec((1, D), lambda i: (0, 0)),
            pl.BlockSpec(wq_t.shape, lambda i: (0, 0)),
            pl.BlockSpec(sub_keys.shape, lambda i: (0, 0, 0)),
        ],
        out_specs=(
            pl.BlockSpec((tt, D), tok),
            pl.BlockSpec((tt, n_sel), tok),
            pl.BlockSpec((tt, n_sel), tok),
            pl.BlockSpec((tt, n_sel), tok),
        ),
        out_shape=(
            jax.ShapeDtypeStruct((T, D), BF16),
            jax.ShapeDtypeStruct((T, n_sel), I32),
            jax.ShapeDtypeStruct((T, n_sel), I32),
            jax.ShapeDtypeStruct((T, n_sel), F32),
        ),
        scratch_shapes=[
            pltpu.VMEM((wq_t.shape[0], tt), F32),
            pltpu.VMEM((n_sel, tt), F32),
            pltpu.VMEM((n_sel, tt), F32),
            pltpu.VMEM((n_sel, tt), F32),
        ],
        compiler_params=_params(1),
        name="route",
    )(h, g_peer.reshape(1, D), wq_t, sub_keys)


def _peer_up_kernel(hn_ref, u_ref, i_ref, j_ref, a_ref):
    e = pl.program_id(1)

    @pl.when(e == 0)
    def _():
        a_ref[...] = jnp.zeros(a_ref.shape, F32)

    hn = hn_ref[...]
    ii = i_ref[...]
    jj = j_ref[...]
    acc = a_ref[...]
    for r0 in range(0, EXPERT_ROWS, 2):
        z = _dot_nt(hn, u_ref[r0 * PEER_KEYS:(r0 + 2) * PEER_KEYS, :])
        for r in range(r0, r0 + 2):
            picked = jnp.take_along_axis(
                z[:, (r - r0) * PEER_KEYS:(r - r0 + 1) * PEER_KEYS], jj, axis=1)
            acc = jnp.where(ii == e * EXPERT_ROWS + r, picked, acc)
    a_ref[...] = acc


def _peer_up(hn, u_bf, i_idx, j_idx):
    T, D = hn.shape
    tt = UP_TOKENS
    n_sel = i_idx.shape[1]
    rows = EXPERT_ROWS * PEER_KEYS
    tok = lambda t, e: (t, 0)
    return pl.pallas_call(
        _peer_up_kernel,
        grid=(T // tt, u_bf.shape[0] // rows),
        in_specs=[
            pl.BlockSpec((tt, D), tok),
            pl.BlockSpec((rows, D), lambda t, e: (e, 0)),
            pl.BlockSpec((tt, n_sel), tok),
            pl.BlockSpec((tt, n_sel), tok),
        ],
        out_specs=pl.BlockSpec((tt, n_sel), tok),
        out_shape=jax.ShapeDtypeStruct((T, n_sel), F32),
        compiler_params=_params(2),
        name="peer_up",
    )(hn, u_bf, i_idx, j_idx)


def _peer_down_kernel(a_ref, gate_ref, i_ref, j_ref, v_ref, h_ref, g_ref, o_ref,
                      w_scr, coef_scr, acc_scr):
    e = pl.program_id(1)
    n_tok = a_ref.shape[0]
    nk = PEER_KEYS

    @pl.when(e == 0)
    def _():
        coef_scr[...] = gate_ref[...] * _gelu(a_ref[...])
        acc_scr[...] = jnp.zeros(acc_scr.shape, F32)
        sub = lax.broadcasted_iota(I32, (nk, i_ref.shape[1]), 0)

        def expand(t, _):
            ii = i_ref[pl.ds(t, 1), :]
            jj = j_ref[pl.ds(t, 1), :]
            cc = coef_scr[pl.ds(t, 1), :]
            a = jnp.where(sub == ii, cc, 0.0).astype(BF16)
            bt = jnp.where(sub == jj, 1.0, 0.0).astype(BF16)
            base = pl.multiple_of(t * W_ROW_STRIDE, SUBLANES)
            w_scr[pl.ds(base, nk), :] = _dot_nt(a, bt)
            return 0

        lax.fori_loop(0, n_tok, expand, 0, unroll=EXPAND_UNROLL)

    parts = []
    for r in range(DOWN_EXPERT_ROWS):
        row = e * DOWN_EXPERT_ROWS + r
        parts.append(w_scr[pl.ds(row, n_tok, stride=W_ROW_STRIDE), :].astype(BF16))
    lhs = jnp.concatenate(parts, axis=1)
    acc_scr[...] += _dot(lhs, v_ref[...])

    @pl.when(e == pl.num_programs(1) - 1)
    def _():
        o_ref[...] = _rms(h_ref[...] + acc_scr[...], g_ref[...])


def _peer_down(a_pre, gate, i_idx, j_idx, v_bf, h, g_final):
    T, D = h.shape
    tt = DOWN_TOKENS
    n_sel = i_idx.shape[1]
    rows = DOWN_EXPERT_ROWS * PEER_KEYS
    tok = lambda t, e: (t, 0)
    sel_spec = pl.BlockSpec((tt, n_sel), tok)
    return pl.pallas_call(
        _peer_down_kernel,
        grid=(T // tt, v_bf.shape[0] // rows),
        in_specs=[
            sel_spec, sel_spec, sel_spec, sel_spec,
            pl.BlockSpec((rows, D), lambda t, e: (e, 0)),
            pl.BlockSpec((tt, D), tok),
            pl.BlockSpec((1, D), lambda t, e: (0, 0)),
        ],
        out_specs=pl.BlockSpec((tt, D), tok),
        out_shape=jax.ShapeDtypeStruct((T, D), F32),
        scratch_shapes=[
            pltpu.VMEM((tt * W_ROW_STRIDE, PEER_KEYS), F32),
            pltpu.VMEM((tt, n_sel), F32),
            pltpu.VMEM((tt, D), F32),
        ],
        compiler_params=_params(2),
        name="peer_down",
    )(a_pre, gate, i_idx, j_idx, v_bf, h, g_final.reshape(1, D))


def _overlap_table(n_cmp, n_slc):
    c0 = jnp.arange(n_cmp) * CMP_STRIDE
    j0 = jnp.arange(n_slc) * SLC_BLOCK
    ov = jnp.clip(jnp.minimum(c0[:, None] + CMP_BLOCK, j0[None, :] + SLC_BLOCK)
                  - jnp.maximum(c0[:, None], j0[None, :]), 0, None)
    return ov.astype(F32) / CMP_BLOCK


def kernel(x, mem, g_mix, w_in, cmp_pe_k, cmp_w1_k, cmp_b1_k, cmp_w2_k, cmp_pe_v, cmp_w1_v, cmp_b1_v, cmp_w2_v, gmlp_ln_g, gmlp_ln_b, gmlp_ws, gmlp_bs, w_nsa_out, w_gmlp_out, w_mix_out, g_xattn, g_mem, w_xq, w_xkv, w_xo, g_peer, w_peer_q, peer_sub_keys, peer_u, peer_v, g_final):
    B, S, D = x.shape
    T = B * S
    G, R, dk = NSA_GROUPS, NSA_REP, NSA_HEAD_DIM
    x2 = x.reshape(T, D)

    n_gate = NSA_HEADS * 3
    gate_cols = w_in[:, 1280:1280 + n_gate]
    per_g = n_gate // G
    gate_blocks = [jnp.pad(gate_cols[:, g * per_g:(g + 1) * per_g], ((0, 0), (0, LANES - per_g)))
                   for g in range(G)]
    w_re = jnp.concatenate([w_in[:, :1280]] + gate_blocks + [w_in[:, 1280 + n_gate:]],
                           axis=1).astype(BF16)
    half = CMP_STRIDE * dk
    pe = jnp.stack([cmp_pe_k.reshape(2, half), cmp_pe_v.reshape(2, half)])
    w1 = jnp.stack([cmp_w1_k, cmp_w1_v]).astype(BF16)
    b1 = jnp.stack([cmp_b1_k, cmp_b1_v])[:, None, :]
    w2 = jnp.stack([cmp_w2_k, cmp_w2_v]).astype(BF16)
    C = GMLP_CHUNK
    ws_causal = (gmlp_ws * jnp.tril(jnp.ones((C, C), F32))).astype(BF16)
    group_dim = gmlp_ln_g.shape[0] // GMLP_GROUPS
    bs_b = jnp.broadcast_to(gmlp_bs[:, :, None], (GMLP_GROUPS, C, group_dim))
    n_chunks = S // CMP_STRIDE
    overlap = _overlap_table(n_chunks, S // SLC_BLOCK)

    q, kvc, ksa, vsa, kw, vw, gates = _in_proj(x2, g_mix, w_re[:, :_C_U], B, S)
    cmp_kv = _compress(kvc.reshape(2, B * G, n_chunks, half), pe, w1, b1, w2)
    cmp_kv = cmp_kv.reshape(2, B, G, n_chunks, dk)
    o_nsa = _nsa(q, ksa, vsa, kw, vw, cmp_kv[0], cmp_kv[1], gates, overlap.T.astype(BF16))
    k_mem, v_mem = _mem_kv(mem, g_mem, w_xkv.astype(BF16))
    h = _mix(x2, g_mix, w_re[:, _C_U:], o_nsa.reshape(T, G * R * dk), gmlp_ln_g, gmlp_ln_b,
             ws_causal, bs_b, w_nsa_out.astype(BF16), w_gmlp_out.astype(BF16),
             w_mix_out.astype(BF16), g_xattn, w_xq.astype(BF16), k_mem, v_mem,
             w_xo.astype(BF16), S)

    half_dim = peer_sub_keys.shape[3]
    sk = peer_sub_keys.reshape(PEER_HEADS * 2, PEER_KEYS, half_dim).astype(BF16)
    hn, i_idx, j_idx, gate = _route(h, g_peer, w_peer_q.T.astype(BF16), sk)
    a_pre = _peer_up(hn, peer_u.astype(BF16), i_idx, j_idx)
    out = _peer_down(a_pre, gate, i_idx, j_idx, peer_v.astype(BF16), h, g_final)
    return out.reshape(B, S, D)
```

```python
import functools

import jax
import jax.numpy as jnp
from jax import lax
from jax.experimental import pallas as pl
from jax.experimental.pallas import tpu as pltpu

F32 = jnp.float32
BF16 = jnp.bfloat16
I32 = jnp.int32

NORM_EPS = 1e-6
NEG_INF = -1e30
TINY = 1e-30
LOG2_E = 1.4426950408889634
MASK_MARGIN = 300.0

NSA_HEADS = 8
NSA_HEAD_DIM = 64
NSA_GROUPS = 2
NSA_REP = NSA_HEADS // NSA_GROUPS
CMP_BLOCK = 32
CMP_STRIDE = 16
SLC_BLOCK = 64
SLC_TOPN = 16
WINDOW = 512
FORCE_BONUS = 1e4
GMLP_GROUPS = 4
GMLP_CHUNK = 128
XATTN_HEADS = 4
PEER_HEADS = 8
PEER_KEYS = 128
PEER_TOPK = 16

LANES = 128
SUBLANES = 8
VMEM_LIMIT_BYTES = 56 * 1024 * 1024

PROJ_ROWS = 512
MIX_ROWS = 512
NSA_QUERIES = 256
SEL_KEYS = 1024
NSA_SPAN = 2048
ROUTE_TOKENS = 512
UP_TOKENS = 1024
DOWN_TOKENS = 512
EXPERT_ROWS = 32
DOWN_EXPERT_ROWS = 8
W_ROW_STRIDE = PEER_KEYS + SUBLANES
EXPAND_UNROLL = 64


def _params(n_axes):
    return pltpu.CompilerParams(
        dimension_semantics=("arbitrary",) * n_axes,
        vmem_limit_bytes=VMEM_LIMIT_BYTES,
    )


def _rms(x, g):
    return x * lax.rsqrt(jnp.mean(x * x, axis=-1, keepdims=True) + NORM_EPS) * g


def _gelu(x):
    return 0.5 * x * (1.0 + lax.erf(x * 0.7071067811865476))


def _sigmoid(x):
    return 1.0 / (1.0 + jnp.exp(-x))


def _dot(a, b):
    return jnp.dot(a, b, preferred_element_type=F32)


def _dot_nt(a, b):
    return lax.dot_general(a, b, (((1,), (1,)), ((), ())), preferred_element_type=F32)


_C_Q = 0
_C_KV = 512
_C_GATE = 1280
_C_U = 1536


def _in_proj_kernel(x_ref, g_ref, w_ref, q_ref, kvc_ref, ks_ref, vs_ref, kw_ref, vw_ref,
                    gate_ref, *, seq_len):
    hn = _rms(x_ref[...], g_ref[...]).astype(BF16)

    def proj(a, b):
        return _dot(hn, w_ref[:, a:b])

    dk = NSA_HEAD_DIM
    pq = proj(_C_Q, _C_KV) * (dk ** -0.5 * LOG2_E)
    for g in range(NSA_GROUPS):
        for r in range(NSA_REP):
            c = (g * NSA_REP + r) * dk
            q_ref[0, g, r] = pq[:, c:c + dk].astype(BF16)
    pk = proj(_C_KV, _C_GATE)
    tm = x_ref.shape[0]
    blk_w = ks_ref.shape[3] - 2 * dk
    pos = (pl.program_id(0) % (seq_len // tm)) * tm + lax.broadcasted_iota(I32, (tm, blk_w), 0)
    own_block = jnp.where(pos // SLC_BLOCK == lax.broadcasted_iota(I32, (tm, blk_w), 1),
                          -1.0, 0.0).astype(BF16)
    ones_col = jnp.where(lax.broadcasted_iota(I32, (tm, LANES - dk), 1) == 0, 1.0, 0.0).astype(BF16)
    for g in range(NSA_GROUPS):
        kvc_ref[0, 0, g] = pk[:, 0 * 128 + g * dk:0 * 128 + (g + 1) * dk]
        kvc_ref[1, 0, g] = pk[:, 1 * 128 + g * dk:1 * 128 + (g + 1) * dk]
        ks_ref[0, g, :, :blk_w] = own_block
        ks_ref[0, g, :, blk_w:blk_w + dk] = (
            pk[:, 2 * 128 + g * dk:2 * 128 + (g + 1) * dk].astype(BF16))
        ks_ref[0, g, :, blk_w + dk:] = jnp.zeros((tm, dk), BF16)
        vs_ref[0, g, :, :dk] = pk[:, 3 * 128 + g * dk:3 * 128 + (g + 1) * dk].astype(BF16)
        vs_ref[0, g, :, dk:] = ones_col
        kw_ref[0, g] = pk[:, 4 * 128 + g * dk:4 * 128 + (g + 1) * dk].astype(BF16)
        vw_ref[0, g, :, :dk] = pk[:, 5 * 128 + g * dk:5 * 128 + (g + 1) * dk].astype(BF16)
        vw_ref[0, g, :, dk:] = ones_col
        gate_ref[0, g] = _sigmoid(proj(_C_GATE + g * 128, _C_GATE + (g + 1) * 128))


def _in_proj(x2, g_mix, w_re, B, S):
    T, D = x2.shape
    tm = PROJ_ROWS
    per_b = S // tm
    G, R, dk = NSA_GROUPS, NSA_REP, NSA_HEAD_DIM
    blk_w = -(-(S // SLC_BLOCK) // LANES) * LANES

    def tok(i):
        return (i, 0)

    def bgs(i):
        return (i // per_b, 0, i % per_b, 0)

    out_shape = (
        jax.ShapeDtypeStruct((B, G, R, S, dk), BF16),
        jax.ShapeDtypeStruct((2, B, G, S, dk), F32),
        jax.ShapeDtypeStruct((B, G, S, blk_w + 2 * dk), BF16),
        jax.ShapeDtypeStruct((B, G, S, LANES), BF16),
        jax.ShapeDtypeStruct((B, G, S, dk), BF16),
        jax.ShapeDtypeStruct((B, G, S, LANES), BF16),
        jax.ShapeDtypeStruct((B, G, S, LANES), F32),
    )
    kv_spec = pl.BlockSpec((1, G, tm, dk), bgs)
    out_specs = (
        pl.BlockSpec((1, G, R, tm, dk), lambda i: (i // per_b, 0, 0, i % per_b, 0)),
        pl.BlockSpec((2, 1, G, tm, dk), lambda i: (0, i // per_b, 0, i % per_b, 0)),
        pl.BlockSpec((1, G, tm, blk_w + 2 * dk), bgs),
        pl.BlockSpec((1, G, tm, LANES), bgs),
        kv_spec,
        pl.BlockSpec((1, G, tm, LANES), bgs),
        pl.BlockSpec((1, G, tm, LANES), bgs),
    )
    return pl.pallas_call(
        functools.partial(_in_proj_kernel, seq_len=S),
        grid=(T // tm,),
        in_specs=[
            pl.BlockSpec((tm, D), tok),
            pl.BlockSpec((1, D), lambda i: (0, 0)),
            pl.BlockSpec((D, _C_U), lambda i: (0, 0)),
        ],
        out_specs=out_specs,
        out_shape=out_shape,
        compiler_params=_params(1),
        name="in_proj",
    )(x2, g_mix.reshape(1, D), w_re)


def _compress_kernel(x_ref, pe_ref, w1_ref, b1_ref, w2_ref, o_ref):
    x = x_ref[0, 0]
    half = x.shape[1]
    lo = (x + pe_ref[0, 0:1, :]).astype(BF16)
    hi = (x + pe_ref[0, 1:2, :]).astype(BF16)
    p = _dot(lo, w1_ref[0, :half, :])
    q = _dot(hi, w1_ref[0, half:, :])
    n = x.shape[0]
    h = p + pltpu.roll(q, n - 1, 0) + b1_ref[0]
    o_ref[0, 0] = _dot(_gelu(h).astype(BF16), w2_ref[0]).astype(BF16)


def _compress(kvc, pe, w1, b1, w2):
    _, BG, n_chunks, width = kvc.shape
    hidden = w1.shape[-1]
    dk = w2.shape[-1]
    return pl.pallas_call(
        _compress_kernel,
        grid=(2, BG),
        in_specs=[
            pl.BlockSpec((1, 1, n_chunks, width), lambda a, b: (a, b, 0, 0)),
            pl.BlockSpec((1, 2, width), lambda a, b: (a, 0, 0)),
            pl.BlockSpec((1, 2 * width, hidden), lambda a, b: (a, 0, 0)),
            pl.BlockSpec((1, 1, hidden), lambda a, b: (a, 0, 0)),
            pl.BlockSpec((1, hidden, dk), lambda a, b: (a, 0, 0)),
        ],
        out_specs=pl.BlockSpec((1, 1, n_chunks, dk), lambda a, b: (a, b, 0, 0)),
        out_shape=jax.ShapeDtypeStruct((2, BG, n_chunks, dk), BF16),
        compiler_params=_params(2),
        name="compress",
    )(kvc, pe, w1, b1, w2)


def _select_blocks(score_t):
    n = score_t.shape[0]
    row = lax.broadcasted_iota(I32, score_t.shape, 0).astype(F32)
    s = score_t
    for _ in range(SLC_TOPN):
        m = jnp.max(s, axis=0, keepdims=True)
        first = jnp.min(jnp.where(s == m, row, float(n)), axis=0, keepdims=True)
        s = jnp.where(row == first, -3.0e38, s)
    return jnp.where((s < -2.0e38) & (score_t > 0.5 * NEG_INF), 1.0, 0.0)


def _nsa_kernel(q_ref, ksa_ref, vsa_ref, kw_ref, vw_ref, kc_ref, vc_ref, gate_ref, ovt_ref, o_ref,
                kmax_scr, *, q_base):
    qb = pl.program_id(2)
    q0 = q_base + qb * NSA_QUERIES
    R, Q, dk = NSA_REP, NSA_QUERIES, NSA_HEAD_DIM
    q2 = q_ref[0, 0].reshape(R * Q, dk)

    kc = kc_ref[0, 0]
    n_cmp = kc.shape[0]
    s_c = _dot_nt(q2, kc)
    t_c = q0 + lax.broadcasted_iota(I32, (Q, n_cmp), 0)
    cmp_end = lax.broadcasted_iota(I32, (Q, n_cmp), 1) * CMP_STRIDE + (CMP_BLOCK - 1)
    mask_c = cmp_end <= t_c
    o_c = []
    p_sum = jnp.zeros((Q, n_cmp), F32)
    for r in range(R):
        s_r = jnp.where(mask_c, s_c[r * Q:(r + 1) * Q], NEG_INF)
        e = jnp.where(mask_c, jnp.exp2(s_r - jnp.max(s_r, axis=-1, keepdims=True)), 0.0)
        inv = 1.0 / jnp.maximum(jnp.sum(e, axis=-1, keepdims=True), TINY)
        p_sum = p_sum + e * inv
        o_c.append(_dot(e.astype(BF16), vc_ref[0, 0]) * inv)

    span = WINDOW + Q
    start = pl.multiple_of(jnp.maximum(q0 - WINDOW, 0), Q)
    kw = kw_ref[0, 0, pl.ds(start, span), :]
    vw = vw_ref[0, 0, pl.ds(start, span), :]
    s_w = _dot_nt(q2, kw)
    pos = start + lax.broadcasted_iota(I32, (Q, span), 1)
    diff = q0 + lax.broadcasted_iota(I32, (Q, span), 0) - pos
    mask_w = (diff >= 0) & (diff < WINDOW)
    o_w = []
    for r in range(R):
        s_r = jnp.where(mask_w, s_w[r * Q:(r + 1) * Q], NEG_INF)
        e = jnp.exp2(s_r - jnp.max(s_r, axis=-1, keepdims=True))
        o = _dot(e.astype(BF16), vw)
        o_w.append(o[:, :dk] / jnp.maximum(o[:, dk:dk + 1], TINY))

    n_slc = ovt_ref.shape[0]
    ovt = ovt_ref[...]
    p_hi = p_sum.astype(BF16)
    rest = p_sum - p_hi.astype(F32)
    p_mid = rest.astype(BF16)
    p_lo = (rest - p_mid.astype(F32)).astype(BF16)
    imp_t = _dot_nt(ovt, p_hi) + _dot_nt(ovt, p_mid) + _dot_nt(ovt, p_lo)
    t_b = q0 + lax.broadcasted_iota(I32, (n_slc, Q), 1)
    blk = lax.broadcasted_iota(I32, (n_slc, Q), 0)
    allowed = blk * SLC_BLOCK <= t_b
    cur = t_b // SLC_BLOCK
    forced = (blk == 0) | (blk == cur) | (blk == cur - 1)
    score_t = jnp.where(forced & allowed, FORCE_BONUS, jnp.where(allowed, imp_t, NEG_INF))
    sel_t = _select_blocks(score_t)

    tk = SEL_KEYS
    blk_w = ksa_ref.shape[3] - 2 * dk

    @pl.when(qb == 0)
    def _():
        k_all = ksa_ref[0, 0, :, blk_w:blk_w + dk].astype(F32)
        k_sq = jnp.max(jnp.sum(k_all * k_all, axis=-1, keepdims=True), axis=0, keepdims=True)
        kmax_scr[...] = jnp.broadcast_to(jnp.sqrt(k_sq), kmax_scr.shape)

    q_f = q2.astype(F32)
    q_sq = jnp.max(jnp.sum(q_f * q_f, axis=-1, keepdims=True), axis=0, keepdims=True)
    drop = (2.0 * jnp.sqrt(q_sq) * kmax_scr[0:1, 0:1] + MASK_MARGIN) * 1.02
    if blk_w > n_slc:
        sel_t = jnp.concatenate([sel_t, jnp.zeros((blk_w - n_slc, Q), F32)], axis=0)
    mask_cols = ((1.0 - sel_t.T) * drop).astype(BF16)
    q_aug = jnp.concatenate(
        [jnp.concatenate([mask_cols] * R, axis=0), q2, jnp.zeros((R * Q, dk), BF16)], axis=1)
    last = q0 // tk
    va_w = vsa_ref.shape[3]

    def sel_tile(kt, carry, causal):
        ms, accs = carry
        base = pl.multiple_of(kt * tk, tk)
        s = _dot_nt(q_aug, ksa_ref[0, 0, pl.ds(base, tk), :])
        v_t = vsa_ref[0, 0, pl.ds(base, tk), :]
        if causal:
            ahead = (base + lax.broadcasted_iota(I32, (Q, tk), 1)
                     > q0 + lax.broadcasted_iota(I32, (Q, tk), 0))
        new_m, new_acc = [], []
        for r in range(R):
            s_r = s[r * Q:(r + 1) * Q]
            if causal:
                s_r = jnp.where(ahead, NEG_INF, s_r)
            m_new = jnp.maximum(ms[r], jnp.max(s_r, axis=-1, keepdims=True))
            alpha = jnp.exp2(ms[r] - m_new)
            p = jnp.exp2(s_r - m_new)
            new_m.append(m_new)
            new_acc.append(alpha * accs[r] + _dot(p.astype(BF16), v_t))
        return tuple(new_m), tuple(new_acc)

    init = (tuple(jnp.full((Q, 1), NEG_INF, F32) for _ in range(R)),
            tuple(jnp.zeros((Q, va_w), F32) for _ in range(R)))
    carry = lax.fori_loop(0, last, functools.partial(sel_tile, causal=False), init)
    _, accs = sel_tile(last, carry, causal=True)
    o_s = [accs[r][:, :dk] / jnp.maximum(accs[r][:, dk:dk + 1], TINY) for r in range(R)]

    gate = gate_ref[0, 0]
    for r in range(R):
        out = (gate[:, 3 * r:3 * r + 1] * o_c[r] + gate[:, 3 * r + 1:3 * r + 2] * o_s[r]
               + gate[:, 3 * r + 2:3 * r + 3] * o_w[r])
        o_ref[0, :, r * dk:(r + 1) * dk] = out.astype(BF16)


def _nsa_span(q, ksa, vsa, kw, vw, kcmp, vcmp, gates, overlap_t, q_base, q_count, prev):
    B, G, R, S, dk = q.shape
    prefix = q_base + q_count
    n_cmp = -(-(prefix // CMP_STRIDE) // LANES) * LANES
    n_cmp = min(n_cmp, kcmp.shape[2])
    n_slc = prefix // SLC_BLOCK
    first = q_base // NSA_QUERIES
    kv_spec = pl.BlockSpec((1, 1, prefix, dk), lambda b, g, i: (b, g, 0, 0))
    cmp_spec = pl.BlockSpec((1, 1, n_cmp, dk), lambda b, g, i: (b, g, 0, 0))
    in_specs = [
        pl.BlockSpec((1, 1, R, NSA_QUERIES, dk), lambda b, g, i: (b, g, 0, first + i, 0)),
        pl.BlockSpec((1, 1, prefix, ksa.shape[3]), lambda b, g, i: (b, g, 0, 0)),
        pl.BlockSpec((1, 1, prefix, vsa.shape[3]), lambda b, g, i: (b, g, 0, 0)),
        kv_spec,
        pl.BlockSpec((1, 1, prefix, vw.shape[3]), lambda b, g, i: (b, g, 0, 0)),
        cmp_spec, cmp_spec,
        pl.BlockSpec((1, 1, NSA_QUERIES, LANES), lambda b, g, i: (b, g, first + i, 0)),
        pl.BlockSpec((n_slc, n_cmp), lambda b, g, i: (0, 0)),
    ]
    operands = [q, ksa, vsa, kw, vw, kcmp, vcmp, gates, overlap_t[:n_slc, :n_cmp]]
    n_in = len(operands)

    def body(*refs):
        _nsa_kernel(*refs[:n_in], *refs[len(refs) - 2:], q_base=q_base)

    aliases = {}
    if prev is not None:
        in_specs.append(pl.BlockSpec(memory_space=pl.ANY))
        operands.append(prev)
        aliases = {n_in: 0}
    return pl.pallas_call(
        body,
        grid=(B, G, q_count // NSA_QUERIES),
        in_specs=in_specs,
        out_specs=pl.BlockSpec((1, NSA_QUERIES, R * dk), lambda b, g, i: (b, first + i, g)),
        out_shape=jax.ShapeDtypeStruct((B, S, G * R * dk), BF16),
        scratch_shapes=[pltpu.VMEM((SUBLANES, LANES), F32)],
        input_output_aliases=aliases,
        compiler_params=_params(3),
        name="nsa",
    )(*operands)


def _nsa(q, ksa, vsa, kw, vw, kcmp, vcmp, gates, overlap_t):
    S = q.shape[3]
    span = min(NSA_SPAN, S)
    out = None
    for base in range(0, S, span):
        out = _nsa_span(q, ksa, vsa, kw, vw, kcmp, vcmp, gates, overlap_t, base, span, out)
    return out


def _mix_kernel(x_ref, g_ref, wuvm_ref, onsa_ref, lng_ref, lnb_ref, ws_ref, bs_ref,
                wn_ref, wg_ref, wm_ref, gx_ref, wq_ref, k_ref, v_ref, wo_ref, h_ref):
    D = x_ref.shape[1]
    W = lng_ref.shape[1]
    hn = _rms(x_ref[...], g_ref[...]).astype(BF16)
    u = _gelu(_dot(hn, wuvm_ref[:, :W]))
    v = _gelu(_dot(hn, wuvm_ref[:, W:2 * W]))
    mu = jnp.mean(v, axis=-1, keepdims=True)
    var = jnp.mean(jnp.square(v - mu), axis=-1, keepdims=True)
    vn = ((v - mu) * lax.rsqrt(var + NORM_EPS) * lng_ref[...] + lnb_ref[...]).astype(BF16)
    C = GMLP_CHUNK
    gd = vn.shape[1] // GMLP_GROUPS
    rows = []
    for c in range(vn.shape[0] // C):
        cols = []
        for g in range(GMLP_GROUPS):
            cols.append(_dot(ws_ref[g], vn[c * C:(c + 1) * C, g * gd:(g + 1) * gd]) + bs_ref[g])
        rows.append(jnp.concatenate(cols, axis=1))
    o_gmlp = u * jnp.concatenate(rows, axis=0)
    mg = _sigmoid(_dot(hn, wuvm_ref[:, 2 * W:]))
    y = (mg[:, :D] * _dot(onsa_ref[...], wn_ref[...])
         + mg[:, D:] * _dot(o_gmlp.astype(BF16), wg_ref[...]))
    h = x_ref[...] + _dot(y.astype(BF16), wm_ref[...])
    h_ref[...] = _xattn_block(h, gx_ref, wq_ref, k_ref, v_ref, wo_ref)


def _mix(x2, g_mix, w_uvm, o_nsa, ln_g, ln_b, ws_causal, bs_b, w_nsa_out, w_gmlp_out, w_mix_out,
         g_xattn, w_xq, k_mem, v_mem, w_xo, S):
    T, D = x2.shape
    tm = MIX_ROWS
    W = ln_g.shape[0]
    per_b = S // tm
    mem_spec = pl.BlockSpec((1, k_mem.shape[1], D), lambda i: (i // per_b, 0, 0))

    def tok(i):
        return (i, 0)

    def const2(i):
        return (0, 0)

    def const3(i):
        return (0, 0, 0)

    return pl.pallas_call(
        _mix_kernel,
        grid=(T // tm,),
        in_specs=[
            pl.BlockSpec((tm, D), tok),
            pl.BlockSpec((1, D), const2),
            pl.BlockSpec(w_uvm.shape, const2),
            pl.BlockSpec((tm, o_nsa.shape[1]), tok),
            pl.BlockSpec((1, W), const2),
            pl.BlockSpec((1, W), const2),
            pl.BlockSpec(ws_causal.shape, const3),
            pl.BlockSpec(bs_b.shape, const3),
            pl.BlockSpec(w_nsa_out.shape, const2),
            pl.BlockSpec(w_gmlp_out.shape, const2),
            pl.BlockSpec(w_mix_out.shape, const2),
            pl.BlockSpec((1, D), const2),
            pl.BlockSpec((D, D), const2),
            mem_spec, mem_spec,
            pl.BlockSpec((D, D), const2),
        ],
        out_specs=pl.BlockSpec((tm, D), tok),
        out_shape=jax.ShapeDtypeStruct((T, D), F32),
        compiler_params=_params(1),
        name="mix",
    )(x2, g_mix.reshape(1, D), w_uvm, o_nsa, ln_g.reshape(1, W), ln_b.reshape(1, W), ws_causal,
      bs_b, w_nsa_out, w_gmlp_out, w_mix_out, g_xattn.reshape(1, D), w_xq, k_mem, v_mem, w_xo)


def _mem_kv_kernel(mem_ref, g_ref, w_ref, k_ref, v_ref):
    D = mem_ref.shape[2]
    mn = _rms(mem_ref[0], g_ref[...]).astype(BF16)
    kv = _dot(mn, w_ref[...])
    k_ref[0] = kv[:, :D].astype(BF16)
    v_ref[0] = kv[:, D:].astype(BF16)


def _mem_kv(mem, g_mem, w_xkv):
    B, M, D = mem.shape
    spec = pl.BlockSpec((1, M, D), lambda b: (b, 0, 0))
    return pl.pallas_call(
        _mem_kv_kernel,
        grid=(B,),
        in_specs=[spec, pl.BlockSpec((1, D), lambda b: (0, 0)),
                  pl.BlockSpec((D, 2 * D), lambda b: (0, 0))],
        out_specs=(spec, spec),
        out_shape=(jax.ShapeDtypeStruct((B, M, D), BF16), jax.ShapeDtypeStruct((B, M, D), BF16)),
        compiler_params=_params(1),
        name="mem_kv",
    )(mem, g_mem.reshape(1, D), w_xkv)


def _xattn_block(h, g_ref, wq_ref, k_ref, v_ref, wo_ref):
    D = h.shape[1]
    hd = D // XATTN_HEADS
    hn = _rms(h, g_ref[...]).astype(BF16)
    q = _dot(hn, wq_ref[...]) * (hd ** -0.5)
    outs = []
    for a in range(XATTN_HEADS):
        s = _dot_nt(q[:, a * hd:(a + 1) * hd].astype(BF16), k_ref[0, :, a * hd:(a + 1) * hd])
        m = jnp.max(s, axis=-1, keepdims=True)
        e = jnp.exp(s - m)
        p = e / jnp.sum(e, axis=-1, keepdims=True)
        outs.append(_dot(p.astype(BF16), v_ref[0, :, a * hd:(a + 1) * hd]))
    o = jnp.concatenate(outs, axis=1).astype(BF16)
    return h + _dot(o, wo_ref[...])


def _top_rows(vals, k, row=None):
    if row is None:
        row = lax.broadcasted_iota(I32, vals.shape, 0).astype(F32)
    top_v, top_i = [], []
    for _ in range(k):
        m = jnp.max(vals, axis=0, keepdims=True)
        first = jnp.min(jnp.where(vals == m, row, 1.0e9), axis=0, keepdims=True)
        top_v.append(m)
        top_i.append(first)
        vals = jnp.where(row == first, -3.0e38, vals)
    return top_v, top_i


def _pair_candidates(s0_all, s1_all, K):
    L, n_tok = s0_all.shape
    sub = lax.broadcasted_iota(I32, (SUBLANES, n_tok), 0).astype(F32)
    vals, ids = [], []
    a = 0
    while a < L and min(K // (a + 1), L) > 1:
        nb = min(K // (a + 1), L)
        for b0 in range(0, nb, SUBLANES):
            v = s0_all[a:a + 1] + s1_all[b0:b0 + SUBLANES]
            vals.append(v if b0 + SUBLANES <= nb else jnp.where(sub < float(nb - b0), v, -3.0e38))
            ids.append(sub + float(a * K + b0))
        a += 1
    while a < L:
        vals.append(s0_all[a:a + SUBLANES] + s1_all[0:1])
        ids.append((sub + float(a)) * float(K))
        a += SUBLANES
    return jnp.concatenate(vals, axis=0), jnp.concatenate(ids, axis=0)


def _route_from_scores(sc0, sc1):
    K = PEER_TOPK
    n_tok = sc0.shape[1]
    s0, i0 = _top_rows(sc0, K)
    s1, i1 = _top_rows(sc1, K)
    s0_all = jnp.concatenate(s0, axis=0)
    s1_all = jnp.concatenate(s1, axis=0)
    i0_all = jnp.concatenate(i0, axis=0)
    i1_all = jnp.concatenate(i1, axis=0)
    cand, cand_id = _pair_candidates(s0_all, s1_all, K)
    best_s, pos = _top_rows(cand, K, cand_id)
    best = jnp.concatenate(best_s, axis=0)
    m = jnp.max(best, axis=0, keepdims=True)
    e = jnp.exp(best - m)
    gate = e / jnp.sum(e, axis=0, keepdims=True)
    rank = lax.broadcasted_iota(I32, (K, n_tok), 0).astype(F32)
    ei, ej = [], []
    for kk in range(K):
        a = jnp.floor(pos[kk] * (1.0 / K))
        b = pos[kk] - a * K
        ei.append(jnp.sum(jnp.where(rank == a, i0_all, 0.0), axis=0, keepdims=True))
        ej.append(jnp.sum(jnp.where(rank == b, i1_all, 0.0), axis=0, keepdims=True))
    return jnp.concatenate(ei, axis=0), jnp.concatenate(ej, axis=0), gate


def _route_head(qt_scr, sk_ref, hd):
    half_dim = sk_ref.shape[2]
    scores = []
    for p in range(2):
        off = pl.multiple_of((hd * 2 + p) * half_dim, half_dim)
        qs = qt_scr[pl.ds(off, half_dim), :].astype(BF16)
        scores.append(_dot(sk_ref[hd * 2 + p], qs))
    return scores


def _route_kernel(h_ref, g_ref, wq_ref, sk_ref, hn_ref, i_ref, j_ref, gate_ref,
                  qt_scr, i_scr, j_scr, gate_scr):
    K = PEER_TOPK
    hn = _rms(h_ref[...], g_ref[...]).astype(BF16)
    hn_ref[...] = hn
    qt_scr[...] = _dot_nt(wq_ref[...], hn)

    def head(hd, _):
        sc0, sc1 = _route_head(qt_scr, sk_ref, hd)
        row0 = pl.multiple_of(hd * K, K)
        ei, ej, gate = _route_from_scores(sc0, sc1)
        i_scr[pl.ds(row0, K), :] = ei
        j_scr[pl.ds(row0, K), :] = ej
        gate_scr[pl.ds(row0, K), :] = gate
        return 0

    lax.fori_loop(0, PEER_HEADS, head, 0)
    i_ref[...] = i_scr[...].T.astype(I32)
    j_ref[...] = j_scr[...].T.astype(I32)
    gate_ref[...] = gate_scr[...].T


def _route(h, g_peer, wq_t, sub_keys):
    T, D = h.shape
    tt = ROUTE_TOKENS
    n_sel = PEER_HEADS * PEER_TOPK
    tok = lambda i: (i, 0)
    return pl.pallas_call(
        _route_kernel,
        grid=(T // tt,),
        in_specs=[
            pl.BlockSpec((tt, D), tok),
            pl.BlockSpec((1, D), lambda i: (0, 0)),
            pl.BlockSpec(wq_t.shape, lambda i: (0, 0)),
            pl.BlockSpec(sub_keys.shape, lambda i: (0, 0, 0)),
        ],
        out_specs=(
            pl.BlockSpec((tt, D), tok),
            pl.BlockSpec((tt, n_sel), tok),
            pl.BlockSpec((tt, n_sel), tok),
            pl.BlockSpec((tt, n_sel), tok),
        ),
        out_shape=(
            jax.ShapeDtypeStruct((T, D), BF16),
            jax.ShapeDtypeStruct((T, n_sel), I32),
            jax.ShapeDtypeStruct((T, n_sel), I32),
            jax.ShapeDtypeStruct((T, n_sel), F32),
        ),
        scratch_shapes=[
            pltpu.VMEM((wq_t.shape[0], tt), F32),
            pltpu.VMEM((n_sel, tt), F32),
            pltpu.VMEM((n_sel, tt), F32),
            pltpu.VMEM((n_sel, tt), F32),
        ],
        compiler_params=_params(1),
        name="route",
    )(h, g_peer.reshape(1, D), wq_t, sub_keys)


def _peer_up_kernel(hn_ref, u_ref, i_ref, j_ref, a_ref):
    e = pl.program_id(1)

    @pl.when(e == 0)
    def _():
        a_ref[...] = jnp.zeros(a_ref.shape, F32)

    hn = hn_ref[...]
    ii = i_ref[...]
    jj = j_ref[...]
    acc = a_ref[...]
    for r0 in range(0, EXPERT_ROWS, 2):
        u_pair = u_ref[r0 * PEER_KEYS:(r0 + 2) * PEER_KEYS, :].astype(BF16)
        z = _dot_nt(hn, u_pair)
        for r in range(r0, r0 + 2):
            picked = jnp.take_along_axis(
                z[:, (r - r0) * PEER_KEYS:(r - r0 + 1) * PEER_KEYS], jj, axis=1)
            acc = jnp.where(ii == e * EXPERT_ROWS + r, picked, acc)
    a_ref[...] = acc


def _peer_up(hn, u_bf, i_idx, j_idx):
    T, D = hn.shape
    tt = UP_TOKENS
    n_sel = i_idx.shape[1]
    rows = EXPERT_ROWS * PEER_KEYS
    tok = lambda t, e: (t, 0)
    return pl.pallas_call(
        _peer_up_kernel,
        grid=(T // tt, u_bf.shape[0] // rows),
        in_specs=[
            pl.BlockSpec((tt, D), tok),
            pl.BlockSpec((rows, D), lambda t, e: (e, 0)),
            pl.BlockSpec((tt, n_sel), tok),
            pl.BlockSpec((tt, n_sel), tok),
        ],
        out_specs=pl.BlockSpec((tt, n_sel), tok),
        out_shape=jax.ShapeDtypeStruct((T, n_sel), F32),
        compiler_params=_params(2),
        name="peer_up",
    )(hn, u_bf, i_idx, j_idx)


def _peer_down_kernel(a_ref, gate_ref, i_ref, j_ref, v_ref, h_ref, g_ref, o_ref,
                      w_scr, coef_scr, acc_scr):
    e = pl.program_id(1)
    n_tok = a_ref.shape[0]
    nk = PEER_KEYS

    @pl.when(e == 0)
    def _():
        coef_scr[...] = gate_ref[...] * _gelu(a_ref[...])
        acc_scr[...] = jnp.zeros(acc_scr.shape, F32)
        sub = lax.broadcasted_iota(I32, (nk, i_ref.shape[1]), 0)

        def expand(t, _):
            ii = i_ref[pl.ds(t, 1), :]
            jj = j_ref[pl.ds(t, 1), :]
            cc = coef_scr[pl.ds(t, 1), :]
            a = jnp.where(sub == ii, cc, 0.0).astype(BF16)
            bt = jnp.where(sub == jj, 1.0, 0.0).astype(BF16)
            base = pl.multiple_of(t * W_ROW_STRIDE, SUBLANES)
            w_scr[pl.ds(base, nk), :] = _dot_nt(a, bt)
            return 0

        lax.fori_loop(0, n_tok, expand, 0, unroll=EXPAND_UNROLL)

    parts = []
    for r in range(DOWN_EXPERT_ROWS):
        row = e * DOWN_EXPERT_ROWS + r
        parts.append(w_scr[pl.ds(row, n_tok, stride=W_ROW_STRIDE), :].astype(BF16))
    lhs = jnp.concatenate(parts, axis=1)
    acc_scr[...] += _dot(lhs, v_ref[...])

    @pl.when(e == pl.num_programs(1) - 1)
    def _():
        o_ref[...] = _rms(h_ref[...] + acc_scr[...], g_ref[...])


def _peer_down(a_pre, gate, i_idx, j_idx, v_bf, h, g_final):
    T, D = h.shape
    tt = DOWN_TOKENS
    n_sel = i_idx.shape[1]
    rows = DOWN_EXPERT_ROWS * PEER_KEYS
    tok = lambda t, e: (t, 0)
    sel_spec = pl.BlockSpec((tt, n_sel), tok)
    return pl.pallas_call(
        _peer_down_kernel,
        grid=(T // tt, v_bf.shape[0] // rows),
        in_specs=[
            sel_spec, sel_spec, sel_spec, sel_spec,
            pl.BlockSpec((rows, D), lambda t, e: (e, 0)),
            pl.BlockSpec((tt, D), tok),
            pl.BlockSpec((1, D), lambda t, e: (0, 0)),
        ],
        out_specs=pl.BlockSpec((tt, D), tok),
        out_shape=jax.ShapeDtypeStruct((T, D), F32),
        scratch_shapes=[
            pltpu.VMEM((tt * W_ROW_STRIDE, PEER_KEYS), F32),
            pltpu.VMEM((tt, n_sel), F32),
            pltpu.VMEM((tt, D), F32),
        ],
        compiler_params=_params(2),
        name="peer_down",
    )(a_pre, gate, i_idx, j_idx, v_bf, h, g_final.reshape(1, D))


def _overlap_table(n_cmp, n_slc):
    c0 = jnp.arange(n_cmp) * CMP_STRIDE
    j0 = jnp.arange(n_slc) * SLC_BLOCK
    ov = jnp.clip(jnp.minimum(c0[:, None] + CMP_BLOCK, j0[None, :] + SLC_BLOCK)
                  - jnp.maximum(c0[:, None], j0[None, :]), 0, None)
    return ov.astype(F32) / CMP_BLOCK


def kernel(x, mem, g_mix, w_in, cmp_pe_k, cmp_w1_k, cmp_b1_k, cmp_w2_k, cmp_pe_v, cmp_w1_v, cmp_b1_v, cmp_w2_v, gmlp_ln_g, gmlp_ln_b, gmlp_ws, gmlp_bs, w_nsa_out, w_gmlp_out, w_mix_out, g_xattn, g_mem, w_xq, w_xkv, w_xo, g_peer, w_peer_q, peer_sub_keys, peer_u, peer_v, g_final):
    B, S, D = x.shape
    T = B * S
    G, R, dk = NSA_GROUPS, NSA_REP, NSA_HEAD_DIM
    x2 = x.reshape(T, D)

    n_gate = NSA_HEADS * 3
    gate_cols = w_in[:, 1280:1280 + n_gate]
    per_g = n_gate // G
    gate_blocks = [jnp.pad(gate_cols[:, g * per_g:(g + 1) * per_g], ((0, 0), (0, LANES - per_g)))
                   for g in range(G)]
    w_re = jnp.concatenate([w_in[:, :1280]] + gate_blocks + [w_in[:, 1280 + n_gate:]],
                           axis=1).astype(BF16)
    half = CMP_STRIDE * dk
    pe = jnp.stack([cmp_pe_k.reshape(2, half), cmp_pe_v.reshape(2, half)])
    w1 = jnp.stack([cmp_w1_k, cmp_w1_v]).astype(BF16)
    b1 = jnp.stack([cmp_b1_k, cmp_b1_v])[:, None, :]
    w2 = jnp.stack([cmp_w2_k, cmp_w2_v]).astype(BF16)
    C = GMLP_CHUNK
    ws_causal = (gmlp_ws * jnp.tril(jnp.ones((C, C), F32))).astype(BF16)
    group_dim = gmlp_ln_g.shape[0] // GMLP_GROUPS
    bs_b = jnp.broadcast_to(gmlp_bs[:, :, None], (GMLP_GROUPS, C, group_dim))
    n_chunks = S // CMP_STRIDE
    overlap = _overlap_table(n_chunks, S // SLC_BLOCK)

    q, kvc, ksa, vsa, kw, vw, gates = _in_proj(x2, g_mix, w_re[:, :_C_U], B, S)
    cmp_kv = _compress(kvc.reshape(2, B * G, n_chunks, half), pe, w1, b1, w2)
    cmp_kv = cmp_kv.reshape(2, B, G, n_chunks, dk)
    o_nsa = _nsa(q, ksa, vsa, kw, vw, cmp_kv[0], cmp_kv[1], gates, overlap.T.astype(BF16))
    k_mem, v_mem = _mem_kv(mem, g_mem, w_xkv.astype(BF16))
    h = _mix(x2, g_mix, w_re[:, _C_U:], o_nsa.reshape(T, G * R * dk), gmlp_ln_g, gmlp_ln_b,
             ws_causal, bs_b, w_nsa_out.astype(BF16), w_gmlp_out.astype(BF16),
             w_mix_out.astype(BF16), g_xattn, w_xq.astype(BF16), k_mem, v_mem,
             w_xo.astype(BF16), S)

    half_dim = peer_sub_keys.shape[3]
    sk = peer_sub_keys.reshape(PEER_HEADS * 2, PEER_KEYS, half_dim).astype(BF16)
    hn, i_idx, j_idx, gate = _route(h, g_peer, w_peer_q.T.astype(BF16), sk)
    a_pre = _peer_up(hn, peer_u, i_idx, j_idx)
    out = _peer_down(a_pre, gate, i_idx, j_idx, peer_v.astype(BF16), h, g_final)
    return out.reshape(B, S, D)
```
